```python
import math
import jax, jax.numpy as jnp
from jax import lax
import numpy as np

D_MODEL = 2048
BATCH = 2
SEQ = 4096
DEPTH = 4
DEC_BATCH = 8
DEC_SEQ = 1
PAST_LEN = 16384
PAGE_SIZE = 128

N_A_LAYERS = DEPTH // 2
N_B_LAYERS = DEPTH - N_A_LAYERS
N_DENSE = (DEPTH + 1) // 2
N_MOE = DEPTH // 2
CONV_W = 3
HEAD_DIM = 128
N_HEADS = D_MODEL // (2 * HEAD_DIM)
V_DIM = 2 * HEAD_DIM
Q_WIDTH = N_HEADS * 2 * HEAD_DIM
K_WIDTH = N_HEADS * 2 * HEAD_DIM
KV_WIDTH = K_WIDTH + N_HEADS * V_DIM
D_FF = ((8 * D_MODEL // 3 + 255) // 256) * 256
N_EXPERTS = 8
TOP_K = 2
D_FF_EXPERT = 7 * D_MODEL // 2
MOE_BLOCK = 128
Q_BLOCK = 128
EPS = 1e-6

kernel_name = 'yoco_shortconv_diffattn_moe_step'


def rmsnorm(x, g):
    xf = x.astype(jnp.float32)
    y = xf * lax.rsqrt(jnp.mean(xf * xf, axis=-1, keepdims=True) + EPS)
    return (y * g.astype(jnp.float32)).astype(x.dtype)


def alibi_slopes():
    return jnp.exp2(-8.0 * jnp.arange(1, N_HEADS + 1, dtype=jnp.float32) / N_HEADS)


def lambda_init(layer):
    return 0.8 - 0.6 * math.exp(-0.3 * layer)


def short_conv_mixer(h, prev, w_in, w_conv, w_out):
    T = h.shape[1]
    bcv = h @ w_in
    b, c, v = jnp.split(bcv, 3, axis=-1)
    u = c * v
    up = jnp.concatenate([prev.astype(u.dtype), u], axis=1)
    conv = sum(w_conv[j] * up[:, j:j + T] for j in range(CONV_W))
    y = (b * conv) @ w_out
    return y, up[:, -(CONV_W - 1):]


def swiglu(h, w_gate, w_up, w_down):
    return (jax.nn.silu(h @ w_gate) * (h @ w_up)) @ w_down


def moe_swiglu(x, w_router, w_gate, w_up, w_down):
    N, D = x.shape
    NK = N * TOP_K
    logits = jnp.einsum('nd,de->ne', x, w_router, preferred_element_type=jnp.float32)
    top_val, top_idx = lax.top_k(logits, TOP_K)
    gates = jax.nn.softmax(top_val, axis=-1)
    flat_e = top_idx.reshape(-1).astype(jnp.int32)
    flat_tok = jnp.repeat(jnp.arange(N, dtype=jnp.int32), TOP_K)
    flat_g = gates.reshape(-1)
    order = jnp.argsort(flat_e)
    se, stok, sg = flat_e[order], flat_tok[order], flat_g[order]
    blk = min(MOE_BLOCK, NK)
    counts = jnp.bincount(flat_e, length=N_EXPERTS)
    padded = (counts + blk - 1) // blk * blk
    start_sorted = jnp.cumsum(counts) - counts
    end_padded = jnp.cumsum(padded)
    start_padded = end_padded - padded
    dest = start_padded[se] + jnp.arange(NK, dtype=jnp.int32) - start_sorted[se]
    n_blocks = -(-NK // blk) + N_EXPERTS
    P = n_blocks * blk
    buf_tok = jnp.zeros((P,), jnp.int32).at[dest].set(stok)
    buf_g = jnp.zeros((P,), jnp.float32).at[dest].set(sg)
    block_e = jnp.minimum(
        jnp.searchsorted(end_padded, jnp.arange(n_blocks) * blk, side='right'),
        N_EXPERTS - 1).astype(jnp.int32)
    xb = x[buf_tok].reshape(n_blocks, blk, D)

    def expert_block(args):
        xi, e = args
        return (jax.nn.silu(xi @ w_gate[e]) * (xi @ w_up[e])) @ w_down[e]

    yb = lax.map(expert_block, (xb, block_e)).reshape(P, D)
    contrib = (yb.astype(jnp.float32) * buf_g[:, None]).astype(x.dtype)
    return jnp.zeros_like(x).at[buf_tok].add(contrib)


def diff_attention(q, k, v, q_pos, k_pos, lam):
    B, T = q.shape[0], q.shape[1]
    blk = Q_BLOCK if T % Q_BLOCK == 0 else T
    nb = T // blk
    qb = jnp.moveaxis(q.reshape(B, nb, blk, N_HEADS, 2, HEAD_DIM), 1, 0)
    pb = q_pos.reshape(nb, blk)
    slopes = alibi_slopes()[:, None, None]

    def one_block(args):
        qi, pi = args
        s = jnp.einsum('bqhcd,bkhcd->bchqk', qi, k, preferred_element_type=jnp.float32)
        dist = (pi[:, None] - k_pos[None, :]).astype(jnp.float32)
        s = jnp.where(dist >= 0, s - slopes * dist, -jnp.inf)
        p = jax.nn.softmax(s, axis=-1)
        w = p[:, 0] - lam * p[:, 1]
        return jnp.einsum('bhqk,bkhe->bqhe', w.astype(v.dtype), v)

    out = lax.map(one_block, (qb, pb))
    return jnp.moveaxis(out, 0, 1).reshape(B, T, N_HEADS, V_DIM)


def trunk(x, conv_prev, past_k, past_v, p):
    B, T, _ = x.shape
    P = past_k.shape[1]
    q_pos = P + jnp.arange(T, dtype=jnp.int32)
    k_pos = jnp.arange(P + T, dtype=jnp.int32)
    new_conv = []
    k_new = v_new = k_all = v_all = None
    for l in range(DEPTH):
        h = rmsnorm(x, p['mix_norm'][l])
        if l < N_A_LAYERS:
            y, st = short_conv_mixer(h, conv_prev[l], p['a_w_in'][l], p['a_conv'][l], p['a_w_out'][l])
            new_conv.append(st)
        else:
            j = l - N_A_LAYERS
            if k_all is None:
                kv = rmsnorm(x, p['kv_norm']) @ p['w_kv']
                k_new = rmsnorm(kv[..., :K_WIDTH].reshape(B, T, N_HEADS, 2, HEAD_DIM), p['k_norm'])
                v_new = kv[..., K_WIDTH:].reshape(B, T, N_HEADS, V_DIM)
                k_all = jnp.concatenate([past_k.astype(k_new.dtype), k_new], axis=1)
                v_all = jnp.concatenate([past_v.astype(v_new.dtype), v_new], axis=1)
            q = (h @ p['b_w_q'][j]).reshape(B, T, N_HEADS, 2, HEAD_DIM)
            q = rmsnorm(q, p['b_q_norm'][j]) * (HEAD_DIM ** -0.5)
            lp = p['b_lam'][j].astype(jnp.float32)
            lam0 = lambda_init(l)
            lam = jnp.exp(jnp.sum(lp[0] * lp[1])) - jnp.exp(jnp.sum(lp[2] * lp[3])) + lam0
            o = diff_attention(q, k_all, v_all, q_pos, k_pos, lam)
            o = rmsnorm(o, p['b_subln'][j]) * (1.0 - lam0)
            y = o.reshape(B, T, N_HEADS * V_DIM) @ p['b_w_o'][j]
        x = x + y
        h2 = rmsnorm(x, p['ffn_norm'][l])
        if l % 2 == 0:
            i = l // 2
            f = swiglu(h2, p['ffn_w_gate'][i], p['ffn_w_up'][i], p['ffn_w_down'][i])
        else:
            i = l // 2
            f = moe_swiglu(h2.reshape(B * T, D_MODEL), p['moe_router'][i], p['moe_w_gate'][i],
                           p['moe_w_up'][i], p['moe_w_down'][i]).reshape(B, T, D_MODEL)
        x = x + f
    return x, jnp.stack(new_conv), k_new, v_new


def setup_inputs(seed: int = 0) -> dict:
    key = jax.random.key(seed)
    ks = jax.random.split(key, 40)
    f32 = jnp.float32
    N_PAGES = PAST_LEN // PAGE_SIZE
    n_used = DEC_BATCH * N_PAGES
    N_POOL = n_used + (n_used + 3) // 4

    def nrm(k, shape, scale):
        return jax.random.normal(k, shape, f32) * scale

    def gain(k, shape):
        return 1.0 + 0.05 * jax.random.normal(k, shape, f32)

    page_table = jax.random.permutation(ks[0], N_POOL)[:n_used].reshape(DEC_BATCH, N_PAGES).astype(jnp.int32)
    return {
        'x_prompt': nrm(ks[1], (BATCH, SEQ, D_MODEL), 1.0),
        'x_sample': nrm(ks[2], (DEC_BATCH, DEC_SEQ, D_MODEL), 1.0),
        'state_conv': nrm(ks[3], (N_A_LAYERS, DEC_BATCH, CONV_W - 1, D_MODEL), 1.0),
        'cache_k': nrm(ks[4], (N_POOL, PAGE_SIZE, N_HEADS, 2, HEAD_DIM), 1.0),
        'cache_v': nrm(ks[5], (N_POOL, PAGE_SIZE, N_HEADS, V_DIM), 1.0),
        'page_table': page_table,
        'mix_norm': gain(ks[6], (DEPTH, D_MODEL)),
        'ffn_norm': gain(ks[7], (DEPTH, D_MODEL)),
        'a_w_in': nrm(ks[8], (N_A_LAYERS, D_MODEL, 3 * D_MODEL), D_MODEL ** -0.5),
        'a_conv': nrm(ks[9], (N_A_LAYERS, CONV_W, D_MODEL), CONV_W ** -0.5),
        'a_w_out': nrm(ks[10], (N_A_LAYERS, D_MODEL, D_MODEL), D_MODEL ** -0.5),
        'kv_norm': gain(ks[11], (D_MODEL,)),
        'w_kv': nrm(ks[12], (D_MODEL, KV_WIDTH), D_MODEL ** -0.5),
        'k_norm': gain(ks[13], (2, HEAD_DIM)),
        'b_w_q': nrm(ks[14], (N_B_LAYERS, D_MODEL, Q_WIDTH), D_MODEL ** -0.5),
        'b_q_norm': gain(ks[15], (N_B_LAYERS, 2, HEAD_DIM)),
        'b_lam': nrm(ks[16], (N_B_LAYERS, 4, HEAD_DIM), 0.1),
        'b_subln': gain(ks[17], (N_B_LAYERS, V_DIM)),
        'b_w_o': nrm(ks[18], (N_B_LAYERS, N_HEADS * V_DIM, D_MODEL), (N_HEADS * V_DIM) ** -0.5),
        'ffn_w_gate': nrm(ks[19], (N_DENSE, D_MODEL, D_FF), D_MODEL ** -0.5),
        'ffn_w_up': nrm(ks[20], (N_DENSE, D_MODEL, D_FF), D_MODEL ** -0.5),
        'ffn_w_down': nrm(ks[21], (N_DENSE, D_FF, D_MODEL), D_FF ** -0.5),
        'moe_router': nrm(ks[22], (N_MOE, D_MODEL, N_EXPERTS), D_MODEL ** -0.5),
        'moe_w_gate': nrm(ks[23], (N_MOE, N_EXPERTS, D_MODEL, D_FF_EXPERT), D_MODEL ** -0.5),
        'moe_w_up': nrm(ks[24], (N_MOE, N_EXPERTS, D_MODEL, D_FF_EXPERT), D_MODEL ** -0.5),
        'moe_w_down': nrm(ks[25], (N_MOE, N_EXPERTS, D_FF_EXPERT, D_MODEL), D_FF_EXPERT ** -0.5),
    }


def reference(x_prompt, x_sample, state_conv, cache_k, cache_v, page_table,
              mix_norm, ffn_norm, a_w_in, a_conv, a_w_out, kv_norm, w_kv, k_norm,
              b_w_q, b_q_norm, b_lam, b_subln, b_w_o,
              ffn_w_gate, ffn_w_up, ffn_w_down,
              moe_router, moe_w_gate, moe_w_up, moe_w_down):
    p = dict(mix_norm=mix_norm, ffn_norm=ffn_norm, a_w_in=a_w_in, a_conv=a_conv,
             a_w_out=a_w_out, kv_norm=kv_norm, w_kv=w_kv, k_norm=k_norm,
             b_w_q=b_w_q, b_q_norm=b_q_norm, b_lam=b_lam, b_subln=b_subln, b_w_o=b_w_o,
             ffn_w_gate=ffn_w_gate, ffn_w_up=ffn_w_up, ffn_w_down=ffn_w_down,
             moe_router=moe_router, moe_w_gate=moe_w_gate, moe_w_up=moe_w_up,
             moe_w_down=moe_w_down)
    Bp = x_prompt.shape[0]
    conv0 = jnp.zeros((N_A_LAYERS, Bp, CONV_W - 1, D_MODEL), x_prompt.dtype)
    pk0 = jnp.zeros((Bp, 0, N_HEADS, 2, HEAD_DIM), x_prompt.dtype)
    pv0 = jnp.zeros((Bp, 0, N_HEADS, V_DIM), x_prompt.dtype)
    y_prompt, conv_prompt, k_prompt, v_prompt = trunk(x_prompt, conv0, pk0, pv0, p)
    Bs, n_pages = page_table.shape
    past_len = n_pages * cache_k.shape[1]
    past_k = cache_k[page_table].reshape(Bs, past_len, N_HEADS, 2, HEAD_DIM)
    past_v = cache_v[page_table].reshape(Bs, past_len, N_HEADS, V_DIM)
    y_sample, conv_sample, k_sample, v_sample = trunk(x_sample, state_conv, past_k, past_v, p)
    return (y_prompt, y_sample, conv_prompt, conv_sample, k_prompt, v_prompt, k_sample, v_sample)
```

```python
import functools
import math

import jax
import jax.numpy as jnp
from jax import lax
from jax.experimental import pallas as pl
from jax.experimental.pallas import tpu as pltpu

F32 = jnp.float32
BF16 = jnp.bfloat16
EPS = 1e-6
HEAD_DIM = 128
V_DIM = 2 * HEAD_DIM
TOP_K = 2
CONV_W = 3
LANE = 128
TAIL = 16
V7X_VMEM_BYTES = 64 * 1024 * 1024
VMEM_LIMIT = V7X_VMEM_BYTES - 8 * 1024 * 1024
NEG_INF = float("-inf")


def _params(*sem):
    return pltpu.CompilerParams(dimension_semantics=sem, vmem_limit_bytes=VMEM_LIMIT)


def _dot(a, b):
    return jnp.dot(a, b, preferred_element_type=F32)


def _dot_nt(a, b):
    return lax.dot_general(a, b, (((1,), (1,)), ((), ())), preferred_element_type=F32)


def _rms(x, g):
    ms = jnp.mean(x * x, axis=-1, keepdims=True)
    return x * lax.rsqrt(ms + EPS) * g


def _group_rms(x, gsize):
    outs = []
    for g in range(x.shape[-1] // gsize):
        blk = x[:, g * gsize:(g + 1) * gsize]
        ms = jnp.mean(blk * blk, axis=-1, keepdims=True)
        outs.append(blk * lax.rsqrt(ms + EPS))
    return outs[0] if len(outs) == 1 else jnp.concatenate(outs, axis=-1)


def _norm_body(*refs, nb, has_add):
    if has_add:
        xm, xt, ym, yt, g, om, ot, sm, st = refs
    else:
        xm, xt, g, om, ot = refs
        ym = yt = sm = st = None
    m = pl.program_id(0)

    def run(x_ref, y_ref, o_ref, s_ref):
        x = x_ref[...]
        if has_add:
            x = x + y_ref[...]
            s_ref[...] = x
        o_ref[...] = _rms(x, g[...]).astype(o_ref.dtype)

    @pl.when(m < nb)
    def _():
        run(xm, ym, om, sm)

    @pl.when(m == nb)
    def _():
        run(xt, yt, ot, st)


def _norm(xm, xt, gain, tm, add=None, out_dtype=BF16):
    npr, d = xm.shape
    nb = npr // tm
    main = pl.BlockSpec((tm, d), lambda m: (jnp.minimum(m, nb - 1), 0))
    tail = pl.BlockSpec((TAIL, d), lambda m: (0, 0))
    gspec = pl.BlockSpec((1, d), lambda m: (0, 0))
    ins = [xm, xt]
    in_specs = [main, tail]
    out_shape = [jax.ShapeDtypeStruct((npr, d), out_dtype), jax.ShapeDtypeStruct((TAIL, d), out_dtype)]
    out_specs = [main, tail]
    if add is not None:
        ins += list(add)
        in_specs += [main, tail]
        out_shape += [jax.ShapeDtypeStruct((npr, d), F32), jax.ShapeDtypeStruct((TAIL, d), F32)]
        out_specs += [main, tail]
    ins.append(gain.reshape(1, d))
    in_specs.append(gspec)
    return pl.pallas_call(
        functools.partial(_norm_body, nb=nb, has_add=add is not None),
        grid=(nb + 1,), in_specs=in_specs, out_specs=out_specs, out_shape=out_shape,
        compiler_params=_params("arbitrary"), name="rmsnorm",
    )(*ins)


def _router_body(xm, xt, g, wr, hm, ht, im, it, gm, gt, *, nb, n_exp):
    m = pl.program_id(0)

    def run(x_ref, h_ref, i_ref, g_ref):
        h = _rms(x_ref[...], g[...])
        h_ref[...] = h
        logits = jnp.dot(h, wr[...], preferred_element_type=F32, precision=lax.Precision.HIGHEST)
        lane = lax.broadcasted_iota(jnp.int32, logits.shape, 1)
        logits = jnp.where(lane < n_exp, logits, NEG_INF)
        v1 = jnp.max(logits, axis=-1, keepdims=True)
        i1 = jnp.min(jnp.where(logits == v1, lane, LANE), axis=-1, keepdims=True)
        rest = jnp.where(lane == i1, NEG_INF, logits)
        v2 = jnp.max(rest, axis=-1, keepdims=True)
        i2 = jnp.min(jnp.where(rest == v2, lane, LANE), axis=-1, keepdims=True)
        e = jnp.exp(v2 - v1)
        g1 = 1.0 / (1.0 + e)
        g2 = e / (1.0 + e)
        i_ref[...] = jnp.where(lane == 0, i1, jnp.where(lane == 1, i2, 0))
        g_ref[...] = jnp.where(lane == 0, g1, jnp.where(lane == 1, g2, 0.0))

    @pl.when(m < nb)
    def _():
        run(xm, hm, im, gm)

    @pl.when(m == nb)
    def _():
        run(xt, ht, it, gt)


def _norm_router(xm, xt, gain, w_router, tm):
    npr, d = xm.shape
    n_exp = w_router.shape[-1]
    nb = npr // tm
    wr = jnp.pad(w_router, ((0, 0), (0, LANE - n_exp)))
    main = pl.BlockSpec((tm, d), lambda m: (jnp.minimum(m, nb - 1), 0))
    tail = pl.BlockSpec((TAIL, d), lambda m: (0, 0))
    rmain = pl.BlockSpec((tm, LANE), lambda m: (jnp.minimum(m, nb - 1), 0))
    rtail = pl.BlockSpec((TAIL, LANE), lambda m: (0, 0))
    return pl.pallas_call(
        functools.partial(_router_body, nb=nb, n_exp=n_exp),
        grid=(nb + 1,),
        in_specs=[main, tail, pl.BlockSpec((1, d), lambda m: (0, 0)),
                  pl.BlockSpec((d, LANE), lambda m: (0, 0))],
        out_specs=[main, tail, rmain, rtail, rmain, rtail],
        out_shape=[jax.ShapeDtypeStruct((npr, d), F32), jax.ShapeDtypeStruct((TAIL, d), F32),
                   jax.ShapeDtypeStruct((npr, LANE), jnp.int32), jax.ShapeDtypeStruct((TAIL, LANE), jnp.int32),
                   jax.ShapeDtypeStruct((npr, LANE), F32), jax.ShapeDtypeStruct((TAIL, LANE), F32)],
        compiler_params=_params("arbitrary"), name="rmsnorm_router",
    )(xm, xt, gain.reshape(1, d), wr)


def _mm_body(*refs, nb, has_res, has_gain, n_out):
    refs = list(refs)
    xm, xt, w = refs[:3]
    pos = 3
    rm = rt = gain = None
    if has_res:
        rm, rt = refs[pos:pos + 2]
        pos += 2
    if has_gain:
        gain = refs[pos]
        pos += 1
    outs = refs[pos:pos + 2 * n_out]
    wb = refs[pos + 2 * n_out]
    m = pl.program_id(1)

    @pl.when(m == 0)
    def _():
        wb[...] = w[...].astype(BF16)

    def run(x_ref, r_ref, o_refs):
        acc = _dot(x_ref[...], wb[...])
        if has_gain:
            acc = _group_rms(acc, HEAD_DIM) * gain[...]
        if has_res:
            acc = acc + r_ref[...]
        for o in o_refs:
            o[...] = acc.astype(o.dtype)

    @pl.when(m < nb)
    def _():
        run(xm, rm, outs[0::2])

    @pl.when(m == nb)
    def _():
        run(xt, rt, outs[1::2])


def _matmul(xm, xt, w, layer, col0, n_cols, tm, tn, out_dtypes, res=None, gain=None):
    npr, k = xm.shape
    nb = npr // tm
    cb = col0 // tn
    xmain = pl.BlockSpec((tm, k), lambda n, m: (jnp.minimum(m, nb - 1), 0))
    xtail = pl.BlockSpec((TAIL, k), lambda n, m: (0, 0))
    omain = pl.BlockSpec((tm, tn), lambda n, m: (jnp.minimum(m, nb - 1), n))
    otail = pl.BlockSpec((TAIL, tn), lambda n, m: (0, n))
    if w.ndim == 3:
        wspec = pl.BlockSpec((None, k, tn), lambda n, m: (layer, 0, cb + n))
    else:
        wspec = pl.BlockSpec((k, tn), lambda n, m: (0, cb + n))
    ins, in_specs = [xm, xt, w], [xmain, xtail, wspec]
    if res is not None:
        ins += list(res)
        in_specs += [omain, otail]
    if gain is not None:
        ins.append(gain.reshape(1, n_cols))
        in_specs.append(pl.BlockSpec((1, tn), lambda n, m: (0, n)))
    out_shape, out_specs = [], []
    for dm, dt in out_dtypes:
        out_shape += [jax.ShapeDtypeStruct((npr, n_cols), dm), jax.ShapeDtypeStruct((TAIL, n_cols), dt)]
        out_specs += [omain, otail]
    return pl.pallas_call(
        functools.partial(_mm_body, nb=nb, has_res=res is not None, has_gain=gain is not None,
                          n_out=len(out_dtypes)),
        grid=(n_cols // tn, nb + 1), in_specs=in_specs, out_specs=out_specs, out_shape=out_shape,
        scratch_shapes=[pltpu.VMEM((k, tn), BF16)],
        compiler_params=_params("arbitrary", "arbitrary"), name="matmul",
    )(*ins)


def _conv_body(xm, xt, w_b, w_c, w_v, cw, p0, p1, gm, gt, ut, wbs, wcs, wvs, carry, *, nb, bps):
    m = pl.program_id(1)

    @pl.when(m == 0)
    def _():
        wbs[...] = w_b[...].astype(BF16)
        wcs[...] = w_c[...].astype(BF16)
        wvs[...] = w_v[...].astype(BF16)

    w0, w1, w2 = cw[0:1, :], cw[1:2, :], cw[2:3, :]

    @pl.when(m < nb)
    def _():
        x = xm[...]
        u = _dot(x, wcs[...]) * _dot(x, wvs[...])
        b = _dot(x, wbs[...])

        @pl.when(m % bps == 0)
        def _():
            carry[...] = jnp.zeros_like(carry)

        prev1 = carry[7:8, :]
        prev2 = carry[6:7, :]
        row = lax.broadcasted_iota(jnp.int32, u.shape, 0)
        u1 = jnp.where(row == 0, prev1, pltpu.roll(u, 1, 0))
        u2 = jnp.where(row == 0, prev2, jnp.where(row == 1, prev1, pltpu.roll(u, 2, 0)))
        gm[...] = (b * (w0 * u2 + w1 * u1 + w2 * u)).astype(gm.dtype)
        tail_rows = u[u.shape[0] - 8:, :]
        carry[...] = tail_rows
        ut[...] = tail_rows

    @pl.when(m == nb)
    def _():
        x = xt[...]
        u = _dot(x, wcs[...]) * _dot(x, wvs[...])
        b = _dot(x, wbs[...])
        gt[...] = (b * (w0 * p0[...] + w1 * p1[...] + w2 * u)).astype(gt.dtype)
        ut[...] = u[0:8, :]


def _conv_mixer(hm, ht, w_in, conv_w, layer, prev0, prev1, tm, tn, seq_len):
    npr, d = hm.shape
    nb = npr // tm
    nn = d // tn
    xmain = pl.BlockSpec((tm, d), lambda n, m: (jnp.minimum(m, nb - 1), 0))
    xtail = pl.BlockSpec((TAIL, d), lambda n, m: (0, 0))
    omain = pl.BlockSpec((tm, tn), lambda n, m: (jnp.minimum(m, nb - 1), n))
    otail = pl.BlockSpec((TAIL, tn), lambda n, m: (0, n))

    def wspec(part):
        return pl.BlockSpec((None, d, tn), lambda n, m: (layer, 0, part * nn + n))

    return pl.pallas_call(
        functools.partial(_conv_body, nb=nb, bps=seq_len // tm),
        grid=(nn, nb + 1),
        in_specs=[xmain, xtail, wspec(0), wspec(1), wspec(2),
                  pl.BlockSpec((None, CONV_W, tn), lambda n, m: (layer, 0, n)), otail, otail],
        out_specs=[omain, otail, pl.BlockSpec((8, tn), lambda n, m: (m, n))],
        out_shape=[jax.ShapeDtypeStruct((npr, d), BF16), jax.ShapeDtypeStruct((TAIL, d), BF16),
                   jax.ShapeDtypeStruct(((nb + 1) * 8, d), F32)],
        scratch_shapes=[pltpu.VMEM((d, tn), BF16)] * 3 + [pltpu.VMEM((8, tn), F32)],
        compiler_params=_params("arbitrary", "arbitrary"), name="conv_mixer",
    )(hm, ht, w_in, w_in, w_in, conv_w, prev0, prev1)


def _ffn_rows(x, wgs, wus, wds):
    g = _dot(x, wgs[...])
    a = (g * jax.nn.sigmoid(g) * _dot(x, wus[...])).astype(BF16)
    return _dot(a, wds[...])


def _cast_weights(wg, wu, wd, wgs, wus, wds):
    wgs[...] = wg[...].astype(BF16)
    wus[...] = wu[...].astype(BF16)
    wds[...] = wd[...].astype(BF16)


def _ffn_dense_body(xm, xt, wg, wu, wd, om, ot, wgs, wus, wds, *, nb, sub):
    c = pl.program_id(0)
    f = pl.program_id(1)
    _cast_weights(wg, wu, wd, wgs, wus, wds)

    @pl.when(c < nb)
    def _():
        @pl.when(f == 0)
        def _():
            om[...] = jnp.zeros_like(om)

        def step(i, carry):
            r = pl.ds(pl.multiple_of(i * sub, sub), sub)
            om[r, :] += _ffn_rows(xm[r, :], wgs, wus, wds)
            return carry

        lax.fori_loop(0, om.shape[0] // sub, step, 0)

    @pl.when(c == nb)
    def _():
        @pl.when(f == 0)
        def _():
            ot[...] = jnp.zeros_like(ot)

        ot[...] += _ffn_rows(xt[...], wgs, wus, wds)


def _ffn_dense(hm, ht, w_gate, w_up, w_down, layer, tm, tf, sub):
    npr, d = hm.shape
    nb = npr // tm
    ff = w_gate.shape[-1]
    main = pl.BlockSpec((tm, d), lambda c, f: (jnp.minimum(c, nb - 1), 0))
    tail = pl.BlockSpec((TAIL, d), lambda c, f: (0, 0))
    return pl.pallas_call(
        functools.partial(_ffn_dense_body, nb=nb, sub=sub),
        grid=(nb + 1, ff // tf),
        in_specs=[main, tail,
                  pl.BlockSpec((None, d, tf), lambda c, f: (layer, 0, f)),
                  pl.BlockSpec((None, d, tf), lambda c, f: (layer, 0, f)),
                  pl.BlockSpec((None, tf, d), lambda c, f: (layer, f, 0))],
        out_specs=[main, tail],
        out_shape=[jax.ShapeDtypeStruct((npr, d), F32), jax.ShapeDtypeStruct((TAIL, d), F32)],
        scratch_shapes=[pltpu.VMEM((d, tf), BF16), pltpu.VMEM((d, tf), BF16), pltpu.VMEM((tf, d), BF16)],
        compiler_params=_params("arbitrary", "arbitrary"), name="ffn_dense",
    )(hm, ht, w_gate, w_up, w_down)


def _ffn_moe_body(ce, cn, xs, wg, wu, wd, o, wgs, wus, wds, *, sub):
    c = pl.program_id(0)
    f = pl.program_id(1)
    nv = cn[c]

    @pl.when(f == 0)
    def _():
        o[...] = jnp.zeros_like(o)

    @pl.when(nv > 0)
    def _():
        _cast_weights(wg, wu, wd, wgs, wus, wds)

        def step(i, carry):
            r = pl.ds(pl.multiple_of(i * sub, sub), sub)
            o[r, :] += _ffn_rows(xs[r, :], wgs, wus, wds)
            return carry

        lax.fori_loop(0, (nv + sub - 1) // sub, step, 0)


def _ffn_moe(xs, chunk_expert, chunk_rows, w_gate, w_up, w_down, layer, rows, tf, sub):
    p, d = xs.shape
    ff = w_gate.shape[-1]
    nf = ff // tf

    def fsel(c, f, cn):
        return jnp.where(cn[c] > 0, f, nf - 1)

    xspec = pl.BlockSpec((rows, d), lambda c, f, ce, cn: (c, 0))
    return pl.pallas_call(
        functools.partial(_ffn_moe_body, sub=sub),
        grid_spec=pltpu.PrefetchScalarGridSpec(
            num_scalar_prefetch=2, grid=(p // rows, nf),
            in_specs=[xspec,
                      pl.BlockSpec((None, None, d, tf), lambda c, f, ce, cn: (layer, ce[c], 0, fsel(c, f, cn))),
                      pl.BlockSpec((None, None, d, tf), lambda c, f, ce, cn: (layer, ce[c], 0, fsel(c, f, cn))),
                      pl.BlockSpec((None, None, tf, d), lambda c, f, ce, cn: (layer, ce[c], fsel(c, f, cn), 0))],
            out_specs=xspec,
            scratch_shapes=[pltpu.VMEM((d, tf), BF16), pltpu.VMEM((d, tf), BF16), pltpu.VMEM((tf, d), BF16)]),
        out_shape=jax.ShapeDtypeStruct((p, d), F32),
        compiler_params=_params("arbitrary", "arbitrary"), name="ffn_moe",
    )(chunk_expert, chunk_rows, xs, w_gate, w_up, w_down)


def _row_copy(src, src_row, dst, dst_row, sem):
    return pltpu.make_async_copy(src.at[pl.ds(src_row, 1), :], dst.at[pl.ds(dst_row, 1), :], sem)


def _dispatch_body(tok, hm, ht, o, buf, sem, *, gb, npr):
    base = pl.program_id(0) * gb

    def issue(r, carry):
        t = tok[base + r]

        @pl.when(t < npr)
        def _():
            _row_copy(hm, t, buf, r, sem).start()

        @pl.when(t >= npr)
        def _():
            _row_copy(ht, t - npr, buf, r, sem).start()

        return carry

    lax.fori_loop(0, gb, issue, 0)

    def drain(r, carry):
        _row_copy(hm, 0, buf, r, sem).wait()
        return carry

    lax.fori_loop(0, gb, drain, 0)
    o[...] = buf[...].astype(o.dtype)


def _dispatch(hm, ht, tok_of_slot, gb):
    npr, d = hm.shape
    p = tok_of_slot.shape[0]
    return pl.pallas_call(
        functools.partial(_dispatch_body, gb=gb, npr=npr),
        grid_spec=pltpu.PrefetchScalarGridSpec(
            num_scalar_prefetch=1, grid=(p // gb,),
            in_specs=[pl.BlockSpec(memory_space=pl.ANY), pl.BlockSpec(memory_space=pl.ANY)],
            out_specs=pl.BlockSpec((gb, d), lambda i, tok: (i, 0)),
            scratch_shapes=[pltpu.VMEM((gb, d), F32), pltpu.SemaphoreType.DMA]),
        out_shape=jax.ShapeDtypeStruct((p, d), BF16),
        compiler_params=_params("arbitrary"), name="moe_dispatch",
    )(tok_of_slot, hm, ht)


def _combine_body(s1, s2, xm, xt, gm, gt, yb, om, ot, abuf, bbuf, sem, *, nb, tb, npr):
    m = pl.program_id(0)

    def run(x_ref, g_ref, o_ref, base, rows):
        def issue(r, carry):
            _row_copy(yb, s1[base + r], abuf, r, sem).start()
            _row_copy(yb, s2[base + r], bbuf, r, sem).start()
            return carry

        lax.fori_loop(0, rows, issue, 0)

        def drain(r, carry):
            _row_copy(yb, 0, abuf, r, sem).wait()
            _row_copy(yb, 0, bbuf, r, sem).wait()
            return carry

        lax.fori_loop(0, rows, drain, 0)
        g = g_ref[...]
        o_ref[...] = x_ref[...] + g[:, 0:1] * abuf[0:rows, :] + g[:, 1:2] * bbuf[0:rows, :]

    @pl.when(m < nb)
    def _():
        run(xm, gm, om, m * tb, tb)

    @pl.when(m == nb)
    def _():
        run(xt, gt, ot, npr, TAIL)


def _combine(xm, xt, gm, gt, yb, slot1, slot2, tb):
    npr, d = xm.shape
    nb = npr // tb
    main = pl.BlockSpec((tb, d), lambda m, s1, s2: (jnp.minimum(m, nb - 1), 0))
    tail = pl.BlockSpec((TAIL, d), lambda m, s1, s2: (0, 0))
    gmain = pl.BlockSpec((tb, LANE), lambda m, s1, s2: (jnp.minimum(m, nb - 1), 0))
    gtail = pl.BlockSpec((TAIL, LANE), lambda m, s1, s2: (0, 0))
    return pl.pallas_call(
        functools.partial(_combine_body, nb=nb, tb=tb, npr=npr),
        grid_spec=pltpu.PrefetchScalarGridSpec(
            num_scalar_prefetch=2, grid=(nb + 1,),
            in_specs=[main, tail, gmain, gtail, pl.BlockSpec(memory_space=pl.ANY)],
            out_specs=[main, tail],
            scratch_shapes=[pltpu.VMEM((tb, d), F32), pltpu.VMEM((tb, d), F32), pltpu.SemaphoreType.DMA]),
        out_shape=[jax.ShapeDtypeStruct((npr, d), F32), jax.ShapeDtypeStruct((TAIL, d), F32)],
        compiler_params=_params("arbitrary"), name="moe_combine",
    )(slot1, slot2, xm, xt, gm, gt, yb)


def _route(route_i, route_i_tail, n_tok, n_exp, rows):
    npr = route_i.shape[0]
    ids = jnp.concatenate([route_i[:, :TOP_K], route_i_tail[:n_tok - npr, :TOP_K]], axis=0)
    flat_e = ids.reshape(-1)
    onehot = (flat_e[:, None] == jnp.arange(n_exp, dtype=jnp.int32)[None, :]).astype(jnp.int32)
    before = jnp.cumsum(onehot, axis=0) - onehot
    rank = jnp.sum(before * onehot, axis=1)
    counts = jnp.sum(onehot, axis=0)
    seg = (counts + rows - 1) // rows * rows
    seg_end = jnp.cumsum(seg)
    seg_start = seg_end - seg
    slot = (seg_start[flat_e] + rank).astype(jnp.int32)
    n_chunks = (n_tok * TOP_K) // rows + n_exp
    tok = jnp.repeat(jnp.arange(n_tok, dtype=jnp.int32), TOP_K)
    tok_of_slot = jnp.zeros((n_chunks * rows,), jnp.int32).at[slot].set(tok)
    chunk_start = jnp.arange(n_chunks, dtype=jnp.int32) * rows
    chunk_e = jnp.minimum(jnp.searchsorted(seg_end, chunk_start, side="right"), n_exp - 1).astype(jnp.int32)
    chunk_rows = jnp.clip(counts[chunk_e] - (chunk_start - seg_start[chunk_e]), 0, rows).astype(jnp.int32)
    last_used = jnp.max(jnp.where(chunk_rows > 0, jnp.arange(n_chunks), 0))
    chunk_e = jnp.where(chunk_rows > 0, chunk_e, chunk_e[last_used]).astype(jnp.int32)
    slots = slot.reshape(n_tok, TOP_K)
    pad = npr + TAIL - n_tok
    slot1 = jnp.pad(slots[:, 0], (0, pad))
    slot2 = jnp.pad(slots[:, 1], (0, pad))
    return tok_of_slot, chunk_e, chunk_rows, slot1, slot2


def _lam(lam_ref, lam0):
    lp = lam_ref[...]
    a = jnp.sum(lp[0:1, :] * lp[1:2, :], axis=-1, keepdims=True)
    b = jnp.sum(lp[2:3, :] * lp[3:4, :], axis=-1, keepdims=True)
    return jnp.exp(a) - jnp.exp(b) + lam0


def _attn_body(q_ref, k_ref, v_ref, sl_ref, lam_ref, sg_ref, o_ref, m_sc, l_sc, acc_sc, *, blk, lam0):
    i = pl.program_id(2)
    slope = sl_ref[0:1, 0:1]
    q = (q_ref[:, 0:HEAD_DIM], q_ref[:, HEAD_DIM:2 * HEAD_DIM])
    rel = (lax.broadcasted_iota(jnp.int32, (blk, blk), 1)
           - lax.broadcasted_iota(jnp.int32, (blk, blk), 0)).astype(F32)
    bias_rel = slope * rel
    m_sc[...] = jnp.full_like(m_sc, NEG_INF)
    l_sc[...] = jnp.zeros_like(l_sc)
    acc_sc[...] = jnp.zeros_like(acc_sc)

    def step(j, diagonal):
        r = pl.ds(pl.multiple_of(j * blk, blk), blk)
        k = k_ref[r, :]
        v = v_ref[r, :]
        bias = bias_rel + slope * ((j - i) * blk).astype(F32)
        for c in range(2):
            s = _dot_nt(q[c], k[:, c * HEAD_DIM:(c + 1) * HEAD_DIM]) + bias
            if diagonal:
                s = jnp.where(rel <= 0.0, s, NEG_INF)
            m_old = m_sc[c]
            m_new = jnp.maximum(m_old, jnp.max(s, axis=-1, keepdims=True))
            alpha = jnp.exp(m_old - m_new)
            p = jnp.exp(s - m_new)
            l_sc[c] = alpha * l_sc[c] + jnp.sum(p, axis=-1, keepdims=True)
            acc_sc[c] = alpha * acc_sc[c] + _dot(p.astype(BF16), v)
            m_sc[c] = m_new

    def off_diagonal(j, carry):
        step(j, False)
        return carry

    lax.fori_loop(0, i, off_diagonal, 0)
    step(i, True)

    o = acc_sc[0] / l_sc[0] - _lam(lam_ref, lam0) * (acc_sc[1] / l_sc[1])
    o_ref[...] = (_rms(o, sg_ref[...]) * (1.0 - lam0)).astype(o_ref.dtype)


def _attn_prompt(q, k, v, slopes, lam_p, sub_gain, n_batch, seq_len, blk, lam0):
    npr, d = q.shape
    n_heads = d // V_DIM
    nq = seq_len // blk
    kv_spec = pl.BlockSpec((seq_len, V_DIM), lambda b, h, i: (b, h))
    return pl.pallas_call(
        functools.partial(_attn_body, blk=blk, lam0=lam0),
        grid=(n_batch, n_heads, nq),
        in_specs=[pl.BlockSpec((blk, V_DIM), lambda b, h, i: (b * nq + i, h)), kv_spec, kv_spec,
                  pl.BlockSpec((None, 1, LANE), lambda b, h, i: (h, 0, 0)),
                  pl.BlockSpec((4, HEAD_DIM), lambda b, h, i: (0, 0)),
                  pl.BlockSpec((1, V_DIM), lambda b, h, i: (0, 0))],
        out_specs=pl.BlockSpec((blk, V_DIM), lambda b, h, i: (b * nq + i, h)),
        out_shape=jax.ShapeDtypeStruct((npr, d), BF16),
        scratch_shapes=[pltpu.VMEM((2, blk, 1), F32), pltpu.VMEM((2, blk, 1), F32),
                        pltpu.VMEM((2, blk, V_DIM), F32)],
        compiler_params=_params("arbitrary", "arbitrary", "arbitrary"), name="attn_prompt",
    )(q, k, v, slopes, lam_p, sub_gain)


def _attn_sample_body(pt, q_ref, kn_ref, vn_ref, sl_ref, lam_ref, sg_ref, *refs, pps, page, q_pos, lam0):
    k_refs = refs[:pps]
    v_refs = refs[pps:2 * pps]
    o_ref, qbd, m_sc, l_sc, acc_sc = refs[2 * pps:]
    p = pl.program_id(1)
    n_hc, d = qbd.shape
    row = lax.broadcasted_iota(jnp.int32, (n_hc, d), 0)
    col = lax.broadcasted_iota(jnp.int32, (n_hc, d), 1)
    own = row == lax.shift_right_logical(col, 7)
    slope = sl_ref[:, 0:1]

    @pl.when(p == 0)
    def _():
        qbd[...] = jnp.where(own, jnp.broadcast_to(q_ref[...], (n_hc, d)), 0.0).astype(BF16)
        m_sc[...] = jnp.full_like(m_sc, NEG_INF)
        l_sc[...] = jnp.zeros_like(l_sc)
        acc_sc[...] = jnp.zeros_like(acc_sc)

    def update(s, pv_fn):
        m_old = m_sc[...]
        m_new = jnp.maximum(m_old, jnp.max(s, axis=-1, keepdims=True))
        alpha = jnp.exp(m_old - m_new)
        pr = jnp.exp(s - m_new)
        l_sc[...] = alpha * l_sc[...] + jnp.sum(pr, axis=-1, keepdims=True)
        acc_sc[...] = alpha * acc_sc[...] + pv_fn(pr)
        m_sc[...] = m_new

    lane = lax.broadcasted_iota(jnp.int32, (1, page), 1)
    for i in range(pps):
        kb = k_refs[i][...].astype(BF16)
        vb = v_refs[i][...].astype(BF16)
        dist = (q_pos - ((p * pps + i) * page + lane)).astype(F32)
        s = _dot_nt(qbd[...], kb) - slope * dist
        update(s, lambda pr, vb=vb: _dot(pr.astype(BF16), vb))

    @pl.when(p == pl.num_programs(1) - 1)
    def _():
        qf = jnp.where(own, jnp.broadcast_to(q_ref[...], (n_hc, d)), 0.0)
        s_new = jnp.sum(qf * kn_ref[...], axis=-1, keepdims=True)
        update(s_new, lambda pr: pr * vn_ref[...])
        o_hc = acc_sc[...] / l_sc[...]
        head = lax.shift_right_logical(col, 8) * 2
        o1 = jnp.sum(jnp.where(row == head, o_hc, 0.0), axis=0, keepdims=True)
        o2 = jnp.sum(jnp.where(row == head + 1, o_hc, 0.0), axis=0, keepdims=True)
        o = o1 - _lam(lam_ref, lam0) * o2
        o_ref[...] = _group_rms(o, V_DIM) * sg_ref[...] * (1.0 - lam0)


def _attn_sample(q, k_new, v_new, cache_k, cache_v, page_table, slope_rows, lam_p, sub_gain_row, pps, lam0):
    n_seq, d = q.shape
    n_pages = page_table.shape[1]
    n_pool, page = cache_k.shape[:2]
    n_hc = d // HEAD_DIM
    ck = cache_k.reshape(n_pool, page, d)
    cv = cache_v.reshape(n_pool, page, d)
    row3 = pl.BlockSpec((None, 1, d), lambda b, p, pt: (b, 0, 0))

    def page_spec(i):
        return pl.BlockSpec((None, page, d), lambda b, p, pt: (pt[b * n_pages + p * pps + i], 0, 0))

    out = pl.pallas_call(
        functools.partial(_attn_sample_body, pps=pps, page=page, q_pos=n_pages * page, lam0=lam0),
        grid_spec=pltpu.PrefetchScalarGridSpec(
            num_scalar_prefetch=1, grid=(n_seq, n_pages // pps),
            in_specs=[row3, row3, row3,
                      pl.BlockSpec((n_hc, LANE), lambda b, p, pt: (0, 0)),
                      pl.BlockSpec((4, HEAD_DIM), lambda b, p, pt: (0, 0)),
                      pl.BlockSpec((1, d), lambda b, p, pt: (0, 0))]
                     + [page_spec(i) for i in range(pps)] * 2,
            out_specs=row3,
            scratch_shapes=[pltpu.VMEM((n_hc, d), BF16), pltpu.VMEM((n_hc, 1), F32),
                            pltpu.VMEM((n_hc, 1), F32), pltpu.VMEM((n_hc, d), F32)]),
        out_shape=jax.ShapeDtypeStruct((n_seq, 1, d), F32),
        compiler_params=_params("arbitrary", "arbitrary"), name="attn_sample",
    )(page_table.reshape(-1), q.reshape(n_seq, 1, d), k_new.reshape(n_seq, 1, d), v_new.reshape(n_seq, 1, d),
      slope_rows, lam_p, sub_gain_row, *([ck] * pps), *([cv] * pps))
    return out.reshape(n_seq, d)


def _tiles(seq_len, d_model, d_ff, d_ff_e):
    tm = min(1024, seq_len)
    return dict(
        tm=tm,
        tr=min(256, tm),
        tn=min(512, d_model),
        tn_conv=min(256, d_model),
        tf=256,
        sub=min(256, tm),
        moe_rows=tm,
        gather_rows=min(256, tm),
        attn_blk=min(512, seq_len),
        pages_per_step=2,
    )


def kernel(x_prompt, x_sample, state_conv, cache_k, cache_v, page_table, mix_norm, ffn_norm, a_w_in, a_conv,
           a_w_out, kv_norm, w_kv, k_norm, b_w_q, b_q_norm, b_lam, b_subln, b_w_o, ffn_w_gate, ffn_w_up,
           ffn_w_down, moe_router, moe_w_gate, moe_w_up, moe_w_down):
    n_batch, seq_len, d = x_prompt.shape
    n_seq = x_sample.shape[0]
    assert x_sample.shape[1] == 1 and n_seq <= TAIL
    depth = mix_norm.shape[0]
    n_a = a_w_in.shape[0]
    n_heads = d // V_DIM
    n_exp = moe_router.shape[-1]
    k_width = n_heads * 2 * HEAD_DIM
    npr = n_batch * seq_len
    n_tok = npr + n_seq
    t = _tiles(seq_len, d, ffn_w_gate.shape[-1], moe_w_gate.shape[-1])
    tm, tn, tr = t["tm"], t["tn"], t["tr"]
    nb = npr // tm
    bps = seq_len // tm

    xm = x_prompt.reshape(npr, d)
    xt = jnp.pad(x_sample.reshape(n_seq, d), ((0, TAIL - n_seq), (0, 0)))

    slopes = jnp.exp2(-8.0 * jnp.arange(1, n_heads + 1, dtype=F32) / n_heads)
    slopes_lane = jnp.broadcast_to(slopes[:, None, None], (n_heads, 1, LANE))
    slope_rows = jnp.broadcast_to(jnp.repeat(slopes, 2)[:, None], (2 * n_heads, LANE))
    k_gain = jnp.tile(k_norm.reshape(-1), n_heads)

    def pad_tail(a):
        return jnp.pad(a, ((0, TAIL - n_seq), (0, 0)))

    conv_prompt, conv_sample = [], []
    pending = None
    k_f32 = v_f32 = k_bf = v_bf = None
    for l in range(depth):
        if pending is None:
            hm, ht = _norm(xm, xt, mix_norm[l], tr)
        else:
            hm, ht, xm, xt = _norm(xm, xt, mix_norm[l], tr, add=pending)
            pending = None
        if l < n_a:
            prev0 = pad_tail(state_conv[l, :, 0, :])
            prev1 = pad_tail(state_conv[l, :, 1, :])
            gm, gt, ut = _conv_mixer(hm, ht, a_w_in, a_conv, l, prev0, prev1, tm, t["tn_conv"], seq_len)
            ut = ut.reshape(nb + 1, 8, d)
            conv_prompt.append(ut[bps - 1:nb:bps, 6:8, :])
            conv_sample.append(jnp.stack([state_conv[l, :, 1, :], ut[nb, :n_seq, :]], axis=1))
            xm, xt = _matmul(gm, gt, a_w_out, l, 0, d, tm, tn, [(F32, F32)], res=(xm, xt))
        else:
            j = l - n_a
            if k_f32 is None:
                nm, nt = _norm(xm, xt, kv_norm, tr)
                k_f32, kt_f32, k_bf, _ = _matmul(nm, nt, w_kv, 0, 0, k_width, tm, tn,
                                                 [(F32, F32), (BF16, BF16)], gain=k_gain)
                v_f32, vt_f32, v_bf, _ = _matmul(nm, nt, w_kv, 0, k_width, d, tm, tn,
                                                 [(F32, F32), (BF16, BF16)])
            lam0 = 0.8 - 0.6 * math.exp(-0.3 * l)
            q_gain = jnp.tile(b_q_norm[j].reshape(-1), n_heads) * (HEAD_DIM ** -0.5)
            qm, qt = _matmul(hm, ht, b_w_q, j, 0, k_width, tm, tn, [(BF16, F32)], gain=q_gain)
            om = _attn_prompt(qm, k_bf, v_bf, slopes_lane, b_lam[j], b_subln[j].reshape(1, V_DIM),
                              n_batch, seq_len, t["attn_blk"], lam0)
            os_ = _attn_sample(qt[:n_seq], kt_f32[:n_seq], vt_f32[:n_seq], cache_k, cache_v, page_table,
                               slope_rows, b_lam[j], jnp.tile(b_subln[j], n_heads).reshape(1, d),
                               t["pages_per_step"], lam0)
            ot = pad_tail(os_).astype(BF16)
            xm, xt = _matmul(om, ot, b_w_o, j, 0, d, tm, tn, [(F32, F32)], res=(xm, xt))
        i = l // 2
        if l % 2 == 0:
            h2m, h2t = _norm(xm, xt, ffn_norm[l], tr)
            pending = _ffn_dense(h2m, h2t, ffn_w_gate, ffn_w_up, ffn_w_down, i, tm, t["tf"], t["sub"])
        else:
            h2m, h2t, rim, rit, rgm, rgt = _norm_router(xm, xt, ffn_norm[l], moe_router[i], tr)
            tok_of_slot, chunk_e, chunk_rows, slot1, slot2 = _route(rim, rit, n_tok, n_exp, t["moe_rows"])
            xs = _dispatch(h2m, h2t, tok_of_slot, t["gather_rows"])
            yb = _ffn_moe(xs, chunk_e, chunk_rows, moe_w_gate, moe_w_up, moe_w_down, i,
                          t["moe_rows"], t["tf"], t["sub"])
            valid = (jnp.arange(TAIL) < n_seq)[:, None]
            xm, xt = _combine(xm, xt, rgm, jnp.where(valid, rgt, 0.0), yb, slot1, slot2, t["gather_rows"])
    if pending is not None:
        xm, xt = xm + pending[0], xt + pending[1]

    y_prompt = xm.reshape(n_batch, seq_len, d)
    y_sample = xt[:n_seq].reshape(n_seq, 1, d)
    return (y_prompt, y_sample, jnp.stack(conv_prompt), jnp.stack(conv_sample),
            k_f32.reshape(n_batch, seq_len, n_heads, 2, HEAD_DIM),
            v_f32.reshape(n_batch, seq_len, n_heads, V_DIM),
            kt_f32[:n_seq].reshape(n_seq, 1, n_heads, 2, HEAD_DIM),
            vt_f32[:n_seq].reshape(n_seq, 1, n_heads, V_DIM))
```

```python
import functools
import math

import jax
import jax.numpy as jnp
from jax import lax
from jax.experimental import pallas as pl
from jax.experimental.pallas import tpu as pltpu

F32 = jnp.float32
BF16 = jnp.bfloat16
EPS = 1e-6
HEAD_DIM = 128
V_DIM = 2 * HEAD_DIM
TOP_K = 2
CONV_W = 3
LANE = 128
TAIL = 16
V7X_VMEM_BYTES = 64 * 1024 * 1024
VMEM_LIMIT = V7X_VMEM_BYTES - 8 * 1024 * 1024
NEG_INF = float("-inf")
LOG2E = 1.0 / math.log(2.0)


def _params(*sem):
    return pltpu.CompilerParams(dimension_semantics=sem, vmem_limit_bytes=VMEM_LIMIT)


def _dot(a, b):
    return jnp.dot(a, b, preferred_element_type=F32)


def _dot_nt(a, b):
    return lax.dot_general(a, b, (((1,), (1,)), ((), ())), preferred_element_type=F32)


def _rms(x, g):
    ms = jnp.mean(x * x, axis=-1, keepdims=True)
    return x * lax.rsqrt(ms + EPS) * g


def _group_rms(x, gsize):
    outs = []
    for g in range(x.shape[-1] // gsize):
        blk = x[:, g * gsize:(g + 1) * gsize]
        ms = jnp.mean(blk * blk, axis=-1, keepdims=True)
        outs.append(blk * lax.rsqrt(ms + EPS))
    return outs[0] if len(outs) == 1 else jnp.concatenate(outs, axis=-1)


def _norm_body(*refs, nb, has_add):
    if has_add:
        xm, xt, ym, yt, g, om, ot, sm, st = refs
    else:
        xm, xt, g, om, ot = refs
        ym = yt = sm = st = None
    m = pl.program_id(0)

    def run(x_ref, y_ref, o_ref, s_ref):
        x = x_ref[...]
        if has_add:
            x = x + y_ref[...]
            s_ref[...] = x
        o_ref[...] = _rms(x, g[...]).astype(o_ref.dtype)

    @pl.when(m < nb)
    def _():
        run(xm, ym, om, sm)

    @pl.when(m == nb)
    def _():
        run(xt, yt, ot, st)


def _norm(xm, xt, gain, tm, add=None, out_dtype=BF16):
    npr, d = xm.shape
    nb = npr // tm
    main = pl.BlockSpec((tm, d), lambda m: (jnp.minimum(m, nb - 1), 0))
    tail = pl.BlockSpec((TAIL, d), lambda m: (0, 0))
    gspec = pl.BlockSpec((1, d), lambda m: (0, 0))
    ins = [xm, xt]
    in_specs = [main, tail]
    out_shape = [jax.ShapeDtypeStruct((npr, d), out_dtype), jax.ShapeDtypeStruct((TAIL, d), out_dtype)]
    out_specs = [main, tail]
    if add is not None:
        ins += list(add)
        in_specs += [main, tail]
        out_shape += [jax.ShapeDtypeStruct((npr, d), F32), jax.ShapeDtypeStruct((TAIL, d), F32)]
        out_specs += [main, tail]
    ins.append(gain.reshape(1, d))
    in_specs.append(gspec)
    return pl.pallas_call(
        functools.partial(_norm_body, nb=nb, has_add=add is not None),
        grid=(nb + 1,), in_specs=in_specs, out_specs=out_specs, out_shape=out_shape,
        compiler_params=_params("arbitrary"), name="rmsnorm",
    )(*ins)


def _router_body(xm, xt, g, wr, h_all, im, it, gm, gt, *, nb, n_exp):
    m = pl.program_id(0)

    def run(x_ref, i_ref, g_ref):
        h = _rms(x_ref[...], g[...])
        rows = h.shape[0]
        h_all[0:rows, :] = h
        if rows < h_all.shape[0]:
            h_all[rows:, :] = jnp.zeros((h_all.shape[0] - rows, h.shape[1]), F32)
        logits = jnp.dot(h, wr[...], preferred_element_type=F32, precision=lax.Precision.HIGHEST)
        lane = lax.broadcasted_iota(jnp.int32, logits.shape, 1)
        logits = jnp.where(lane < n_exp, logits, NEG_INF)
        v1 = jnp.max(logits, axis=-1, keepdims=True)
        i1 = jnp.min(jnp.where(logits == v1, lane, LANE), axis=-1, keepdims=True)
        rest = jnp.where(lane == i1, NEG_INF, logits)
        v2 = jnp.max(rest, axis=-1, keepdims=True)
        i2 = jnp.min(jnp.where(rest == v2, lane, LANE), axis=-1, keepdims=True)
        e = jnp.exp(v2 - v1)
        g1 = 1.0 / (1.0 + e)
        g2 = e / (1.0 + e)
        i_ref[...] = jnp.where(lane == 0, i1, jnp.where(lane == 1, i2, 0))
        g_ref[...] = jnp.where(lane == 0, g1, jnp.where(lane == 1, g2, 0.0))

    @pl.when(m < nb)
    def _():
        run(xm, im, gm)

    @pl.when(m == nb)
    def _():
        run(xt, it, gt)


def _norm_router(xm, xt, gain, w_router, tm):
    npr, d = xm.shape
    n_exp = w_router.shape[-1]
    nb = npr // tm
    wr = jnp.pad(w_router, ((0, 0), (0, LANE - n_exp)))
    main = pl.BlockSpec((tm, d), lambda m: (jnp.minimum(m, nb - 1), 0))
    tail = pl.BlockSpec((TAIL, d), lambda m: (0, 0))
    rmain = pl.BlockSpec((tm, LANE), lambda m: (jnp.minimum(m, nb - 1), 0))
    rtail = pl.BlockSpec((TAIL, LANE), lambda m: (0, 0))
    return pl.pallas_call(
        functools.partial(_router_body, nb=nb, n_exp=n_exp),
        grid=(nb + 1,),
        in_specs=[main, tail, pl.BlockSpec((1, d), lambda m: (0, 0)),
                  pl.BlockSpec((d, LANE), lambda m: (0, 0))],
        out_specs=[pl.BlockSpec((tm, d), lambda m: (m, 0)), rmain, rtail, rmain, rtail],
        out_shape=[jax.ShapeDtypeStruct((npr + tm, d), F32),
                   jax.ShapeDtypeStruct((npr, LANE), jnp.int32), jax.ShapeDtypeStruct((TAIL, LANE), jnp.int32),
                   jax.ShapeDtypeStruct((npr, LANE), F32), jax.ShapeDtypeStruct((TAIL, LANE), F32)],
        compiler_params=_params("arbitrary"), name="rmsnorm_router",
    )(xm, xt, gain.reshape(1, d), wr)


def _mm_body(*refs, nb, has_res, has_gain, n_out):
    refs = list(refs)
    xm, xt, w = refs[:3]
    pos = 3
    rm = rt = gain = None
    if has_res:
        rm, rt = refs[pos:pos + 2]
        pos += 2
    if has_gain:
        gain = refs[pos]
        pos += 1
    outs = refs[pos:pos + 2 * n_out]
    wb = refs[pos + 2 * n_out]
    m = pl.program_id(1)

    @pl.when(m == 0)
    def _():
        wb[...] = w[...].astype(BF16)

    def run(x_ref, r_ref, o_refs):
        acc = _dot(x_ref[...], wb[...])
        if has_gain:
            acc = _group_rms(acc, HEAD_DIM) * gain[...]
        if has_res:
            acc = acc + r_ref[...]
        for o in o_refs:
            o[...] = acc.astype(o.dtype)

    @pl.when(m < nb)
    def _():
        run(xm, rm, outs[0::2])

    @pl.when(m == nb)
    def _():
        run(xt, rt, outs[1::2])


def _matmul(xm, xt, w, layer, col0, n_cols, tm, tn, out_dtypes, res=None, gain=None):
    npr, k = xm.shape
    nb = npr // tm
    cb = col0 // tn
    xmain = pl.BlockSpec((tm, k), lambda n, m: (jnp.minimum(m, nb - 1), 0))
    xtail = pl.BlockSpec((TAIL, k), lambda n, m: (0, 0))
    omain = pl.BlockSpec((tm, tn), lambda n, m: (jnp.minimum(m, nb - 1), n))
    otail = pl.BlockSpec((TAIL, tn), lambda n, m: (0, n))
    if w.ndim == 3:
        wspec = pl.BlockSpec((None, k, tn), lambda n, m: (layer, 0, cb + n))
    else:
        wspec = pl.BlockSpec((k, tn), lambda n, m: (0, cb + n))
    ins, in_specs = [xm, xt, w], [xmain, xtail, wspec]
    if res is not None:
        ins += list(res)
        in_specs += [omain, otail]
    if gain is not None:
        ins.append(gain.reshape(1, n_cols))
        in_specs.append(pl.BlockSpec((1, tn), lambda n, m: (0, n)))
    out_shape, out_specs = [], []
    for dm, dt in out_dtypes:
        out_shape += [jax.ShapeDtypeStruct((npr, n_cols), dm), jax.ShapeDtypeStruct((TAIL, n_cols), dt)]
        out_specs += [omain, otail]
    return pl.pallas_call(
        functools.partial(_mm_body, nb=nb, has_res=res is not None, has_gain=gain is not None,
                          n_out=len(out_dtypes)),
        grid=(n_cols // tn, nb + 1), in_specs=in_specs, out_specs=out_specs, out_shape=out_shape,
        scratch_shapes=[pltpu.VMEM((k, tn), BF16)],
        compiler_params=_params("arbitrary", "arbitrary"), name="matmul",
    )(*ins)


def _conv_body(xm, xt, w_b, w_c, w_v, cw, p0, p1, gm, gt, ut, wbs, wcs, wvs, carry, *, nb, bps):
    m = pl.program_id(1)

    @pl.when(m == 0)
    def _():
        wbs[...] = w_b[...].astype(BF16)
        wcs[...] = w_c[...].astype(BF16)
        wvs[...] = w_v[...].astype(BF16)

    w0, w1, w2 = cw[0:1, :], cw[1:2, :], cw[2:3, :]

    @pl.when(m < nb)
    def _():
        x = xm[...]
        u = _dot(x, wcs[...]) * _dot(x, wvs[...])
        b = _dot(x, wbs[...])

        @pl.when(m % bps == 0)
        def _():
            carry[...] = jnp.zeros_like(carry)

        prev1 = carry[7:8, :]
        prev2 = carry[6:7, :]
        row = lax.broadcasted_iota(jnp.int32, u.shape, 0)
        u1 = jnp.where(row == 0, prev1, pltpu.roll(u, 1, 0))
        u2 = jnp.where(row == 0, prev2, jnp.where(row == 1, prev1, pltpu.roll(u, 2, 0)))
        gm[...] = (b * (w0 * u2 + w1 * u1 + w2 * u)).astype(gm.dtype)
        tail_rows = u[u.shape[0] - 8:, :]
        carry[...] = tail_rows
        ut[...] = tail_rows

    @pl.when(m == nb)
    def _():
        x = xt[...]
        u = _dot(x, wcs[...]) * _dot(x, wvs[...])
        b = _dot(x, wbs[...])
        gt[...] = (b * (w0 * p0[...] + w1 * p1[...] + w2 * u)).astype(gt.dtype)
        ut[...] = u[0:8, :]


def _conv_mixer(hm, ht, w_in, conv_w, layer, prev0, prev1, tm, tn, seq_len):
    npr, d = hm.shape
    nb = npr // tm
    nn = d // tn
    xmain = pl.BlockSpec((tm, d), lambda n, m: (jnp.minimum(m, nb - 1), 0))
    xtail = pl.BlockSpec((TAIL, d), lambda n, m: (0, 0))
    omain = pl.BlockSpec((tm, tn), lambda n, m: (jnp.minimum(m, nb - 1), n))
    otail = pl.BlockSpec((TAIL, tn), lambda n, m: (0, n))

    def wspec(part):
        return pl.BlockSpec((None, d, tn), lambda n, m: (layer, 0, part * nn + n))

    return pl.pallas_call(
        functools.partial(_conv_body, nb=nb, bps=seq_len // tm),
        grid=(nn, nb + 1),
        in_specs=[xmain, xtail, wspec(0), wspec(1), wspec(2),
                  pl.BlockSpec((None, CONV_W, tn), lambda n, m: (layer, 0, n)), otail, otail],
        out_specs=[omain, otail, pl.BlockSpec((8, tn), lambda n, m: (m, n))],
        out_shape=[jax.ShapeDtypeStruct((npr, d), BF16), jax.ShapeDtypeStruct((TAIL, d), BF16),
                   jax.ShapeDtypeStruct(((nb + 1) * 8, d), F32)],
        scratch_shapes=[pltpu.VMEM((d, tn), BF16)] * 3 + [pltpu.VMEM((8, tn), F32)],
        compiler_params=_params("arbitrary", "arbitrary"), name="conv_mixer",
    )(hm, ht, w_in, w_in, w_in, conv_w, prev0, prev1)


def _ffn_rows(x, wgs, wus, wds):
    g = _dot(x, wgs[...])
    a = (g * jax.nn.sigmoid(g) * _dot(x, wus[...])).astype(BF16)
    return _dot(a, wds[...])


def _cast_weights(wg, wu, wd, wgs, wus, wds):
    wgs[...] = wg[...].astype(BF16)
    wus[...] = wu[...].astype(BF16)
    wds[...] = wd[...].astype(BF16)


def _ffn_dense_body(xm, xt, wg, wu, wd, om, ot, wgs, wus, wds, *, nb):
    c = pl.program_id(0)
    f = pl.program_id(1)
    _cast_weights(wg, wu, wd, wgs, wus, wds)

    @pl.when(c < nb)
    def _():
        @pl.when(f == 0)
        def _():
            om[...] = jnp.zeros_like(om)

        om[...] += _ffn_rows(xm[...], wgs, wus, wds)

    @pl.when(c == nb)
    def _():
        @pl.when(f == 0)
        def _():
            ot[...] = jnp.zeros_like(ot)

        ot[...] += _ffn_rows(xt[...], wgs, wus, wds)


def _ffn_dense(hm, ht, w_gate, w_up, w_down, layer, tm, tf):
    npr, d = hm.shape
    nb = npr // tm
    ff = w_gate.shape[-1]
    main = pl.BlockSpec((tm, d), lambda c, f: (jnp.minimum(c, nb - 1), 0))
    tail = pl.BlockSpec((TAIL, d), lambda c, f: (0, 0))
    return pl.pallas_call(
        functools.partial(_ffn_dense_body, nb=nb),
        grid=(nb + 1, ff // tf),
        in_specs=[main, tail,
                  pl.BlockSpec((None, d, tf), lambda c, f: (layer, 0, f)),
                  pl.BlockSpec((None, d, tf), lambda c, f: (layer, 0, f)),
                  pl.BlockSpec((None, tf, d), lambda c, f: (layer, f, 0))],
        out_specs=[main, tail],
        out_shape=[jax.ShapeDtypeStruct((npr, d), F32), jax.ShapeDtypeStruct((TAIL, d), F32)],
        scratch_shapes=[pltpu.VMEM((d, tf), BF16), pltpu.VMEM((d, tf), BF16), pltpu.VMEM((tf, d), BF16)],
        compiler_params=_params("arbitrary", "arbitrary"), name="ffn_dense",
    )(hm, ht, w_gate, w_up, w_down)


def _ffn_moe_body(ce, cn, xs, wg, wu, wd, o, wgs, wus, wds, *, sub):
    c = pl.program_id(0)
    f = pl.program_id(1)
    nv = cn[c]

    @pl.when(f == 0)
    def _():
        o[...] = jnp.zeros_like(o)

    @pl.when(nv > 0)
    def _():
        _cast_weights(wg, wu, wd, wgs, wus, wds)

    @pl.when(nv == o.shape[0])
    def _():
        o[...] += _ffn_rows(xs[...], wgs, wus, wds)

    @pl.when(jnp.logical_and(nv > 0, nv < o.shape[0]))
    def _():
        def step(i, carry):
            r = pl.ds(pl.multiple_of(i * sub, sub), sub)
            o[r, :] += _ffn_rows(xs[r, :], wgs, wus, wds)
            return carry

        lax.fori_loop(0, (nv + sub - 1) // sub, step, 0)


def _ffn_moe(xs, chunk_expert, chunk_rows, w_gate, w_up, w_down, layer, rows, tf, sub):
    p, d = xs.shape
    ff = w_gate.shape[-1]
    nf = ff // tf

    def fsel(c, f, cn):
        return jnp.where(cn[c] > 0, f, nf - 1)

    xspec = pl.BlockSpec((rows, d), lambda c, f, ce, cn: (c, 0))
    return pl.pallas_call(
        functools.partial(_ffn_moe_body, sub=sub),
        grid_spec=pltpu.PrefetchScalarGridSpec(
            num_scalar_prefetch=2, grid=(p // rows, nf),
            in_specs=[xspec,
                      pl.BlockSpec((None, None, d, tf), lambda c, f, ce, cn: (layer, ce[c], 0, fsel(c, f, cn))),
                      pl.BlockSpec((None, None, d, tf), lambda c, f, ce, cn: (layer, ce[c], 0, fsel(c, f, cn))),
                      pl.BlockSpec((None, None, tf, d), lambda c, f, ce, cn: (layer, ce[c], fsel(c, f, cn), 0))],
            out_specs=xspec,
            scratch_shapes=[pltpu.VMEM((d, tf), BF16), pltpu.VMEM((d, tf), BF16), pltpu.VMEM((tf, d), BF16)]),
        out_shape=jax.ShapeDtypeStruct((p, d), F32),
        compiler_params=_params("arbitrary", "arbitrary"), name="ffn_moe",
    )(chunk_expert, chunk_rows, xs, w_gate, w_up, w_down)


def _row_copy(src, src_row, dst, dst_row, sem):
    return pltpu.make_async_copy(src.at[pl.ds(src_row, 1), :], dst.at[pl.ds(dst_row, 1), :], sem)


def _dispatch_body(tok, h, o, buf, sem, *, gb):
    base = pl.program_id(0) * gb

    def issue(r, carry):
        _row_copy(h, tok[base + r], buf, r, sem).start()
        return carry

    lax.fori_loop(0, gb, issue, 0, unroll=8)

    def drain(r, carry):
        _row_copy(h, 0, buf, r, sem).wait()
        return carry

    lax.fori_loop(0, gb, drain, 0, unroll=8)
    o[...] = buf[...].astype(o.dtype)


def _dispatch(h, tok_of_slot, gb):
    d = h.shape[1]
    p = tok_of_slot.shape[0]
    return pl.pallas_call(
        functools.partial(_dispatch_body, gb=gb),
        grid_spec=pltpu.PrefetchScalarGridSpec(
            num_scalar_prefetch=1, grid=(p // gb,),
            in_specs=[pl.BlockSpec(memory_space=pl.ANY)],
            out_specs=pl.BlockSpec((gb, d), lambda i, tok: (i, 0)),
            scratch_shapes=[pltpu.VMEM((gb, d), F32), pltpu.SemaphoreType.DMA]),
        out_shape=jax.ShapeDtypeStruct((p, d), BF16),
        compiler_params=_params("arbitrary"), name="moe_dispatch",
    )(tok_of_slot, h)


def _combine_body(s1, s2, xm, xt, gm, gt, yb, om, ot, abuf, bbuf, sem, *, nb, tb, npr):
    m = pl.program_id(0)

    def run(x_ref, g_ref, o_ref, base, rows):
        def issue(r, carry):
            _row_copy(yb, s1[base + r], abuf, r, sem).start()
            _row_copy(yb, s2[base + r], bbuf, r, sem).start()
            return carry

        lax.fori_loop(0, rows, issue, 0, unroll=8)

        def drain(r, carry):
            _row_copy(yb, 0, abuf, r, sem).wait()
            _row_copy(yb, 0, bbuf, r, sem).wait()
            return carry

        lax.fori_loop(0, rows, drain, 0, unroll=8)
        g = g_ref[...]
        o_ref[...] = x_ref[...] + g[:, 0:1] * abuf[0:rows, :] + g[:, 1:2] * bbuf[0:rows, :]

    @pl.when(m < nb)
    def _():
        run(xm, gm, om, m * tb, tb)

    @pl.when(m == nb)
    def _():
        run(xt, gt, ot, npr, TAIL)


def _combine(xm, xt, gm, gt, yb, slot1, slot2, tb):
    npr, d = xm.shape
    nb = npr // tb
    main = pl.BlockSpec((tb, d), lambda m, s1, s2: (jnp.minimum(m, nb - 1), 0))
    tail = pl.BlockSpec((TAIL, d), lambda m, s1, s2: (0, 0))
    gmain = pl.BlockSpec((tb, LANE), lambda m, s1, s2: (jnp.minimum(m, nb - 1), 0))
    gtail = pl.BlockSpec((TAIL, LANE), lambda m, s1, s2: (0, 0))
    return pl.pallas_call(
        functools.partial(_combine_body, nb=nb, tb=tb, npr=npr),
        grid_spec=pltpu.PrefetchScalarGridSpec(
            num_scalar_prefetch=2, grid=(nb + 1,),
            in_specs=[main, tail, gmain, gtail, pl.BlockSpec(memory_space=pl.ANY)],
            out_specs=[main, tail],
            scratch_shapes=[pltpu.VMEM((tb, d), F32), pltpu.VMEM((tb, d), F32), pltpu.SemaphoreType.DMA]),
        out_shape=[jax.ShapeDtypeStruct((npr, d), F32), jax.ShapeDtypeStruct((TAIL, d), F32)],
        compiler_params=_params("arbitrary"), name="moe_combine",
    )(slot1, slot2, xm, xt, gm, gt, yb)


def _route(route_i, route_i_tail, n_tok, n_exp, rows):
    npr = route_i.shape[0]
    ids = jnp.concatenate([route_i[:, :TOP_K], route_i_tail[:n_tok - npr, :TOP_K]], axis=0)
    flat_e = ids.reshape(-1)
    onehot = (flat_e[:, None] == jnp.arange(n_exp, dtype=jnp.int32)[None, :]).astype(jnp.int32)
    before = jnp.cumsum(onehot, axis=0) - onehot
    rank = jnp.sum(before * onehot, axis=1)
    counts = jnp.sum(onehot, axis=0)
    seg = (counts + rows - 1) // rows * rows
    seg_end = jnp.cumsum(seg)
    seg_start = seg_end - seg
    slot = (seg_start[flat_e] + rank).astype(jnp.int32)
    n_chunks = (n_tok * TOP_K) // rows + n_exp
    tok = jnp.repeat(jnp.arange(n_tok, dtype=jnp.int32), TOP_K)
    tok_of_slot = jnp.zeros((n_chunks * rows,), jnp.int32).at[slot].set(tok)
    chunk_start = jnp.arange(n_chunks, dtype=jnp.int32) * rows
    chunk_e = jnp.minimum(jnp.searchsorted(seg_end, chunk_start, side="right"), n_exp - 1).astype(jnp.int32)
    chunk_rows = jnp.clip(counts[chunk_e] - (chunk_start - seg_start[chunk_e]), 0, rows).astype(jnp.int32)
    last_used = jnp.max(jnp.where(chunk_rows > 0, jnp.arange(n_chunks), 0))
    chunk_e = jnp.where(chunk_rows > 0, chunk_e, chunk_e[last_used]).astype(jnp.int32)
    slots = slot.reshape(n_tok, TOP_K)
    pad = npr + TAIL - n_tok
    slot1 = jnp.pad(slots[:, 0], (0, pad))
    slot2 = jnp.pad(slots[:, 1], (0, pad))
    return tok_of_slot, chunk_e, chunk_rows, slot1, slot2


def _lam(lam_ref, lam0):
    lp = lam_ref[...]
    a = jnp.sum(lp[0:1, :] * lp[1:2, :], axis=-1, keepdims=True)
    b = jnp.sum(lp[2:3, :] * lp[3:4, :], axis=-1, keepdims=True)
    return jnp.exp(a) - jnp.exp(b) + lam0


def _attn_body(q_ref, k_ref, v_ref, sl_ref, lam_ref, sg_ref, o_ref, vt_sc, m_sc, l_sc, acc_sc, *, blk, lam0):
    i = pl.program_id(2)

    @pl.when(i == 0)
    def _():
        for c in range(vt_sc.shape[0]):
            vt_sc[c] = v_ref[c * blk:(c + 1) * blk, :].T.astype(BF16)

    slope = sl_ref[0:1, 0:1]
    q = (q_ref[:, 0:HEAD_DIM], q_ref[:, HEAD_DIM:2 * HEAD_DIM])
    rel = (lax.broadcasted_iota(jnp.int32, (blk, blk), 0)
           - lax.broadcasted_iota(jnp.int32, (blk, blk), 1)).astype(F32)
    bias_rel = slope * rel
    m_sc[...] = jnp.full_like(m_sc, NEG_INF)
    l_sc[...] = jnp.zeros_like(l_sc)
    acc_sc[...] = jnp.zeros_like(acc_sc)

    def step(j, diagonal):
        k = k_ref[pl.ds(pl.multiple_of(j * blk, blk), blk), :]
        vt = vt_sc[j]
        bias = bias_rel + slope * ((j - i) * blk).astype(F32)
        for c in range(2):
            s = _dot_nt(k[:, c * HEAD_DIM:(c + 1) * HEAD_DIM], q[c]) + bias
            if diagonal:
                s = jnp.where(rel <= 0.0, s, NEG_INF)
            m_old = m_sc[c]
            m_new = jnp.maximum(m_old, jnp.max(s, axis=0, keepdims=True))
            alpha = jnp.exp2(m_old - m_new)
            p = jnp.exp2(s - m_new)
            l_sc[c] = alpha * l_sc[c] + jnp.sum(p, axis=0, keepdims=True)
            acc_sc[c] = alpha * acc_sc[c] + _dot(vt, p.astype(BF16))
            m_sc[c] = m_new

    def off_diagonal(j, carry):
        step(j, False)
        return carry

    lax.fori_loop(0, i, off_diagonal, 0)
    step(i, True)

    o = acc_sc[0] * (1.0 / l_sc[0]) - _lam(lam_ref, lam0) * (acc_sc[1] * (1.0 / l_sc[1]))
    ms = jnp.mean(o * o, axis=0, keepdims=True)
    o = (o * lax.rsqrt(ms + EPS)).T
    o_ref[...] = (o * sg_ref[...] * (1.0 - lam0)).astype(o_ref.dtype)


def _attn_prompt(q, k, v, slopes, lam_p, sub_gain, n_batch, seq_len, blk, lam0):
    npr, d = q.shape
    n_heads = d // V_DIM
    nq = seq_len // blk
    kv_spec = pl.BlockSpec((seq_len, V_DIM), lambda b, h, i: (b, h))
    return pl.pallas_call(
        functools.partial(_attn_body, blk=blk, lam0=lam0),
        grid=(n_batch, n_heads, nq),
        in_specs=[pl.BlockSpec((blk, V_DIM), lambda b, h, i: (b * nq + i, h)), kv_spec, kv_spec,
                  pl.BlockSpec((None, 1, LANE), lambda b, h, i: (h, 0, 0)),
                  pl.BlockSpec((4, HEAD_DIM), lambda b, h, i: (0, 0)),
                  pl.BlockSpec((1, V_DIM), lambda b, h, i: (0, 0))],
        out_specs=pl.BlockSpec((blk, V_DIM), lambda b, h, i: (b * nq + i, h)),
        out_shape=jax.ShapeDtypeStruct((npr, d), BF16),
        scratch_shapes=[pltpu.VMEM((nq, V_DIM, blk), BF16), pltpu.VMEM((2, 1, blk), F32),
                        pltpu.VMEM((2, 1, blk), F32), pltpu.VMEM((2, V_DIM, blk), F32)],
        compiler_params=_params("arbitrary", "arbitrary", "arbitrary"), name="attn_prompt",
    )(q, k, v, slopes, lam_p, sub_gain)


def _attn_sample_body(pt, q_ref, kn_ref, vn_ref, sl_ref, lam_ref, sg_ref, *refs, pps, page, q_pos, lam0):
    k_refs = refs[:pps]
    v_refs = refs[pps:2 * pps]
    o_ref, m_sc, l_sc, acc_sc = refs[2 * pps:]
    p = pl.program_id(1)
    n_rows = q_ref.shape[0]
    n_heads = n_rows // 2
    cols = page * n_heads
    row = lax.broadcasted_iota(jnp.int32, (n_rows, cols), 0)
    col = lax.broadcasted_iota(jnp.int32, (n_rows, cols), 1)
    own = jnp.bitwise_and(col, n_heads - 1) == jnp.bitwise_and(row, n_heads - 1)
    key = lax.shift_right_logical(lax.broadcasted_iota(jnp.int32, (1, cols), 1), n_heads.bit_length() - 1)
    slope = sl_ref[:, 0:1]

    @pl.when(p == 0)
    def _():
        m_sc[...] = jnp.full_like(m_sc, NEG_INF)
        l_sc[...] = jnp.zeros_like(l_sc)
        acc_sc[...] = jnp.zeros_like(acc_sc)

    def update(s, pv_fn):
        m_old = m_sc[...]
        m_new = jnp.maximum(m_old, jnp.max(s, axis=-1, keepdims=True))
        alpha = jnp.exp2(m_old - m_new)
        pr = jnp.exp2(s - m_new)
        l_sc[...] = alpha * l_sc[...] + jnp.sum(pr, axis=-1, keepdims=True)
        acc_sc[...] = alpha * acc_sc[...] + pv_fn(pr)
        m_sc[...] = m_new

    qb = q_ref[...].astype(BF16)
    scores = []
    for i in range(pps):
        s_c = [_dot_nt(qb, k_refs[i][pl.ds(c, cols, stride=2), :].astype(BF16)) for c in range(2)]
        s = jnp.concatenate([s_c[0][:n_heads], s_c[1][n_heads:]], axis=0)
        dist = (q_pos - ((p * pps + i) * page + key)).astype(F32)
        scores.append(jnp.where(own, s - slope * dist, NEG_INF))

    def weighted_values(pr):
        pr = pr.astype(BF16)
        return sum(_dot(pr[:, i * cols:(i + 1) * cols], v_refs[i][...].astype(BF16)) for i in range(pps))

    update(jnp.concatenate(scores, axis=1), weighted_values)

    @pl.when(p == pl.num_programs(1) - 1)
    def _():
        s_new = jnp.sum(q_ref[...] * kn_ref[...], axis=-1, keepdims=True)
        vn = vn_ref[...]
        update(s_new, lambda pr: pr * jnp.concatenate([vn, vn], axis=0))
        o_hc = acc_sc[...] * (1.0 / l_sc[...])
        o = o_hc[:n_heads] - _lam(lam_ref, lam0) * o_hc[n_heads:]
        o_ref[...] = _rms(o, sg_ref[...]) * (1.0 - lam0)


def _attn_sample(q, k_new, v_new, cache_k, cache_v, page_table, slope_rows, lam_p, sub_gain, pps, lam0):
    n_seq, n_rows, _ = q.shape
    n_heads = n_rows // 2
    assert n_heads & (n_heads - 1) == 0
    n_pages = page_table.shape[1]
    n_pool, page = cache_k.shape[:2]
    ck = cache_k.reshape(n_pool * page * n_rows, HEAD_DIM)
    cv = cache_v.reshape(n_pool * page * n_heads, V_DIM)
    qspec = pl.BlockSpec((None, n_rows, HEAD_DIM), lambda b, p, pt: (b, 0, 0))
    vspec = pl.BlockSpec((None, n_heads, V_DIM), lambda b, p, pt: (b, 0, 0))

    def page_idx(b, p, pt, i):
        return pt[b * n_pages + p * pps + i]

    k_specs = [pl.BlockSpec((page * n_rows, HEAD_DIM), lambda b, p, pt, i=i: (page_idx(b, p, pt, i), 0))
               for i in range(pps)]
    v_specs = [pl.BlockSpec((page * n_heads, V_DIM), lambda b, p, pt, i=i: (page_idx(b, p, pt, i), 0))
               for i in range(pps)]
    return pl.pallas_call(
        functools.partial(_attn_sample_body, pps=pps, page=page, q_pos=n_pages * page, lam0=lam0),
        grid_spec=pltpu.PrefetchScalarGridSpec(
            num_scalar_prefetch=1, grid=(n_seq, n_pages // pps),
            in_specs=[qspec, qspec, vspec,
                      pl.BlockSpec((n_rows, LANE), lambda b, p, pt: (0, 0)),
                      pl.BlockSpec((4, HEAD_DIM), lambda b, p, pt: (0, 0)),
                      pl.BlockSpec((1, V_DIM), lambda b, p, pt: (0, 0))] + k_specs + v_specs,
            out_specs=vspec,
            scratch_shapes=[pltpu.VMEM((n_rows, 1), F32), pltpu.VMEM((n_rows, 1), F32),
                            pltpu.VMEM((n_rows, V_DIM), F32)]),
        out_shape=jax.ShapeDtypeStruct((n_seq, n_heads, V_DIM), F32),
        compiler_params=_params("arbitrary", "arbitrary"), name="attn_sample",
    )(page_table.reshape(-1), q, k_new, v_new, slope_rows, lam_p, sub_gain, *([ck] * pps), *([cv] * pps))


def _tiles(seq_len, d_model, d_ff, d_ff_e):
    tm = min(1024, seq_len)
    return dict(
        tm=tm,
        tr=min(256, tm),
        tn=min(512, d_model),
        tn_conv=min(256, d_model),
        tf=256,
        sub=min(256, tm),
        moe_rows=tm,
        gather_rows=min(256, tm),
        attn_blk=min(512, seq_len),
        pages_per_step=4,
    )


def kernel(x_prompt, x_sample, state_conv, cache_k, cache_v, page_table, mix_norm, ffn_norm, a_w_in, a_conv,
           a_w_out, kv_norm, w_kv, k_norm, b_w_q, b_q_norm, b_lam, b_subln, b_w_o, ffn_w_gate, ffn_w_up,
           ffn_w_down, moe_router, moe_w_gate, moe_w_up, moe_w_down):
    n_batch, seq_len, d = x_prompt.shape
    n_seq = x_sample.shape[0]
    assert x_sample.shape[1] == 1 and n_seq <= TAIL
    depth = mix_norm.shape[0]
    n_a = a_w_in.shape[0]
    n_heads = d // V_DIM
    n_exp = moe_router.shape[-1]
    k_width = n_heads * 2 * HEAD_DIM
    npr = n_batch * seq_len
    n_tok = npr + n_seq
    t = _tiles(seq_len, d, ffn_w_gate.shape[-1], moe_w_gate.shape[-1])
    tm, tn, tr = t["tm"], t["tn"], t["tr"]
    nb = npr // tm
    bps = seq_len // tm

    xm = x_prompt.reshape(npr, d)
    xt = jnp.pad(x_sample.reshape(n_seq, d), ((0, TAIL - n_seq), (0, 0)))

    slopes = jnp.exp2(-8.0 * jnp.arange(1, n_heads + 1, dtype=F32) / n_heads) * LOG2E
    slopes_lane = jnp.broadcast_to(slopes[:, None, None], (n_heads, 1, LANE))
    slope_rows = jnp.broadcast_to(jnp.tile(slopes, 2)[:, None], (2 * n_heads, LANE))
    k_gain = jnp.tile(k_norm.reshape(-1), n_heads)

    def comp_major(a):
        return a[:n_seq].reshape(n_seq, n_heads, 2, HEAD_DIM).transpose(0, 2, 1, 3).reshape(n_seq, 2 * n_heads, HEAD_DIM)

    def pad_tail(a):
        return jnp.pad(a, ((0, TAIL - n_seq), (0, 0)))

    conv_prompt, conv_sample = [], []
    pending = None
    k_f32 = v_f32 = k_bf = v_bf = None
    for l in range(depth):
        if pending is None:
            hm, ht = _norm(xm, xt, mix_norm[l], tr)
        else:
            hm, ht, xm, xt = _norm(xm, xt, mix_norm[l], tr, add=pending)
            pending = None
        if l < n_a:
            prev0 = pad_tail(state_conv[l, :, 0, :])
            prev1 = pad_tail(state_conv[l, :, 1, :])
            gm, gt, ut = _conv_mixer(hm, ht, a_w_in, a_conv, l, prev0, prev1, tm, t["tn_conv"], seq_len)
            ut = ut.reshape(nb + 1, 8, d)
            conv_prompt.append(ut[bps - 1:nb:bps, 6:8, :])
            conv_sample.append(jnp.stack([state_conv[l, :, 1, :], ut[nb, :n_seq, :]], axis=1))
            xm, xt = _matmul(gm, gt, a_w_out, l, 0, d, tm, tn, [(F32, F32)], res=(xm, xt))
        else:
            j = l - n_a
            if k_f32 is None:
                nm, nt = _norm(xm, xt, kv_norm, tr)
                k_f32, kt_f32, k_bf, _ = _matmul(nm, nt, w_kv, 0, 0, k_width, tm, tn,
                                                 [(F32, F32), (BF16, BF16)], gain=k_gain)
                v_f32, vt_f32 = _matmul(nm, nt, w_kv, 0, k_width, d, tm, tn, [(F32, F32)])
            lam0 = 0.8 - 0.6 * math.exp(-0.3 * l)
            q_gain = jnp.tile(b_q_norm[j].reshape(-1), n_heads) * (HEAD_DIM ** -0.5 * LOG2E)
            qm, qt = _matmul(hm, ht, b_w_q, j, 0, k_width, tm, tn, [(BF16, F32)], gain=q_gain)
            sub_gain = b_subln[j].reshape(1, V_DIM)
            om = _attn_prompt(qm, k_bf, v_f32, slopes_lane, b_lam[j], sub_gain, n_batch, seq_len,
                              t["attn_blk"], lam0)
            os_ = _attn_sample(comp_major(qt), comp_major(kt_f32), vt_f32[:n_seq].reshape(n_seq, n_heads, V_DIM),
                               cache_k, cache_v, page_table, slope_rows, b_lam[j], sub_gain,
                               t["pages_per_step"], lam0)
            ot = pad_tail(os_.reshape(n_seq, d)).astype(BF16)
            xm, xt = _matmul(om, ot, b_w_o, j, 0, d, tm, tn, [(F32, F32)], res=(xm, xt))
        i = l // 2
        if l % 2 == 0:
            h2m, h2t = _norm(xm, xt, ffn_norm[l], tr)
            pending = _ffn_dense(h2m, h2t, ffn_w_gate, ffn_w_up, ffn_w_down, i, tm, t["tf"])
        else:
            h2, rim, rit, rgm, rgt = _norm_router(xm, xt, ffn_norm[l], moe_router[i], tr)
            tok_of_slot, chunk_e, chunk_rows, slot1, slot2 = _route(rim, rit, n_tok, n_exp, t["moe_rows"])
            xs = _dispatch(h2, tok_of_slot, t["gather_rows"])
            yb = _ffn_moe(xs, chunk_e, chunk_rows, moe_w_gate, moe_w_up, moe_w_down, i,
                          t["moe_rows"], t["tf"], t["sub"])
            valid = (jnp.arange(TAIL) < n_seq)[:, None]
            xm, xt = _combine(xm, xt, rgm, jnp.where(valid, rgt, 0.0), yb, slot1, slot2, t["gather_rows"])
    if pending is not None:
        xm, xt = xm + pending[0], xt + pending[1]

    y_prompt = xm.reshape(n_batch, seq_len, d)
    y_sample = xt[:n_seq].reshape(n_seq, 1, d)
    return (y_prompt, y_sample, jnp.stack(conv_prompt), jnp.stack(conv_sample),
            k_f32.reshape(n_batch, seq_len, n_heads, 2, HEAD_DIM),
            v_f32.reshape(n_batch, seq_len, n_heads, V_DIM),
            kt_f32[:n_seq].reshape(n_seq, 1, n_heads, 2, HEAD_DIM),
            vt_f32[:n_seq].reshape(n_seq, 1, n_heads, V_DIM))
```

```python
import functools
import math

import jax
import jax.numpy as jnp
from jax import lax
from jax.experimental import pallas as pl
from jax.experimental.pallas import tpu as pltpu

F32 = jnp.float32
BF16 = jnp.bfloat16
EPS = 1e-6
HEAD_DIM = 128
V_DIM = 2 * HEAD_DIM
TOP_K = 2
CONV_W = 3
LANE = 128
TAIL = 16
V7X_VMEM_BYTES = 64 * 1024 * 1024
VMEM_LIMIT = V7X_VMEM_BYTES - 8 * 1024 * 1024
NEG_INF = float("-inf")
LOG2E = 1.0 / math.log(2.0)


def _params(*sem):
    return pltpu.CompilerParams(dimension_semantics=sem, vmem_limit_bytes=VMEM_LIMIT)


def _dot(a, b):
    return jnp.dot(a, b, preferred_element_type=F32)


def _dot_nt(a, b):
    return lax.dot_general(a, b, (((1,), (1,)), ((), ())), preferred_element_type=F32)


def _rms(x, g):
    ms = jnp.mean(x * x, axis=-1, keepdims=True)
    return x * lax.rsqrt(ms + EPS) * g


def _group_rms(x, gsize):
    outs = []
    for g in range(x.shape[-1] // gsize):
        blk = x[:, g * gsize:(g + 1) * gsize]
        ms = jnp.mean(blk * blk, axis=-1, keepdims=True)
        outs.append(blk * lax.rsqrt(ms + EPS))
    return outs[0] if len(outs) == 1 else jnp.concatenate(outs, axis=-1)


def _norm_body(*refs, nb, has_add):
    if has_add:
        xm, xt, ym, yt, g, om, ot, sm, st = refs
    else:
        xm, xt, g, om, ot = refs
        ym = yt = sm = st = None
    m = pl.program_id(0)

    def run(x_ref, y_ref, o_ref, s_ref):
        x = x_ref[...]
        if has_add:
            x = x + y_ref[...]
            s_ref[...] = x
        o_ref[...] = _rms(x, g[...]).astype(o_ref.dtype)

    @pl.when(m < nb)
    def _():
        run(xm, ym, om, sm)

    @pl.when(m == nb)
    def _():
        run(xt, yt, ot, st)


def _norm(xm, xt, gain, tm, add=None, out_dtype=BF16):
    npr, d = xm.shape
    nb = npr // tm
    main = pl.BlockSpec((tm, d), lambda m: (jnp.minimum(m, nb - 1), 0))
    tail = pl.BlockSpec((TAIL, d), lambda m: (0, 0))
    gspec = pl.BlockSpec((1, d), lambda m: (0, 0))
    ins = [xm, xt]
    in_specs = [main, tail]
    out_shape = [jax.ShapeDtypeStruct((npr, d), out_dtype), jax.ShapeDtypeStruct((TAIL, d), out_dtype)]
    out_specs = [main, tail]
    if add is not None:
        ins += list(add)
        in_specs += [main, tail]
        out_shape += [jax.ShapeDtypeStruct((npr, d), F32), jax.ShapeDtypeStruct((TAIL, d), F32)]
        out_specs += [main, tail]
    ins.append(gain.reshape(1, d))
    in_specs.append(gspec)
    return pl.pallas_call(
        functools.partial(_norm_body, nb=nb, has_add=add is not None),
        grid=(nb + 1,), in_specs=in_specs, out_specs=out_specs, out_shape=out_shape,
        compiler_params=_params("arbitrary"), name="rmsnorm",
    )(*ins)


def _router_body(xm, xt, g, wr, h_all, im, it, gm, gt, *, nb, n_exp):
    m = pl.program_id(0)

    def run(x_ref, i_ref, g_ref):
        h = _rms(x_ref[...], g[...])
        rows, d = h.shape
        nt = d // LANE
        for j in range(nt):
            h_all[pl.ds(j, rows, stride=nt), :] = h[:, j * LANE:(j + 1) * LANE]
        if rows * nt < h_all.shape[0]:
            h_all[rows * nt:, :] = jnp.zeros((h_all.shape[0] - rows * nt, LANE), F32)
        logits = jnp.dot(h, wr[...], preferred_element_type=F32, precision=lax.Precision.HIGHEST)
        lane = lax.broadcasted_iota(jnp.int32, logits.shape, 1)
        logits = jnp.where(lane < n_exp, logits, NEG_INF)
        v1 = jnp.max(logits, axis=-1, keepdims=True)
        i1 = jnp.min(jnp.where(logits == v1, lane, LANE), axis=-1, keepdims=True)
        rest = jnp.where(lane == i1, NEG_INF, logits)
        v2 = jnp.max(rest, axis=-1, keepdims=True)
        i2 = jnp.min(jnp.where(rest == v2, lane, LANE), axis=-1, keepdims=True)
        e = jnp.exp(v2 - v1)
        g1 = 1.0 / (1.0 + e)
        g2 = e / (1.0 + e)
        i_ref[...] = jnp.where(lane == 0, i1, jnp.where(lane == 1, i2, 0))
        g_ref[...] = jnp.where(lane == 0, g1, jnp.where(lane == 1, g2, 0.0))

    @pl.when(m < nb)
    def _():
        run(xm, im, gm)

    @pl.when(m == nb)
    def _():
        run(xt, it, gt)


def _norm_router(xm, xt, gain, w_router, tm):
    npr, d = xm.shape
    n_exp = w_router.shape[-1]
    nb = npr // tm
    wr = jnp.pad(w_router, ((0, 0), (0, LANE - n_exp)))
    main = pl.BlockSpec((tm, d), lambda m: (jnp.minimum(m, nb - 1), 0))
    tail = pl.BlockSpec((TAIL, d), lambda m: (0, 0))
    rmain = pl.BlockSpec((tm, LANE), lambda m: (jnp.minimum(m, nb - 1), 0))
    rtail = pl.BlockSpec((TAIL, LANE), lambda m: (0, 0))
    return pl.pallas_call(
        functools.partial(_router_body, nb=nb, n_exp=n_exp),
        grid=(nb + 1,),
        in_specs=[main, tail, pl.BlockSpec((1, d), lambda m: (0, 0)),
                  pl.BlockSpec((d, LANE), lambda m: (0, 0))],
        out_specs=[pl.BlockSpec((tm * (d // LANE), LANE), lambda m: (m, 0)), rmain, rtail, rmain, rtail],
        out_shape=[jax.ShapeDtypeStruct(((npr + tm) * (d // LANE), LANE), F32),
                   jax.ShapeDtypeStruct((npr, LANE), jnp.int32), jax.ShapeDtypeStruct((TAIL, LANE), jnp.int32),
                   jax.ShapeDtypeStruct((npr, LANE), F32), jax.ShapeDtypeStruct((TAIL, LANE), F32)],
        compiler_params=_params("arbitrary"), name="rmsnorm_router",
    )(xm, xt, gain.reshape(1, d), wr)


def _mm_body(*refs, nb, has_res, has_gain, n_out):
    refs = list(refs)
    xm, xt, w = refs[:3]
    pos = 3
    rm = rt = gain = None
    if has_res:
        rm, rt = refs[pos:pos + 2]
        pos += 2
    if has_gain:
        gain = refs[pos]
        pos += 1
    outs = refs[pos:pos + 2 * n_out]
    wb = refs[pos + 2 * n_out]
    m = pl.program_id(1)

    @pl.when(m == 0)
    def _():
        wb[...] = w[...].astype(BF16)

    def run(x_ref, r_ref, o_refs):
        acc = _dot(x_ref[...], wb[...])
        if has_gain:
            acc = _group_rms(acc, HEAD_DIM) * gain[...]
        if has_res:
            acc = acc + r_ref[...]
        for o in o_refs:
            o[...] = acc.astype(o.dtype)

    @pl.when(m < nb)
    def _():
        run(xm, rm, outs[0::2])

    @pl.when(m == nb)
    def _():
        run(xt, rt, outs[1::2])


def _matmul(xm, xt, w, layer, col0, n_cols, tm, tn, out_dtypes, res=None, gain=None):
    npr, k = xm.shape
    nb = npr // tm
    cb = col0 // tn
    xmain = pl.BlockSpec((tm, k), lambda n, m: (jnp.minimum(m, nb - 1), 0))
    xtail = pl.BlockSpec((TAIL, k), lambda n, m: (0, 0))
    omain = pl.BlockSpec((tm, tn), lambda n, m: (jnp.minimum(m, nb - 1), n))
    otail = pl.BlockSpec((TAIL, tn), lambda n, m: (0, n))
    if w.ndim == 3:
        wspec = pl.BlockSpec((None, k, tn), lambda n, m: (layer, 0, cb + n))
    else:
        wspec = pl.BlockSpec((k, tn), lambda n, m: (0, cb + n))
    ins, in_specs = [xm, xt, w], [xmain, xtail, wspec]
    if res is not None:
        ins += list(res)
        in_specs += [omain, otail]
    if gain is not None:
        ins.append(gain.reshape(1, n_cols))
        in_specs.append(pl.BlockSpec((1, tn), lambda n, m: (0, n)))
    out_shape, out_specs = [], []
    for dm, dt in out_dtypes:
        out_shape += [jax.ShapeDtypeStruct((npr, n_cols), dm), jax.ShapeDtypeStruct((TAIL, n_cols), dt)]
        out_specs += [omain, otail]
    return pl.pallas_call(
        functools.partial(_mm_body, nb=nb, has_res=res is not None, has_gain=gain is not None,
                          n_out=len(out_dtypes)),
        grid=(n_cols // tn, nb + 1), in_specs=in_specs, out_specs=out_specs, out_shape=out_shape,
        scratch_shapes=[pltpu.VMEM((k, tn), BF16)],
        compiler_params=_params("arbitrary", "arbitrary"), name="matmul",
    )(*ins)


def _conv_body(xm, xt, w_b, w_c, w_v, cw, p0, p1, gm, gt, ut, wbs, wcs, wvs, carry, *, nb, bps):
    m = pl.program_id(1)

    @pl.when(m == 0)
    def _():
        wbs[...] = w_b[...].astype(BF16)
        wcs[...] = w_c[...].astype(BF16)
        wvs[...] = w_v[...].astype(BF16)

    w0, w1, w2 = cw[0:1, :], cw[1:2, :], cw[2:3, :]

    @pl.when(m < nb)
    def _():
        x = xm[...]
        u = _dot(x, wcs[...]) * _dot(x, wvs[...])
        b = _dot(x, wbs[...])

        @pl.when(m % bps == 0)
        def _():
            carry[...] = jnp.zeros_like(carry)

        prev1 = carry[7:8, :]
        prev2 = carry[6:7, :]
        row = lax.broadcasted_iota(jnp.int32, u.shape, 0)
        u1 = jnp.where(row == 0, prev1, pltpu.roll(u, 1, 0))
        u2 = jnp.where(row == 0, prev2, jnp.where(row == 1, prev1, pltpu.roll(u, 2, 0)))
        gm[...] = (b * (w0 * u2 + w1 * u1 + w2 * u)).astype(gm.dtype)
        tail_rows = u[u.shape[0] - 8:, :]
        carry[...] = tail_rows
        ut[...] = tail_rows

    @pl.when(m == nb)
    def _():
        x = xt[...]
        u = _dot(x, wcs[...]) * _dot(x, wvs[...])
        b = _dot(x, wbs[...])
        gt[...] = (b * (w0 * p0[...] + w1 * p1[...] + w2 * u)).astype(gt.dtype)
        ut[...] = u[0:8, :]


def _conv_mixer(hm, ht, w_in, conv_w, layer, prev0, prev1, tm, tn, seq_len):
    npr, d = hm.shape
    nb = npr // tm
    nn = d // tn
    xmain = pl.BlockSpec((tm, d), lambda n, m: (jnp.minimum(m, nb - 1), 0))
    xtail = pl.BlockSpec((TAIL, d), lambda n, m: (0, 0))
    omain = pl.BlockSpec((tm, tn), lambda n, m: (jnp.minimum(m, nb - 1), n))
    otail = pl.BlockSpec((TAIL, tn), lambda n, m: (0, n))

    def wspec(part):
        return pl.BlockSpec((None, d, tn), lambda n, m: (layer, 0, part * nn + n))

    return pl.pallas_call(
        functools.partial(_conv_body, nb=nb, bps=seq_len // tm),
        grid=(nn, nb + 1),
        in_specs=[xmain, xtail, wspec(0), wspec(1), wspec(2),
                  pl.BlockSpec((None, CONV_W, tn), lambda n, m: (layer, 0, n)), otail, otail],
        out_specs=[omain, otail, pl.BlockSpec((8, tn), lambda n, m: (m, n))],
        out_shape=[jax.ShapeDtypeStruct((npr, d), BF16), jax.ShapeDtypeStruct((TAIL, d), BF16),
                   jax.ShapeDtypeStruct(((nb + 1) * 8, d), F32)],
        scratch_shapes=[pltpu.VMEM((d, tn), BF16)] * 3 + [pltpu.VMEM((8, tn), F32)],
        compiler_params=_params("arbitrary", "arbitrary"), name="conv_mixer",
    )(hm, ht, w_in, w_in, w_in, conv_w, prev0, prev1)


def _ffn_rows(x, wgs, wus, wds):
    g = _dot(x, wgs[...])
    a = (g * jax.nn.sigmoid(g) * _dot(x, wus[...])).astype(BF16)
    return _dot(a, wds[...])


def _cast_weights(wg, wu, wd, wgs, wus, wds):
    wgs[...] = wg[...].astype(BF16)
    wus[...] = wu[...].astype(BF16)
    wds[...] = wd[...].astype(BF16)


def _ffn_dense_body(xm, xt, wg, wu, wd, om, ot, wgs, wus, wds, *, nb):
    c = pl.program_id(0)
    f = pl.program_id(1)

    @pl.when(c < nb)
    def _():
        @pl.when(f == 0)
        def _():
            om[...] = jnp.zeros_like(om)

        _cast_weights(wg, wu, wd, wgs, wus, wds)
        om[...] += _ffn_rows(xm[...], wgs, wus, wds)

    @pl.when(c == nb)
    def _():
        @pl.when(f == 0)
        def _():
            ot[...] = jnp.zeros_like(ot)

        _cast_weights(wg, wu, wd, wgs, wus, wds)
        ot[...] += _ffn_rows(xt[...], wgs, wus, wds)


def _ffn_dense(hm, ht, w_gate, w_up, w_down, layer, tm, tf):
    npr, d = hm.shape
    nb = npr // tm
    ff = w_gate.shape[-1]
    main = pl.BlockSpec((tm, d), lambda c, f: (jnp.minimum(c, nb - 1), 0))
    tail = pl.BlockSpec((TAIL, d), lambda c, f: (0, 0))
    return pl.pallas_call(
        functools.partial(_ffn_dense_body, nb=nb),
        grid=(nb + 1, ff // tf),
        in_specs=[main, tail,
                  pl.BlockSpec((None, d, tf), lambda c, f: (layer, 0, f)),
                  pl.BlockSpec((None, d, tf), lambda c, f: (layer, 0, f)),
                  pl.BlockSpec((None, tf, d), lambda c, f: (layer, f, 0))],
        out_specs=[main, tail],
        out_shape=[jax.ShapeDtypeStruct((npr, d), F32), jax.ShapeDtypeStruct((TAIL, d), F32)],
        scratch_shapes=[pltpu.VMEM((d, tf), BF16), pltpu.VMEM((d, tf), BF16), pltpu.VMEM((tf, d), BF16)],
        compiler_params=_params("arbitrary", "arbitrary"), name="ffn_dense",
    )(hm, ht, w_gate, w_up, w_down)


def _ffn_moe_body(ce, cn, xs, wg, wu, wd, o, wgs, wus, wds, *, sub):
    c = pl.program_id(0)
    f = pl.program_id(1)
    nv = cn[c]

    @pl.when(f == 0)
    def _():
        o[...] = jnp.zeros_like(o)

    @pl.when(nv == o.shape[0])
    def _():
        _cast_weights(wg, wu, wd, wgs, wus, wds)
        o[...] += _ffn_rows(xs[...], wgs, wus, wds)

    @pl.when(jnp.logical_and(nv > 0, nv < o.shape[0]))
    def _():
        _cast_weights(wg, wu, wd, wgs, wus, wds)

        def step(i, carry):
            r = pl.ds(pl.multiple_of(i * sub, sub), sub)
            o[r, :] += _ffn_rows(xs[r, :], wgs, wus, wds)
            return carry

        lax.fori_loop(0, (nv + sub - 1) // sub, step, 0)


def _ffn_moe(xs, chunk_expert, chunk_rows, w_gate, w_up, w_down, layer, rows, tf, sub):
    p, d = xs.shape
    ff = w_gate.shape[-1]
    nf = ff // tf

    def fsel(c, f, cn):
        return jnp.where(cn[c] > 0, f, nf - 1)

    xspec = pl.BlockSpec((rows, d), lambda c, f, ce, cn: (c, 0))
    return pl.pallas_call(
        functools.partial(_ffn_moe_body, sub=sub),
        grid_spec=pltpu.PrefetchScalarGridSpec(
            num_scalar_prefetch=2, grid=(p // rows, nf),
            in_specs=[xspec,
                      pl.BlockSpec((None, None, d, tf), lambda c, f, ce, cn: (layer, ce[c], 0, fsel(c, f, cn))),
                      pl.BlockSpec((None, None, d, tf), lambda c, f, ce, cn: (layer, ce[c], 0, fsel(c, f, cn))),
                      pl.BlockSpec((None, None, tf, d), lambda c, f, ce, cn: (layer, ce[c], fsel(c, f, cn), 0))],
            out_specs=xspec,
            scratch_shapes=[pltpu.VMEM((d, tf), BF16), pltpu.VMEM((d, tf), BF16), pltpu.VMEM((tf, d), BF16)]),
        out_shape=jax.ShapeDtypeStruct((p, d), F32),
        compiler_params=_params("arbitrary", "arbitrary"), name="ffn_moe",
    )(chunk_expert, chunk_rows, xs, w_gate, w_up, w_down)


def _row_copy(src, src_row, dst, dst_row, sem, n=1):
    return pltpu.make_async_copy(src.at[pl.ds(src_row * n, n)], dst.at[pl.ds(dst_row * n, n)], sem)


def _dispatch_body(tok, h, o, buf, sem, *, gb):
    base = pl.program_id(0) * gb
    nt = o.shape[1] // LANE

    def issue(r, carry):
        _row_copy(h, tok[base + r], buf, r, sem, nt).start()
        return carry

    lax.fori_loop(0, gb, issue, 0, unroll=8)

    def drain(r, carry):
        _row_copy(h, 0, buf, r, sem, nt).wait()
        return carry

    lax.fori_loop(0, gb, drain, 0, unroll=8)
    for j in range(nt):
        o[:, j * LANE:(j + 1) * LANE] = buf[pl.ds(j, gb, stride=nt), :].astype(o.dtype)


def _dispatch(h, tok_of_slot, d, gb):
    p = tok_of_slot.shape[0]
    return pl.pallas_call(
        functools.partial(_dispatch_body, gb=gb),
        grid_spec=pltpu.PrefetchScalarGridSpec(
            num_scalar_prefetch=1, grid=(p // gb,),
            in_specs=[pl.BlockSpec(memory_space=pl.ANY)],
            out_specs=pl.BlockSpec((gb, d), lambda i, tok: (i, 0)),
            scratch_shapes=[pltpu.VMEM((gb * (d // LANE), LANE), F32), pltpu.SemaphoreType.DMA]),
        out_shape=jax.ShapeDtypeStruct((p, d), BF16),
        compiler_params=_params("arbitrary"), name="moe_dispatch",
    )(tok_of_slot, h)


def _combine_body(s1, s2, xm, xt, gm, gt, yb, om, ot, abuf, bbuf, sem, *, nb, tb, npr):
    m = pl.program_id(0)

    def run(x_ref, g_ref, o_ref, base, rows):
        def issue(r, carry):
            _row_copy(yb, s1[base + r], abuf, r, sem).start()
            _row_copy(yb, s2[base + r], bbuf, r, sem).start()
            return carry

        lax.fori_loop(0, rows, issue, 0, unroll=8)

        def drain(r, carry):
            _row_copy(yb, 0, abuf, r, sem).wait()
            _row_copy(yb, 0, bbuf, r, sem).wait()
            return carry

        lax.fori_loop(0, rows, drain, 0, unroll=8)
        g = g_ref[...]
        o_ref[...] = x_ref[...] + g[:, 0:1] * abuf[0:rows, :] + g[:, 1:2] * bbuf[0:rows, :]

    @pl.when(m < nb)
    def _():
        run(xm, gm, om, m * tb, tb)

    @pl.when(m == nb)
    def _():
        run(xt, gt, ot, npr, TAIL)


def _combine(xm, xt, gm, gt, yb, slot1, slot2, tb):
    npr, d = xm.shape
    nb = npr // tb
    main = pl.BlockSpec((tb, d), lambda m, s1, s2: (jnp.minimum(m, nb - 1), 0))
    tail = pl.BlockSpec((TAIL, d), lambda m, s1, s2: (0, 0))
    gmain = pl.BlockSpec((tb, LANE), lambda m, s1, s2: (jnp.minimum(m, nb - 1), 0))
    gtail = pl.BlockSpec((TAIL, LANE), lambda m, s1, s2: (0, 0))
    return pl.pallas_call(
        functools.partial(_combine_body, nb=nb, tb=tb, npr=npr),
        grid_spec=pltpu.PrefetchScalarGridSpec(
            num_scalar_prefetch=2, grid=(nb + 1,),
            in_specs=[main, tail, gmain, gtail, pl.BlockSpec(memory_space=pl.ANY)],
            out_specs=[main, tail],
            scratch_shapes=[pltpu.VMEM((tb, d), F32), pltpu.VMEM((tb, d), F32), pltpu.SemaphoreType.DMA]),
        out_shape=[jax.ShapeDtypeStruct((npr, d), F32), jax.ShapeDtypeStruct((TAIL, d), F32)],
        compiler_params=_params("arbitrary"), name="moe_combine",
    )(slot1, slot2, xm, xt, gm, gt, yb)


def _route(route_i, route_i_tail, n_tok, n_exp, rows):
    npr = route_i.shape[0]
    ids = jnp.concatenate([route_i[:, :TOP_K], route_i_tail[:n_tok - npr, :TOP_K]], axis=0)
    flat_e = ids.reshape(-1)
    onehot = (flat_e[:, None] == jnp.arange(n_exp, dtype=jnp.int32)[None, :]).astype(jnp.int32)
    before = jnp.cumsum(onehot, axis=0) - onehot
    rank = jnp.sum(before * onehot, axis=1)
    counts = jnp.sum(onehot, axis=0)
    seg = (counts + rows - 1) // rows * rows
    seg_end = jnp.cumsum(seg)
    seg_start = seg_end - seg
    slot = (seg_start[flat_e] + rank).astype(jnp.int32)
    n_chunks = (n_tok * TOP_K) // rows + n_exp
    tok = jnp.repeat(jnp.arange(n_tok, dtype=jnp.int32), TOP_K)
    tok_of_slot = jnp.zeros((n_chunks * rows,), jnp.int32).at[slot].set(tok)
    chunk_start = jnp.arange(n_chunks, dtype=jnp.int32) * rows
    chunk_e = jnp.minimum(jnp.searchsorted(seg_end, chunk_start, side="right"), n_exp - 1).astype(jnp.int32)
    chunk_rows = jnp.clip(counts[chunk_e] - (chunk_start - seg_start[chunk_e]), 0, rows).astype(jnp.int32)
    last_used = jnp.max(jnp.where(chunk_rows > 0, jnp.arange(n_chunks), 0))
    chunk_e = jnp.where(chunk_rows > 0, chunk_e, chunk_e[last_used]).astype(jnp.int32)
    slots = slot.reshape(n_tok, TOP_K)
    pad = npr + TAIL - n_tok
    slot1 = jnp.pad(slots[:, 0], (0, pad))
    slot2 = jnp.pad(slots[:, 1], (0, pad))
    return tok_of_slot, chunk_e, chunk_rows, slot1, slot2


def _lam(lam_ref, lam0):
    lp = lam_ref[...]
    a = jnp.sum(lp[0:1, :] * lp[1:2, :], axis=-1, keepdims=True)
    b = jnp.sum(lp[2:3, :] * lp[3:4, :], axis=-1, keepdims=True)
    return jnp.exp(a) - jnp.exp(b) + lam0


def _attn_body(q_ref, k_ref, v_ref, sl_ref, lam_ref, sg_ref, o_ref, vt_sc, m_sc, l_sc, acc_sc, *, blk, lam0):
    i = pl.program_id(2)

    @pl.when(i == 0)
    def _():
        for c in range(vt_sc.shape[0]):
            vt_sc[c] = v_ref[c * blk:(c + 1) * blk, :].T.astype(BF16)

    slope = sl_ref[0:1, 0:1]
    q = (q_ref[:, 0:HEAD_DIM], q_ref[:, HEAD_DIM:2 * HEAD_DIM])
    rel = (lax.broadcasted_iota(jnp.int32, (blk, blk), 0)
           - lax.broadcasted_iota(jnp.int32, (blk, blk), 1)).astype(F32)
    bias_rel = slope * rel
    m_sc[...] = jnp.full_like(m_sc, NEG_INF)
    l_sc[...] = jnp.zeros_like(l_sc)
    acc_sc[...] = jnp.zeros_like(acc_sc)

    def step(j, diagonal):
        k = k_ref[pl.ds(pl.multiple_of(j * blk, blk), blk), :]
        vt = vt_sc[j]
        bias = bias_rel + slope * ((j - i) * blk).astype(F32)
        for c in range(2):
            s = _dot_nt(k[:, c * HEAD_DIM:(c + 1) * HEAD_DIM], q[c]) + bias
            if diagonal:
                s = jnp.where(rel <= 0.0, s, NEG_INF)
            m_old = m_sc[c]
            m_new = jnp.maximum(m_old, jnp.max(s, axis=0, keepdims=True))
            alpha = jnp.exp2(m_old - m_new)
            p = jnp.exp2(s - m_new)
            l_sc[c] = alpha * l_sc[c] + jnp.sum(p, axis=0, keepdims=True)
            acc_sc[c] = alpha * acc_sc[c] + _dot(vt, p.astype(BF16))
            m_sc[c] = m_new

    def off_diagonal(j, carry):
        step(j, False)
        return carry

    lax.fori_loop(0, i, off_diagonal, 0)
    step(i, True)

    o = acc_sc[0] * (1.0 / l_sc[0]) - _lam(lam_ref, lam0) * (acc_sc[1] * (1.0 / l_sc[1]))
    ms = jnp.mean(o * o, axis=0, keepdims=True)
    o = (o * lax.rsqrt(ms + EPS)).T
    o_ref[...] = (o * sg_ref[...] * (1.0 - lam0)).astype(o_ref.dtype)


def _attn_prompt(q, k, v, slopes, lam_p, sub_gain, n_batch, seq_len, blk, lam0):
    npr, d = q.shape
    n_heads = d // V_DIM
    nq = seq_len // blk
    kv_spec = pl.BlockSpec((seq_len, V_DIM), lambda b, h, i: (b, h))
    return pl.pallas_call(
        functools.partial(_attn_body, blk=blk, lam0=lam0),
        grid=(n_batch, n_heads, nq),
        in_specs=[pl.BlockSpec((blk, V_DIM), lambda b, h, i: (b * nq + i, h)), kv_spec, kv_spec,
                  pl.BlockSpec((None, 1, LANE), lambda b, h, i: (h, 0, 0)),
                  pl.BlockSpec((4, HEAD_DIM), lambda b, h, i: (0, 0)),
                  pl.BlockSpec((1, V_DIM), lambda b, h, i: (0, 0))],
        out_specs=pl.BlockSpec((blk, V_DIM), lambda b, h, i: (b * nq + i, h)),
        out_shape=jax.ShapeDtypeStruct((npr, d), BF16),
        scratch_shapes=[pltpu.VMEM((nq, V_DIM, blk), BF16), pltpu.VMEM((2, 1, blk), F32),
                        pltpu.VMEM((2, 1, blk), F32), pltpu.VMEM((2, V_DIM, blk), F32)],
        compiler_params=_params("arbitrary", "arbitrary", "arbitrary"), name="attn_prompt",
    )(q, k, v, slopes, lam_p, sub_gain)


def _attn_sample_body(pt, q_ref, kn_ref, vn_ref, sl_ref, lam_ref, sg_ref, *refs, pps, page, q_pos, lam0):
    k_refs = refs[:pps]
    v_refs = refs[pps:2 * pps]
    o_ref, m_sc, l_sc, acc_sc = refs[2 * pps:]
    p = pl.program_id(1)
    n_rows = q_ref.shape[0]
    n_heads = n_rows // 2
    cols = page * n_heads
    row = lax.broadcasted_iota(jnp.int32, (n_rows, cols), 0)
    col = lax.broadcasted_iota(jnp.int32, (n_rows, cols), 1)
    own = jnp.bitwise_and(col, n_heads - 1) == jnp.bitwise_and(row, n_heads - 1)
    key = lax.shift_right_logical(lax.broadcasted_iota(jnp.int32, (1, cols), 1), n_heads.bit_length() - 1)
    slope = sl_ref[:, 0:1]

    @pl.when(p == 0)
    def _():
        m_sc[...] = jnp.full_like(m_sc, NEG_INF)
        l_sc[...] = jnp.zeros_like(l_sc)
        acc_sc[...] = jnp.zeros_like(acc_sc)

    def update(s, pv_fn):
        m_old = m_sc[...]
        m_new = jnp.maximum(m_old, jnp.max(s, axis=-1, keepdims=True))
        alpha = jnp.exp2(m_old - m_new)
        pr = jnp.exp2(s - m_new)
        l_sc[...] = alpha * l_sc[...] + jnp.sum(pr, axis=-1, keepdims=True)
        acc_sc[...] = alpha * acc_sc[...] + pv_fn(pr)
        m_sc[...] = m_new

    qb = q_ref[...].astype(BF16)
    scores = []
    for i in range(pps):
        s_c = [_dot_nt(qb, k_refs[i][pl.ds(c, cols, stride=2), :].astype(BF16)) for c in range(2)]
        s = jnp.concatenate([s_c[0][:n_heads], s_c[1][n_heads:]], axis=0)
        dist = (q_pos - ((p * pps + i) * page + key)).astype(F32)
        scores.append(jnp.where(own, s - slope * dist, NEG_INF))

    def weighted_values(pr):
        pr = pr.astype(BF16)
        return sum(_dot(pr[:, i * cols:(i + 1) * cols], v_refs[i][...].astype(BF16)) for i in range(pps))

    update(jnp.concatenate(scores, axis=1), weighted_values)

    @pl.when(p == pl.num_programs(1) - 1)
    def _():
        s_new = jnp.sum(q_ref[...] * kn_ref[...], axis=-1, keepdims=True)
        vn = vn_ref[...]
        update(s_new, lambda pr: pr * jnp.concatenate([vn, vn], axis=0))
        o_hc = acc_sc[...] * (1.0 / l_sc[...])
        o = o_hc[:n_heads] - _lam(lam_ref, lam0) * o_hc[n_heads:]
        o_ref[...] = _rms(o, sg_ref[...]) * (1.0 - lam0)


def _attn_sample(q, k_new, v_new, cache_k, cache_v, page_table, slope_rows, lam_p, sub_gain, pps, lam0):
    n_seq, n_rows, _ = q.shape
    n_heads = n_rows // 2
    assert n_heads & (n_heads - 1) == 0
    n_pages = page_table.shape[1]
    n_pool, page = cache_k.shape[:2]
    ck = cache_k.reshape(n_pool * page * n_rows, HEAD_DIM)
    cv = cache_v.reshape(n_pool * page * n_heads, V_DIM)
    qspec = pl.BlockSpec((None, n_rows, HEAD_DIM), lambda b, p, pt: (b, 0, 0))
    vspec = pl.BlockSpec((None, n_heads, V_DIM), lambda b, p, pt: (b, 0, 0))

    def page_idx(b, p, pt, i):
        return pt[b * n_pages + p * pps + i]

    k_specs = [pl.BlockSpec((page * n_rows, HEAD_DIM), lambda b, p, pt, i=i: (page_idx(b, p, pt, i), 0))
               for i in range(pps)]
    v_specs = [pl.BlockSpec((page * n_heads, V_DIM), lambda b, p, pt, i=i: (page_idx(b, p, pt, i), 0))
               for i in range(pps)]
    return pl.pallas_call(
        functools.partial(_attn_sample_body, pps=pps, page=page, q_pos=n_pages * page, lam0=lam0),
        grid_spec=pltpu.PrefetchScalarGridSpec(
            num_scalar_prefetch=1, grid=(n_seq, n_pages // pps),
            in_specs=[qspec, qspec, vspec,
                      pl.BlockSpec((n_rows, LANE), lambda b, p, pt: (0, 0)),
                      pl.BlockSpec((4, HEAD_DIM), lambda b, p, pt: (0, 0)),
                      pl.BlockSpec((1, V_DIM), lambda b, p, pt: (0, 0))] + k_specs + v_specs,
            out_specs=vspec,
            scratch_shapes=[pltpu.VMEM((n_rows, 1), F32), pltpu.VMEM((n_rows, 1), F32),
                            pltpu.VMEM((n_rows, V_DIM), F32)]),
        out_shape=jax.ShapeDtypeStruct((n_seq, n_heads, V_DIM), F32),
        compiler_params=_params("arbitrary", "arbitrary"), name="attn_sample",
    )(page_table.reshape(-1), q, k_new, v_new, slope_rows, lam_p, sub_gain, *([ck] * pps), *([cv] * pps))


def _tiles(seq_len, d_model, d_ff, d_ff_e):
    tm = min(1024, seq_len)
    return dict(
        tm=tm,
        tr=min(256, tm),
        tn=min(512, d_model),
        tn_conv=min(256, d_model),
        tf=256,
        sub=min(512, tm),
        moe_rows=tm,
        gather_rows=min(256, tm),
        attn_blk=min(512, seq_len),
        pages_per_step=4,
    )


def kernel(x_prompt, x_sample, state_conv, cache_k, cache_v, page_table, mix_norm, ffn_norm, a_w_in, a_conv,
           a_w_out, kv_norm, w_kv, k_norm, b_w_q, b_q_norm, b_lam, b_subln, b_w_o, ffn_w_gate, ffn_w_up,
           ffn_w_down, moe_router, moe_w_gate, moe_w_up, moe_w_down):
    n_batch, seq_len, d = x_prompt.shape
    n_seq = x_sample.shape[0]
    assert x_sample.shape[1] == 1 and n_seq <= TAIL
    depth = mix_norm.shape[0]
    n_a = a_w_in.shape[0]
    n_heads = d // V_DIM
    n_exp = moe_router.shape[-1]
    k_width = n_heads * 2 * HEAD_DIM
    npr = n_batch * seq_len
    n_tok = npr + n_seq
    t = _tiles(seq_len, d, ffn_w_gate.shape[-1], moe_w_gate.shape[-1])
    tm, tn, tr = t["tm"], t["tn"], t["tr"]
    nb = npr // tm
    bps = seq_len // tm

    xm = x_prompt.reshape(npr, d)
    xt = jnp.pad(x_sample.reshape(n_seq, d), ((0, TAIL - n_seq), (0, 0)))

    slopes = jnp.exp2(-8.0 * jnp.arange(1, n_heads + 1, dtype=F32) / n_heads) * LOG2E
    slopes_lane = jnp.broadcast_to(slopes[:, None, None], (n_heads, 1, LANE))
    slope_rows = jnp.broadcast_to(jnp.tile(slopes, 2)[:, None], (2 * n_heads, LANE))
    k_gain = jnp.tile(k_norm.reshape(-1), n_heads)

    def comp_major(a):
        return a[:n_seq].reshape(n_seq, n_heads, 2, HEAD_DIM).transpose(0, 2, 1, 3).reshape(n_seq, 2 * n_heads, HEAD_DIM)

    def pad_tail(a):
        return jnp.pad(a, ((0, TAIL - n_seq), (0, 0)))

    conv_prompt, conv_sample = [], []
    pending = None
    k_f32 = v_f32 = k_bf = v_bf = None
    for l in range(depth):
        if pending is None:
            hm, ht = _norm(xm, xt, mix_norm[l], tr)
        else:
            hm, ht, xm, xt = _norm(xm, xt, mix_norm[l], tr, add=pending)
            pending = None
        if l < n_a:
            prev0 = pad_tail(state_conv[l, :, 0, :])
            prev1 = pad_tail(state_conv[l, :, 1, :])
            gm, gt, ut = _conv_mixer(hm, ht, a_w_in, a_conv, l, prev0, prev1, tm, t["tn_conv"], seq_len)
            ut = ut.reshape(nb + 1, 8, d)
            conv_prompt.append(ut[bps - 1:nb:bps, 6:8, :])
            conv_sample.append(jnp.stack([state_conv[l, :, 1, :], ut[nb, :n_seq, :]], axis=1))
            xm, xt = _matmul(gm, gt, a_w_out, l, 0, d, tm, tn, [(F32, F32)], res=(xm, xt))
        else:
            j = l - n_a
            if k_f32 is None:
                nm, nt = _norm(xm, xt, kv_norm, tr)
                k_f32, kt_f32, k_bf, _ = _matmul(nm, nt, w_kv, 0, 0, k_width, tm, tn,
                                                 [(F32, F32), (BF16, BF16)], gain=k_gain)
                v_f32, vt_f32 = _matmul(nm, nt, w_kv, 0, k_width, d, tm, tn, [(F32, F32)])
            lam0 = 0.8 - 0.6 * math.exp(-0.3 * l)
            q_gain = jnp.tile(b_q_norm[j].reshape(-1), n_heads) * (HEAD_DIM ** -0.5 * LOG2E)
            qm, qt = _matmul(hm, ht, b_w_q, j, 0, k_width, tm, tn, [(BF16, F32)], gain=q_gain)
            sub_gain = b_subln[j].reshape(1, V_DIM)
            om = _attn_prompt(qm, k_bf, v_f32, slopes_lane, b_lam[j], sub_gain, n_batch, seq_len,
                              t["attn_blk"], lam0)
            os_ = _attn_sample(comp_major(qt), comp_major(kt_f32), vt_f32[:n_seq].reshape(n_seq, n_heads, V_DIM),
                               cache_k, cache_v, page_table, slope_rows, b_lam[j], sub_gain,
                               t["pages_per_step"], lam0)
            ot = pad_tail(os_.reshape(n_seq, d)).astype(BF16)
            xm, xt = _matmul(om, ot, b_w_o, j, 0, d, tm, tn, [(F32, F32)], res=(xm, xt))
        i = l // 2
        if l % 2 == 0:
            h2m, h2t = _norm(xm, xt, ffn_norm[l], tr)
            pending = _ffn_dense(h2m, h2t, ffn_w_gate, ffn_w_up, ffn_w_down, i, tm, t["tf"])
        else:
            h2, rim, rit, rgm, rgt = _norm_router(xm, xt, ffn_norm[l], moe_router[i], tr)
            tok_of_slot, chunk_e, chunk_rows, slot1, slot2 = _route(rim, rit, n_tok, n_exp, t["moe_rows"])
            xs = _dispatch(h2, tok_of_slot, d, t["gather_rows"])
            yb = _ffn_moe(xs, chunk_e, chunk_rows, moe_w_gate, moe_w_up, moe_w_down, i,
                          t["moe_rows"], t["tf"], t["sub"])
            valid = (jnp.arange(TAIL) < n_seq)[:, None]
            xm, xt = _combine(xm, xt, rgm, jnp.where(valid, rgt, 0.0), yb, slot1, slot2, t["gather_rows"])
    if pending is not None:
        xm, xt = xm + pending[0], xt + pending[1]

    y_prompt = xm.reshape(n_batch, seq_len, d)
    y_sample = xt[:n_seq].reshape(n_seq, 1, d)
    return (y_prompt, y_sample, jnp.stack(conv_prompt), jnp.stack(conv_sample),
            k_f32.reshape(n_batch, seq_len, n_heads, 2, HEAD_DIM),
            v_f32.reshape(n_batch, seq_len, n_heads, V_DIM),
            kt_f32[:n_seq].reshape(n_seq, 1, n_heads, 2, HEAD_DIM),
            vt_f32[:n_seq].reshape(n_seq, 1, n_heads, V_DIM))
```

```python
import functools
import math

import jax
import jax.numpy as jnp
from jax import lax
from jax.experimental import pallas as pl
from jax.experimental.pallas import tpu as pltpu

F32 = jnp.float32
BF16 = jnp.bfloat16
EPS = 1e-6
HEAD_DIM = 128
V_DIM = 2 * HEAD_DIM
TOP_K = 2
CONV_W = 3
LANE = 128
TAIL = 16
V7X_VMEM_BYTES = 64 * 1024 * 1024
VMEM_LIMIT = V7X_VMEM_BYTES - 8 * 1024 * 1024
NEG_INF = float("-inf")
LOG2E = 1.0 / math.log(2.0)


def _params(*sem):
    return pltpu.CompilerParams(dimension_semantics=sem, vmem_limit_bytes=VMEM_LIMIT)


def _dot(a, b):
    return jnp.dot(a, b, preferred_element_type=F32)


def _dot_nt(a, b):
    return lax.dot_general(a, b, (((1,), (1,)), ((), ())), preferred_element_type=F32)


def _rms(x, g):
    ms = jnp.mean(x * x, axis=-1, keepdims=True)
    return x * lax.rsqrt(ms + EPS) * g


def _group_rms(x, gsize):
    outs = []
    for g in range(x.shape[-1] // gsize):
        blk = x[:, g * gsize:(g + 1) * gsize]
        ms = jnp.mean(blk * blk, axis=-1, keepdims=True)
        outs.append(blk * lax.rsqrt(ms + EPS))
    return outs[0] if len(outs) == 1 else jnp.concatenate(outs, axis=-1)


def _norm_body(*refs, nb, has_add):
    if has_add:
        xm, xt, ym, yt, g, om, ot, sm, st = refs
    else:
        xm, xt, g, om, ot = refs
        ym = yt = sm = st = None
    m = pl.program_id(0)

    def run(x_ref, y_ref, o_ref, s_ref):
        x = x_ref[...]
        if has_add:
            x = x + y_ref[...]
            s_ref[...] = x
        o_ref[...] = _rms(x, g[...]).astype(o_ref.dtype)

    @pl.when(m < nb)
    def _():
        run(xm, ym, om, sm)

    @pl.when(m == nb)
    def _():
        run(xt, yt, ot, st)


def _norm(xm, xt, gain, tm, add=None, out_dtype=BF16):
    npr, d = xm.shape
    nb = npr // tm
    main = pl.BlockSpec((tm, d), lambda m: (jnp.minimum(m, nb - 1), 0))
    tail = pl.BlockSpec((TAIL, d), lambda m: (0, 0))
    gspec = pl.BlockSpec((1, d), lambda m: (0, 0))
    ins = [xm, xt]
    in_specs = [main, tail]
    out_shape = [jax.ShapeDtypeStruct((npr, d), out_dtype), jax.ShapeDtypeStruct((TAIL, d), out_dtype)]
    out_specs = [main, tail]
    if add is not None:
        ins += list(add)
        in_specs += [main, tail]
        out_shape += [jax.ShapeDtypeStruct((npr, d), F32), jax.ShapeDtypeStruct((TAIL, d), F32)]
        out_specs += [main, tail]
    ins.append(gain.reshape(1, d))
    in_specs.append(gspec)
    return pl.pallas_call(
        functools.partial(_norm_body, nb=nb, has_add=add is not None),
        grid=(nb + 1,), in_specs=in_specs, out_specs=out_specs, out_shape=out_shape,
        compiler_params=_params("arbitrary"), name="rmsnorm",
    )(*ins)


def _router_body(xm, xt, g, wr, h_all, im, it, gm, gt, *, nb, n_exp):
    m = pl.program_id(0)

    def run(x_ref, i_ref, g_ref):
        h = _rms(x_ref[...], g[...])
        rows, d = h.shape
        pk = d // (2 * LANE)
        bits = lax.bitcast_convert_type(h.astype(BF16).astype(F32), jnp.uint32)
        for j in range(pk):
            lo = lax.shift_right_logical(bits[:, j * LANE:(j + 1) * LANE], jnp.uint32(16))
            hi = bits[:, d // 2 + j * LANE:d // 2 + (j + 1) * LANE]
            h_all[pl.ds(j, rows, stride=pk), :] = jnp.bitwise_or(hi, lo)
        if rows * pk < h_all.shape[0]:
            h_all[rows * pk:, :] = jnp.zeros((h_all.shape[0] - rows * pk, LANE), jnp.uint32)
        logits = jnp.dot(h, wr[...], preferred_element_type=F32, precision=lax.Precision.HIGHEST)
        lane = lax.broadcasted_iota(jnp.int32, logits.shape, 1)
        logits = jnp.where(lane < n_exp, logits, NEG_INF)
        v1 = jnp.max(logits, axis=-1, keepdims=True)
        i1 = jnp.min(jnp.where(logits == v1, lane, LANE), axis=-1, keepdims=True)
        rest = jnp.where(lane == i1, NEG_INF, logits)
        v2 = jnp.max(rest, axis=-1, keepdims=True)
        i2 = jnp.min(jnp.where(rest == v2, lane, LANE), axis=-1, keepdims=True)
        e = jnp.exp(v2 - v1)
        g1 = 1.0 / (1.0 + e)
        g2 = e / (1.0 + e)
        i_ref[...] = jnp.where(lane == 0, i1, jnp.where(lane == 1, i2, 0))
        g_ref[...] = jnp.where(lane == 0, g1, jnp.where(lane == 1, g2, 0.0))

    @pl.when(m < nb)
    def _():
        run(xm, im, gm)

    @pl.when(m == nb)
    def _():
        run(xt, it, gt)


def _norm_router(xm, xt, gain, w_router, tm):
    npr, d = xm.shape
    n_exp = w_router.shape[-1]
    nb = npr // tm
    wr = jnp.pad(w_router, ((0, 0), (0, LANE - n_exp)))
    main = pl.BlockSpec((tm, d), lambda m: (jnp.minimum(m, nb - 1), 0))
    tail = pl.BlockSpec((TAIL, d), lambda m: (0, 0))
    rmain = pl.BlockSpec((tm, LANE), lambda m: (jnp.minimum(m, nb - 1), 0))
    rtail = pl.BlockSpec((TAIL, LANE), lambda m: (0, 0))
    return pl.pallas_call(
        functools.partial(_router_body, nb=nb, n_exp=n_exp),
        grid=(nb + 1,),
        in_specs=[main, tail, pl.BlockSpec((1, d), lambda m: (0, 0)),
                  pl.BlockSpec((d, LANE), lambda m: (0, 0))],
        out_specs=[pl.BlockSpec((tm * (d // (2 * LANE)), LANE), lambda m: (m, 0)), rmain, rtail, rmain, rtail],
        out_shape=[jax.ShapeDtypeStruct(((npr + tm) * (d // (2 * LANE)), LANE), jnp.uint32),
                   jax.ShapeDtypeStruct((npr, LANE), jnp.int32), jax.ShapeDtypeStruct((TAIL, LANE), jnp.int32),
                   jax.ShapeDtypeStruct((npr, LANE), F32), jax.ShapeDtypeStruct((TAIL, LANE), F32)],
        compiler_params=_params("arbitrary"), name="rmsnorm_router",
    )(xm, xt, gain.reshape(1, d), wr)


def _mm_body(*refs, nb, has_res, has_gain, n_out):
    refs = list(refs)
    xm, xt, w = refs[:3]
    pos = 3
    rm = rt = gain = None
    if has_res:
        rm, rt = refs[pos:pos + 2]
        pos += 2
    if has_gain:
        gain = refs[pos]
        pos += 1
    outs = refs[pos:pos + 2 * n_out]
    wb = refs[pos + 2 * n_out]
    m = pl.program_id(1)

    @pl.when(m == 0)
    def _():
        wb[...] = w[...].astype(BF16)

    def run(x_ref, r_ref, o_refs):
        acc = _dot(x_ref[...], wb[...])
        if has_gain:
            acc = _group_rms(acc, HEAD_DIM) * gain[...]
        if has_res:
            acc = acc + r_ref[...]
        for o in o_refs:
            o[...] = acc.astype(o.dtype)

    @pl.when(m < nb)
    def _():
        run(xm, rm, outs[0::2])

    @pl.when(m == nb)
    def _():
        run(xt, rt, outs[1::2])


def _matmul(xm, xt, w, layer, col0, n_cols, tm, tn, out_dtypes, res=None, gain=None):
    npr, k = xm.shape
    nb = npr // tm
    cb = col0 // tn
    xmain = pl.BlockSpec((tm, k), lambda n, m: (jnp.minimum(m, nb - 1), 0))
    xtail = pl.BlockSpec((TAIL, k), lambda n, m: (0, 0))
    omain = pl.BlockSpec((tm, tn), lambda n, m: (jnp.minimum(m, nb - 1), n))
    otail = pl.BlockSpec((TAIL, tn), lambda n, m: (0, n))
    if w.ndim == 3:
        wspec = pl.BlockSpec((None, k, tn), lambda n, m: (layer, 0, cb + n))
    else:
        wspec = pl.BlockSpec((k, tn), lambda n, m: (0, cb + n))
    ins, in_specs = [xm, xt, w], [xmain, xtail, wspec]
    if res is not None:
        ins += list(res)
        in_specs += [omain, otail]
    if gain is not None:
        ins.append(gain.reshape(1, n_cols))
        in_specs.append(pl.BlockSpec((1, tn), lambda n, m: (0, n)))
    out_shape, out_specs = [], []
    for dm, dt in out_dtypes:
        out_shape += [jax.ShapeDtypeStruct((npr, n_cols), dm), jax.ShapeDtypeStruct((TAIL, n_cols), dt)]
        out_specs += [omain, otail]
    return pl.pallas_call(
        functools.partial(_mm_body, nb=nb, has_res=res is not None, has_gain=gain is not None,
                          n_out=len(out_dtypes)),
        grid=(n_cols // tn, nb + 1), in_specs=in_specs, out_specs=out_specs, out_shape=out_shape,
        scratch_shapes=[pltpu.VMEM((k, tn), BF16)],
        compiler_params=_params("arbitrary", "arbitrary"), name="matmul",
    )(*ins)


def _conv_body(xm, xt, w_b, w_c, w_v, cw, p0, p1, gm, gt, ut, wbs, wcs, wvs, carry, *, nb, bps):
    m = pl.program_id(1)

    @pl.when(m == 0)
    def _():
        wbs[...] = w_b[...].astype(BF16)
        wcs[...] = w_c[...].astype(BF16)
        wvs[...] = w_v[...].astype(BF16)

    w0, w1, w2 = cw[0:1, :], cw[1:2, :], cw[2:3, :]

    @pl.when(m < nb)
    def _():
        x = xm[...]
        u = _dot(x, wcs[...]) * _dot(x, wvs[...])
        b = _dot(x, wbs[...])

        @pl.when(m % bps == 0)
        def _():
            carry[...] = jnp.zeros_like(carry)

        prev1 = carry[7:8, :]
        prev2 = carry[6:7, :]
        row = lax.broadcasted_iota(jnp.int32, u.shape, 0)
        u1 = jnp.where(row == 0, prev1, pltpu.roll(u, 1, 0))
        u2 = jnp.where(row == 0, prev2, jnp.where(row == 1, prev1, pltpu.roll(u, 2, 0)))
        gm[...] = (b * (w0 * u2 + w1 * u1 + w2 * u)).astype(gm.dtype)
        tail_rows = u[u.shape[0] - 8:, :]
        carry[...] = tail_rows
        ut[...] = tail_rows

    @pl.when(m == nb)
    def _():
        x = xt[...]
        u = _dot(x, wcs[...]) * _dot(x, wvs[...])
        b = _dot(x, wbs[...])
        gt[...] = (b * (w0 * p0[...] + w1 * p1[...] + w2 * u)).astype(gt.dtype)
        ut[...] = u[0:8, :]


def _conv_mixer(hm, ht, w_in, conv_w, layer, prev0, prev1, tm, tn, seq_len):
    npr, d = hm.shape
    nb = npr // tm
    nn = d // tn
    xmain = pl.BlockSpec((tm, d), lambda n, m: (jnp.minimum(m, nb - 1), 0))
    xtail = pl.BlockSpec((TAIL, d), lambda n, m: (0, 0))
    omain = pl.BlockSpec((tm, tn), lambda n, m: (jnp.minimum(m, nb - 1), n))
    otail = pl.BlockSpec((TAIL, tn), lambda n, m: (0, n))

    def wspec(part):
        return pl.BlockSpec((None, d, tn), lambda n, m: (layer, 0, part * nn + n))

    return pl.pallas_call(
        functools.partial(_conv_body, nb=nb, bps=seq_len // tm),
        grid=(nn, nb + 1),
        in_specs=[xmain, xtail, wspec(0), wspec(1), wspec(2),
                  pl.BlockSpec((None, CONV_W, tn), lambda n, m: (layer, 0, n)), otail, otail],
        out_specs=[omain, otail, pl.BlockSpec((8, tn), lambda n, m: (m, n))],
        out_shape=[jax.ShapeDtypeStruct((npr, d), BF16), jax.ShapeDtypeStruct((TAIL, d), BF16),
                   jax.ShapeDtypeStruct(((nb + 1) * 8, d), F32)],
        scratch_shapes=[pltpu.VMEM((d, tn), BF16)] * 3 + [pltpu.VMEM((8, tn), F32)],
        compiler_params=_params("arbitrary", "arbitrary"), name="conv_mixer",
    )(hm, ht, w_in, w_in, w_in, conv_w, prev0, prev1)


def _ffn_rows(x, wgs, wus, wds):
    g = _dot(x, wgs[...])
    a = (g * jax.nn.sigmoid(g) * _dot(x, wus[...])).astype(BF16)
    return _dot(a, wds[...])


def _cast_weights(wg, wu, wd, wgs, wus, wds):
    wgs[...] = wg[...].astype(BF16)
    wus[...] = wu[...].astype(BF16)
    wds[...] = wd[...].astype(BF16)


def _ffn_dense_body(xm, xt, wg, wu, wd, om, ot, wgs, wus, wds, *, nb):
    c = pl.program_id(0)
    f = pl.program_id(1)

    @pl.when(c < nb)
    def _():
        @pl.when(f == 0)
        def _():
            om[...] = jnp.zeros_like(om)

        _cast_weights(wg, wu, wd, wgs, wus, wds)
        om[...] += _ffn_rows(xm[...], wgs, wus, wds)

    @pl.when(c == nb)
    def _():
        @pl.when(f == 0)
        def _():
            ot[...] = jnp.zeros_like(ot)

        _cast_weights(wg, wu, wd, wgs, wus, wds)
        ot[...] += _ffn_rows(xt[...], wgs, wus, wds)


def _ffn_dense(hm, ht, w_gate, w_up, w_down, layer, tm, tf):
    npr, d = hm.shape
    nb = npr // tm
    ff = w_gate.shape[-1]
    main = pl.BlockSpec((tm, d), lambda c, f: (jnp.minimum(c, nb - 1), 0))
    tail = pl.BlockSpec((TAIL, d), lambda c, f: (0, 0))
    return pl.pallas_call(
        functools.partial(_ffn_dense_body, nb=nb),
        grid=(nb + 1, ff // tf),
        in_specs=[main, tail,
                  pl.BlockSpec((None, d, tf), lambda c, f: (layer, 0, f)),
                  pl.BlockSpec((None, d, tf), lambda c, f: (layer, 0, f)),
                  pl.BlockSpec((None, tf, d), lambda c, f: (layer, f, 0))],
        out_specs=[main, tail],
        out_shape=[jax.ShapeDtypeStruct((npr, d), F32), jax.ShapeDtypeStruct((TAIL, d), F32)],
        scratch_shapes=[pltpu.VMEM((d, tf), BF16), pltpu.VMEM((d, tf), BF16), pltpu.VMEM((tf, d), BF16)],
        compiler_params=_params("arbitrary", "arbitrary"), name="ffn_dense",
    )(hm, ht, w_gate, w_up, w_down)


def _ffn_moe_body(ce, cn, tok, h, wg, wu, wd, o, gbuf, xsb, wgs, wus, wds, sem, *, sub, per_step):
    c = pl.program_id(0)
    f = pl.program_id(1)
    rows, d = o.shape
    pk = d // (2 * LANE)
    nv = cn[c]
    slot = lax.rem(c, 2)

    def issue(chunk, first, count, s):
        def body(i, carry):
            r = first + i
            t = tok[chunk * rows + r]
            pltpu.make_async_copy(h.at[pl.ds(t * pk, pk)], gbuf.at[s, pl.ds(r * pk, pk)], sem.at[s]).start()
            return carry

        lax.fori_loop(0, count, body, 0, unroll=8)

    @pl.when(jnp.logical_and(jnp.logical_and(c == 0, f == 0), nv > 0))
    def _():
        issue(0, 0, rows, 0)

    @pl.when(f == 0)
    def _():
        o[...] = jnp.zeros_like(o)

        @pl.when(nv > 0)
        def _():
            pltpu.make_async_copy(h.at[pl.ds(0, rows * pk)], gbuf.at[slot], sem.at[slot]).wait()
            for j in range(pk):
                w = gbuf[slot, pl.ds(j, rows, stride=pk), :]
                lo = lax.bitcast_convert_type(lax.shift_left(w, jnp.uint32(16)), F32)
                hi = lax.bitcast_convert_type(jnp.bitwise_and(w, jnp.uint32(0xFFFF0000)), F32)
                xsb[:, j * LANE:(j + 1) * LANE] = lo.astype(BF16)
                xsb[:, d // 2 + j * LANE:d // 2 + (j + 1) * LANE] = hi.astype(BF16)

    @pl.when(jnp.logical_and(f < rows // per_step, cn[c + 1] > 0))
    def _():
        issue(c + 1, f * per_step, per_step, 1 - slot)

    @pl.when(nv == rows)
    def _():
        _cast_weights(wg, wu, wd, wgs, wus, wds)
        o[...] += _ffn_rows(xsb[...], wgs, wus, wds)

    @pl.when(jnp.logical_and(nv > 0, nv < rows))
    def _():
        _cast_weights(wg, wu, wd, wgs, wus, wds)

        def step(i, carry):
            r = pl.ds(pl.multiple_of(i * sub, sub), sub)
            o[r, :] += _ffn_rows(xsb[r, :], wgs, wus, wds)
            return carry

        lax.fori_loop(0, (nv + sub - 1) // sub, step, 0)


def _ffn_moe(h, tok_of_slot, chunk_expert, chunk_rows, w_gate, w_up, w_down, layer, d, rows, tf, sub):
    p = tok_of_slot.shape[0]
    n_chunks = p // rows
    ff = w_gate.shape[-1]
    nf = ff // tf
    pk = d // (2 * LANE)
    per_step = next(q for q in range(-(-rows // nf), rows + 1) if rows % q == 0)

    def fsel(c, f, cn):
        return jnp.where(cn[c] > 0, f, nf - 1)

    return pl.pallas_call(
        functools.partial(_ffn_moe_body, sub=sub, per_step=per_step),
        grid_spec=pltpu.PrefetchScalarGridSpec(
            num_scalar_prefetch=3, grid=(n_chunks, nf),
            in_specs=[pl.BlockSpec(memory_space=pl.ANY),
                      pl.BlockSpec((None, None, d, tf),
                                   lambda c, f, ce, cn, tok: (layer, ce[c], 0, fsel(c, f, cn))),
                      pl.BlockSpec((None, None, d, tf),
                                   lambda c, f, ce, cn, tok: (layer, ce[c], 0, fsel(c, f, cn))),
                      pl.BlockSpec((None, None, tf, d),
                                   lambda c, f, ce, cn, tok: (layer, ce[c], fsel(c, f, cn), 0))],
            out_specs=pl.BlockSpec((rows, d), lambda c, f, ce, cn, tok: (c, 0)),
            scratch_shapes=[pltpu.VMEM((2, rows * pk, LANE), jnp.uint32), pltpu.VMEM((rows, d), BF16),
                            pltpu.VMEM((d, tf), BF16), pltpu.VMEM((d, tf), BF16), pltpu.VMEM((tf, d), BF16),
                            pltpu.SemaphoreType.DMA((2,))]),
        out_shape=jax.ShapeDtypeStruct((p, d), F32),
        compiler_params=_params("arbitrary", "arbitrary"), name="ffn_moe",
    )(chunk_expert, jnp.pad(chunk_rows, (0, 1)), tok_of_slot, h, w_gate, w_up, w_down)


def _row_copy(src, src_row, dst, dst_row, sem):
    return pltpu.make_async_copy(src.at[pl.ds(src_row, 1)], dst.at[pl.ds(dst_row, 1)], sem)


def _combine_body(s1, s2, xm, xt, gm, gt, yb, om, ot, abuf, bbuf, sem, *, nb, tb, npr):
    m = pl.program_id(0)

    def run(x_ref, g_ref, o_ref, base, rows):
        def issue(r, carry):
            _row_copy(yb, s1[base + r], abuf, r, sem).start()
            _row_copy(yb, s2[base + r], bbuf, r, sem).start()
            return carry

        lax.fori_loop(0, rows, issue, 0, unroll=8)

        def drain(r, carry):
            _row_copy(yb, 0, abuf, r, sem).wait()
            _row_copy(yb, 0, bbuf, r, sem).wait()
            return carry

        lax.fori_loop(0, rows, drain, 0, unroll=8)
        g = g_ref[...]
        o_ref[...] = x_ref[...] + g[:, 0:1] * abuf[0:rows, :] + g[:, 1:2] * bbuf[0:rows, :]

    @pl.when(m < nb)
    def _():
        run(xm, gm, om, m * tb, tb)

    @pl.when(m == nb)
    def _():
        run(xt, gt, ot, npr, TAIL)


def _combine(xm, xt, gm, gt, yb, slot1, slot2, tb):
    npr, d = xm.shape
    nb = npr // tb
    main = pl.BlockSpec((tb, d), lambda m, s1, s2: (jnp.minimum(m, nb - 1), 0))
    tail = pl.BlockSpec((TAIL, d), lambda m, s1, s2: (0, 0))
    gmain = pl.BlockSpec((tb, LANE), lambda m, s1, s2: (jnp.minimum(m, nb - 1), 0))
    gtail = pl.BlockSpec((TAIL, LANE), lambda m, s1, s2: (0, 0))
    return pl.pallas_call(
        functools.partial(_combine_body, nb=nb, tb=tb, npr=npr),
        grid_spec=pltpu.PrefetchScalarGridSpec(
            num_scalar_prefetch=2, grid=(nb + 1,),
            in_specs=[main, tail, gmain, gtail, pl.BlockSpec(memory_space=pl.ANY)],
            out_specs=[main, tail],
            scratch_shapes=[pltpu.VMEM((tb, d), F32), pltpu.VMEM((tb, d), F32), pltpu.SemaphoreType.DMA]),
        out_shape=[jax.ShapeDtypeStruct((npr, d), F32), jax.ShapeDtypeStruct((TAIL, d), F32)],
        compiler_params=_params("arbitrary"), name="moe_combine",
    )(slot1, slot2, xm, xt, gm, gt, yb)


def _route(route_i, route_i_tail, n_tok, n_exp, rows):
    npr = route_i.shape[0]
    ids = jnp.concatenate([route_i[:, :TOP_K], route_i_tail[:n_tok - npr, :TOP_K]], axis=0)
    flat_e = ids.reshape(-1)
    onehot = (flat_e[:, None] == jnp.arange(n_exp, dtype=jnp.int32)[None, :]).astype(jnp.int32)
    before = jnp.cumsum(onehot, axis=0) - onehot
    rank = jnp.sum(before * onehot, axis=1)
    counts = jnp.sum(onehot, axis=0)
    seg = (counts + rows - 1) // rows * rows
    seg_end = jnp.cumsum(seg)
    seg_start = seg_end - seg
    slot = (seg_start[flat_e] + rank).astype(jnp.int32)
    n_chunks = (n_tok * TOP_K) // rows + n_exp
    tok = jnp.repeat(jnp.arange(n_tok, dtype=jnp.int32), TOP_K)
    tok_of_slot = jnp.zeros((n_chunks * rows,), jnp.int32).at[slot].set(tok)
    chunk_start = jnp.arange(n_chunks, dtype=jnp.int32) * rows
    chunk_e = jnp.minimum(jnp.searchsorted(seg_end, chunk_start, side="right"), n_exp - 1).astype(jnp.int32)
    chunk_rows = jnp.clip(counts[chunk_e] - (chunk_start - seg_start[chunk_e]), 0, rows).astype(jnp.int32)
    last_used = jnp.max(jnp.where(chunk_rows > 0, jnp.arange(n_chunks), 0))
    chunk_e = jnp.where(chunk_rows > 0, chunk_e, chunk_e[last_used]).astype(jnp.int32)
    slots = slot.reshape(n_tok, TOP_K)
    pad = npr + TAIL - n_tok
    slot1 = jnp.pad(slots[:, 0], (0, pad))
    slot2 = jnp.pad(slots[:, 1], (0, pad))
    return tok_of_slot, chunk_e, chunk_rows, slot1, slot2


def _lam(lam_ref, lam0):
    lp = lam_ref[...]
    a = jnp.sum(lp[0:1, :] * lp[1:2, :], axis=-1, keepdims=True)
    b = jnp.sum(lp[2:3, :] * lp[3:4, :], axis=-1, keepdims=True)
    return jnp.exp(a) - jnp.exp(b) + lam0


def _attn_body(q_ref, k_ref, v_ref, sl_ref, lam_ref, sg_ref, o_ref, vt_sc, m_sc, l_sc, acc_sc, *, blk, lam0):
    i = pl.program_id(2)

    @pl.when(i == 0)
    def _():
        for c in range(vt_sc.shape[0]):
            vt_sc[c] = v_ref[c * blk:(c + 1) * blk, :].T.astype(BF16)

    slope = sl_ref[0:1, 0:1]
    q = (q_ref[:, 0:HEAD_DIM], q_ref[:, HEAD_DIM:2 * HEAD_DIM])
    rel = (lax.broadcasted_iota(jnp.int32, (blk, blk), 0)
           - lax.broadcasted_iota(jnp.int32, (blk, blk), 1)).astype(F32)
    bias_rel = slope * rel
    m_sc[...] = jnp.full_like(m_sc, NEG_INF)
    l_sc[...] = jnp.zeros_like(l_sc)
    acc_sc[...] = jnp.zeros_like(acc_sc)

    def step(j, diagonal):
        k = k_ref[pl.ds(pl.multiple_of(j * blk, blk), blk), :]
        vt = vt_sc[j]
        bias = bias_rel + slope * ((j - i) * blk).astype(F32)
        for c in range(2):
            s = _dot_nt(k[:, c * HEAD_DIM:(c + 1) * HEAD_DIM], q[c]) + bias
            if diagonal:
                s = jnp.where(rel <= 0.0, s, NEG_INF)
            m_old = m_sc[c]
            m_new = jnp.maximum(m_old, jnp.max(s, axis=0, keepdims=True))
            alpha = jnp.exp2(m_old - m_new)
            p = jnp.exp2(s - m_new)
            l_sc[c] = alpha * l_sc[c] + jnp.sum(p, axis=0, keepdims=True)
            acc_sc[c] = alpha * acc_sc[c] + _dot(vt, p.astype(BF16))
            m_sc[c] = m_new

    def off_diagonal(j, carry):
        step(j, False)
        return carry

    lax.fori_loop(0, i, off_diagonal, 0)
    step(i, True)

    o = acc_sc[0] * (1.0 / l_sc[0]) - _lam(lam_ref, lam0) * (acc_sc[1] * (1.0 / l_sc[1]))
    ms = jnp.mean(o * o, axis=0, keepdims=True)
    o = (o * lax.rsqrt(ms + EPS)).T
    o_ref[...] = (o * sg_ref[...] * (1.0 - lam0)).astype(o_ref.dtype)


def _attn_prompt(q, k, v, slopes, lam_p, sub_gain, n_batch, seq_len, blk, lam0):
    npr, d = q.shape
    n_heads = d // V_DIM
    nq = seq_len // blk
    kv_spec = pl.BlockSpec((seq_len, V_DIM), lambda b, h, i: (b, h))
    return pl.pallas_call(
        functools.partial(_attn_body, blk=blk, lam0=lam0),
        grid=(n_batch, n_heads, nq),
        in_specs=[pl.BlockSpec((blk, V_DIM), lambda b, h, i: (b * nq + i, h)), kv_spec, kv_spec,
                  pl.BlockSpec((None, 1, LANE), lambda b, h, i: (h, 0, 0)),
                  pl.BlockSpec((4, HEAD_DIM), lambda b, h, i: (0, 0)),
                  pl.BlockSpec((1, V_DIM), lambda b, h, i: (0, 0))],
        out_specs=pl.BlockSpec((blk, V_DIM), lambda b, h, i: (b * nq + i, h)),
        out_shape=jax.ShapeDtypeStruct((npr, d), BF16),
        scratch_shapes=[pltpu.VMEM((nq, V_DIM, blk), BF16), pltpu.VMEM((2, 1, blk), F32),
                        pltpu.VMEM((2, 1, blk), F32), pltpu.VMEM((2, V_DIM, blk), F32)],
        compiler_params=_params("arbitrary", "arbitrary", "arbitrary"), name="attn_prompt",
    )(q, k, v, slopes, lam_p, sub_gain)


def _attn_sample_body(pt, q_ref, kn_ref, vn_ref, sl_ref, lam_ref, sg_ref, *refs, pps, page, q_pos, lam0):
    k_refs = refs[:pps]
    v_refs = refs[pps:2 * pps]
    o_ref, m_sc, l_sc, acc_sc = refs[2 * pps:]
    p = pl.program_id(1)
    n_rows = q_ref.shape[0]
    n_heads = n_rows // 2
    cols = page * n_heads
    row = lax.broadcasted_iota(jnp.int32, (n_rows, cols), 0)
    col = lax.broadcasted_iota(jnp.int32, (n_rows, cols), 1)
    own = jnp.bitwise_and(col, n_heads - 1) == jnp.bitwise_and(row, n_heads - 1)
    key = lax.shift_right_logical(lax.broadcasted_iota(jnp.int32, (1, cols), 1), n_heads.bit_length() - 1)
    slope = sl_ref[:, 0:1]

    @pl.when(p == 0)
    def _():
        m_sc[...] = jnp.full_like(m_sc, NEG_INF)
        l_sc[...] = jnp.zeros_like(l_sc)
        acc_sc[...] = jnp.zeros_like(acc_sc)

    def update(s, pv_fn):
        m_old = m_sc[...]
        m_new = jnp.maximum(m_old, jnp.max(s, axis=-1, keepdims=True))
        alpha = jnp.exp2(m_old - m_new)
        pr = jnp.exp2(s - m_new)
        l_sc[...] = alpha * l_sc[...] + jnp.sum(pr, axis=-1, keepdims=True)
        acc_sc[...] = alpha * acc_sc[...] + pv_fn(pr)
        m_sc[...] = m_new

    qb = q_ref[...].astype(BF16)
    scores = []
    for i in range(pps):
        s_c = [_dot_nt(qb, k_refs[i][pl.ds(c, cols, stride=2), :].astype(BF16)) for c in range(2)]
        s = jnp.concatenate([s_c[0][:n_heads], s_c[1][n_heads:]], axis=0)
        dist = (q_pos - ((p * pps + i) * page + key)).astype(F32)
        scores.append(jnp.where(own, s - slope * dist, NEG_INF))

    def weighted_values(pr):
        pr = pr.astype(BF16)
        return sum(_dot(pr[:, i * cols:(i + 1) * cols], v_refs[i][...].astype(BF16)) for i in range(pps))

    update(jnp.concatenate(scores, axis=1), weighted_values)

    @pl.when(p == pl.num_programs(1) - 1)
    def _():
        s_new = jnp.sum(q_ref[...] * kn_ref[...], axis=-1, keepdims=True)
        vn = vn_ref[...]
        update(s_new, lambda pr: pr * jnp.concatenate([vn, vn], axis=0))
        o_hc = acc_sc[...] * (1.0 / l_sc[...])
        o = o_hc[:n_heads] - _lam(lam_ref, lam0) * o_hc[n_heads:]
        o_ref[...] = _rms(o, sg_ref[...]) * (1.0 - lam0)


def _attn_sample(q, k_new, v_new, cache_k, cache_v, page_table, slope_rows, lam_p, sub_gain, pps, lam0):
    n_seq, n_rows, _ = q.shape
    n_heads = n_rows // 2
    assert n_heads & (n_heads - 1) == 0
    n_pages = page_table.shape[1]
    n_pool, page = cache_k.shape[:2]
    ck = cache_k.reshape(n_pool * page * n_rows, HEAD_DIM)
    cv = cache_v.reshape(n_pool * page * n_heads, V_DIM)
    qspec = pl.BlockSpec((None, n_rows, HEAD_DIM), lambda b, p, pt: (b, 0, 0))
    vspec = pl.BlockSpec((None, n_heads, V_DIM), lambda b, p, pt: (b, 0, 0))

    def page_idx(b, p, pt, i):
        return pt[b * n_pages + p * pps + i]

    k_specs = [pl.BlockSpec((page * n_rows, HEAD_DIM), lambda b, p, pt, i=i: (page_idx(b, p, pt, i), 0))
               for i in range(pps)]
    v_specs = [pl.BlockSpec((page * n_heads, V_DIM), lambda b, p, pt, i=i: (page_idx(b, p, pt, i), 0))
               for i in range(pps)]
    return pl.pallas_call(
        functools.partial(_attn_sample_body, pps=pps, page=page, q_pos=n_pages * page, lam0=lam0),
        grid_spec=pltpu.PrefetchScalarGridSpec(
            num_scalar_prefetch=1, grid=(n_seq, n_pages // pps),
            in_specs=[qspec, qspec, vspec,
                      pl.BlockSpec((n_rows, LANE), lambda b, p, pt: (0, 0)),
                      pl.BlockSpec((4, HEAD_DIM), lambda b, p, pt: (0, 0)),
                      pl.BlockSpec((1, V_DIM), lambda b, p, pt: (0, 0))] + k_specs + v_specs,
            out_specs=vspec,
            scratch_shapes=[pltpu.VMEM((n_rows, 1), F32), pltpu.VMEM((n_rows, 1), F32),
                            pltpu.VMEM((n_rows, V_DIM), F32)]),
        out_shape=jax.ShapeDtypeStruct((n_seq, n_heads, V_DIM), F32),
        compiler_params=_params("arbitrary", "arbitrary"), name="attn_sample",
    )(page_table.reshape(-1), q, k_new, v_new, slope_rows, lam_p, sub_gain, *([ck] * pps), *([cv] * pps))


def _tiles(seq_len, d_model, d_ff, d_ff_e):
    tm = min(1024, seq_len)
    return dict(
        tm=tm,
        tr=min(256, tm),
        tn=min(512, d_model),
        tn_conv=min(256, d_model),
        tf=256,
        sub=min(512, tm),
        moe_rows=tm,
        gather_rows=min(256, tm),
        attn_blk=min(512, seq_len),
        pages_per_step=4,
    )


def kernel(x_prompt, x_sample, state_conv, cache_k, cache_v, page_table, mix_norm, ffn_norm, a_w_in, a_conv,
           a_w_out, kv_norm, w_kv, k_norm, b_w_q, b_q_norm, b_lam, b_subln, b_w_o, ffn_w_gate, ffn_w_up,
           ffn_w_down, moe_router, moe_w_gate, moe_w_up, moe_w_down):
    n_batch, seq_len, d = x_prompt.shape
    n_seq = x_sample.shape[0]
    assert x_sample.shape[1] == 1 and n_seq <= TAIL
    depth = mix_norm.shape[0]
    n_a = a_w_in.shape[0]
    n_heads = d // V_DIM
    n_exp = moe_router.shape[-1]
    k_width = n_heads * 2 * HEAD_DIM
    npr = n_batch * seq_len
    n_tok = npr + n_seq
    t = _tiles(seq_len, d, ffn_w_gate.shape[-1], moe_w_gate.shape[-1])
    tm, tn, tr = t["tm"], t["tn"], t["tr"]
    nb = npr // tm
    bps = seq_len // tm

    xm = x_prompt.reshape(npr, d)
    xt = jnp.pad(x_sample.reshape(n_seq, d), ((0, TAIL - n_seq), (0, 0)))

    slopes = jnp.exp2(-8.0 * jnp.arange(1, n_heads + 1, dtype=F32) / n_heads) * LOG2E
    slopes_lane = jnp.broadcast_to(slopes[:, None, None], (n_heads, 1, LANE))
    slope_rows = jnp.broadcast_to(jnp.tile(slopes, 2)[:, None], (2 * n_heads, LANE))
    k_gain = jnp.tile(k_norm.reshape(-1), n_heads)

    def comp_major(a):
        return a[:n_seq].reshape(n_seq, n_heads, 2, HEAD_DIM).transpose(0, 2, 1, 3).reshape(n_seq, 2 * n_heads, HEAD_DIM)

    def pad_tail(a):
        return jnp.pad(a, ((0, TAIL - n_seq), (0, 0)))

    conv_prompt, conv_sample = [], []
    pending = None
    k_f32 = v_f32 = k_bf = v_bf = None
    for l in range(depth):
        if pending is None:
            hm, ht = _norm(xm, xt, mix_norm[l], tr)
        else:
            hm, ht, xm, xt = _norm(xm, xt, mix_norm[l], tr, add=pending)
            pending = None
        if l < n_a:
            prev0 = pad_tail(state_conv[l, :, 0, :])
            prev1 = pad_tail(state_conv[l, :, 1, :])
            gm, gt, ut = _conv_mixer(hm, ht, a_w_in, a_conv, l, prev0, prev1, tm, t["tn_conv"], seq_len)
            ut = ut.reshape(nb + 1, 8, d)
            conv_prompt.append(ut[bps - 1:nb:bps, 6:8, :])
            conv_sample.append(jnp.stack([state_conv[l, :, 1, :], ut[nb, :n_seq, :]], axis=1))
            xm, xt = _matmul(gm, gt, a_w_out, l, 0, d, tm, tn, [(F32, F32)], res=(xm, xt))
        else:
            j = l - n_a
            if k_f32 is None:
                nm, nt = _norm(xm, xt, kv_norm, tr)
                k_f32, kt_f32, k_bf, _ = _matmul(nm, nt, w_kv, 0, 0, k_width, tm, tn,
                                                 [(F32, F32), (BF16, BF16)], gain=k_gain)
                v_f32, vt_f32 = _matmul(nm, nt, w_kv, 0, k_width, d, tm, tn, [(F32, F32)])
            lam0 = 0.8 - 0.6 * math.exp(-0.3 * l)
            q_gain = jnp.tile(b_q_norm[j].reshape(-1), n_heads) * (HEAD_DIM ** -0.5 * LOG2E)
            qm, qt = _matmul(hm, ht, b_w_q, j, 0, k_width, tm, tn, [(BF16, F32)], gain=q_gain)
            sub_gain = b_subln[j].reshape(1, V_DIM)
            om = _attn_prompt(qm, k_bf, v_f32, slopes_lane, b_lam[j], sub_gain, n_batch, seq_len,
                              t["attn_blk"], lam0)
            os_ = _attn_sample(comp_major(qt), comp_major(kt_f32), vt_f32[:n_seq].reshape(n_seq, n_heads, V_DIM),
                               cache_k, cache_v, page_table, slope_rows, b_lam[j], sub_gain,
                               t["pages_per_step"], lam0)
            ot = pad_tail(os_.reshape(n_seq, d)).astype(BF16)
            xm, xt = _matmul(om, ot, b_w_o, j, 0, d, tm, tn, [(F32, F32)], res=(xm, xt))
        i = l // 2
        if l % 2 == 0:
            h2m, h2t = _norm(xm, xt, ffn_norm[l], tr)
            pending = _ffn_dense(h2m, h2t, ffn_w_gate, ffn_w_up, ffn_w_down, i, tm, t["tf"])
        else:
            h2, rim, rit, rgm, rgt = _norm_router(xm, xt, ffn_norm[l], moe_router[i], tr)
            tok_of_slot, chunk_e, chunk_rows, slot1, slot2 = _route(rim, rit, n_tok, n_exp, t["moe_rows"])
            yb = _ffn_moe(h2, tok_of_slot, chunk_e, chunk_rows, moe_w_gate, moe_w_up, moe_w_down, i,
                          d, t["moe_rows"], t["tf"], t["sub"])
            valid = (jnp.arange(TAIL) < n_seq)[:, None]
            xm, xt = _combine(xm, xt, rgm, jnp.where(valid, rgt, 0.0), yb, slot1, slot2, t["gather_rows"])
    if pending is not None:
        xm, xt = xm + pending[0], xt + pending[1]

    y_prompt = xm.reshape(n_batch, seq_len, d)
    y_sample = xt[:n_seq].reshape(n_seq, 1, d)
    return (y_prompt, y_sample, jnp.stack(conv_prompt), jnp.stack(conv_sample),
            k_f32.reshape(n_batch, seq_len, n_heads, 2, HEAD_DIM),
            v_f32.reshape(n_batch, seq_len, n_heads, V_DIM),
            kt_f32[:n_seq].reshape(n_seq, 1, n_heads, 2, HEAD_DIM),
            vt_f32[:n_seq].reshape(n_seq, 1, n_heads, V_DIM))
```

```python
import functools
import math

import jax
import jax.numpy as jnp
from jax import lax
from jax.experimental import pallas as pl
from jax.experimental.pallas import tpu as pltpu

F32 = jnp.float32
BF16 = jnp.bfloat16
EPS = 1e-6
HEAD_DIM = 128
V_DIM = 2 * HEAD_DIM
TOP_K = 2
CONV_W = 3
LANE = 128
TAIL = 16
V7X_VMEM_BYTES = 64 * 1024 * 1024
VMEM_LIMIT = V7X_VMEM_BYTES - 8 * 1024 * 1024
NEG_INF = float("-inf")
LOG2E = 1.0 / math.log(2.0)


def _params(*sem):
    return pltpu.CompilerParams(dimension_semantics=sem, vmem_limit_bytes=VMEM_LIMIT)


def _dot(a, b):
    return jnp.dot(a, b, preferred_element_type=F32)


def _dot_nt(a, b):
    return lax.dot_general(a, b, (((1,), (1,)), ((), ())), preferred_element_type=F32)


def _rms(x, g):
    ms = jnp.mean(x * x, axis=-1, keepdims=True)
    return x * lax.rsqrt(ms + EPS) * g


def _group_rms(x, gsize):
    outs = []
    for g in range(x.shape[-1] // gsize):
        blk = x[:, g * gsize:(g + 1) * gsize]
        ms = jnp.mean(blk * blk, axis=-1, keepdims=True)
        outs.append(blk * lax.rsqrt(ms + EPS))
    return outs[0] if len(outs) == 1 else jnp.concatenate(outs, axis=-1)


def _norm_body(*refs, nb, has_add):
    if has_add:
        xm, xt, ym, yt, g, om, ot, sm, st = refs
    else:
        xm, xt, g, om, ot = refs
        ym = yt = sm = st = None
    m = pl.program_id(0)

    def run(x_ref, y_ref, o_ref, s_ref):
        x = x_ref[...]
        if has_add:
            x = x + y_ref[...]
            s_ref[...] = x
        o_ref[...] = _rms(x, g[...]).astype(o_ref.dtype)

    @pl.when(m < nb)
    def _():
        run(xm, ym, om, sm)

    @pl.when(m == nb)
    def _():
        run(xt, yt, ot, st)


def _norm(xm, xt, gain, tm, add=None, out_dtype=BF16):
    npr, d = xm.shape
    nb = npr // tm
    main = pl.BlockSpec((tm, d), lambda m: (jnp.minimum(m, nb - 1), 0))
    tail = pl.BlockSpec((TAIL, d), lambda m: (0, 0))
    gspec = pl.BlockSpec((1, d), lambda m: (0, 0))
    ins = [xm, xt]
    in_specs = [main, tail]
    out_shape = [jax.ShapeDtypeStruct((npr, d), out_dtype), jax.ShapeDtypeStruct((TAIL, d), out_dtype)]
    out_specs = [main, tail]
    if add is not None:
        ins += list(add)
        in_specs += [main, tail]
        out_shape += [jax.ShapeDtypeStruct((npr, d), F32), jax.ShapeDtypeStruct((TAIL, d), F32)]
        out_specs += [main, tail]
    ins.append(gain.reshape(1, d))
    in_specs.append(gspec)
    return pl.pallas_call(
        functools.partial(_norm_body, nb=nb, has_add=add is not None),
        grid=(nb + 1,), in_specs=in_specs, out_specs=out_specs, out_shape=out_shape,
        compiler_params=_params("arbitrary"), name="rmsnorm",
    )(*ins)


def _router_body(xm, xt, g, wr, h_all, im, it, gm, gt, *, nb, n_exp):
    m = pl.program_id(0)

    def run(x_ref, i_ref, g_ref):
        h = _rms(x_ref[...], g[...])
        rows, d = h.shape
        pk = d // (2 * LANE)
        bits = lax.bitcast_convert_type(h.astype(BF16).astype(F32), jnp.uint32)
        for j in range(pk):
            lo = lax.shift_right_logical(bits[:, j * LANE:(j + 1) * LANE], jnp.uint32(16))
            hi = bits[:, d // 2 + j * LANE:d // 2 + (j + 1) * LANE]
            h_all[pl.ds(j, rows, stride=pk), :] = jnp.bitwise_or(hi, lo)
        if rows * pk < h_all.shape[0]:
            h_all[rows * pk:, :] = jnp.zeros((h_all.shape[0] - rows * pk, LANE), jnp.uint32)
        logits = jnp.dot(h, wr[...], preferred_element_type=F32, precision=lax.Precision.HIGHEST)
        lane = lax.broadcasted_iota(jnp.int32, logits.shape, 1)
        logits = jnp.where(lane < n_exp, logits, NEG_INF)
        v1 = jnp.max(logits, axis=-1, keepdims=True)
        i1 = jnp.min(jnp.where(logits == v1, lane, LANE), axis=-1, keepdims=True)
        rest = jnp.where(lane == i1, NEG_INF, logits)
        v2 = jnp.max(rest, axis=-1, keepdims=True)
        i2 = jnp.min(jnp.where(rest == v2, lane, LANE), axis=-1, keepdims=True)
        e = jnp.exp(v2 - v1)
        g1 = 1.0 / (1.0 + e)
        g2 = e / (1.0 + e)
        i_ref[...] = jnp.where(lane == 0, i1, jnp.where(lane == 1, i2, 0))
        g_ref[...] = jnp.where(lane == 0, g1, jnp.where(lane == 1, g2, 0.0))

    @pl.when(m < nb)
    def _():
        run(xm, im, gm)

    @pl.when(m == nb)
    def _():
        run(xt, it, gt)


def _norm_router(xm, xt, gain, w_router, tm):
    npr, d = xm.shape
    n_exp = w_router.shape[-1]
    nb = npr // tm
    wr = jnp.pad(w_router, ((0, 0), (0, LANE - n_exp)))
    main = pl.BlockSpec((tm, d), lambda m: (jnp.minimum(m, nb - 1), 0))
    tail = pl.BlockSpec((TAIL, d), lambda m: (0, 0))
    rmain = pl.BlockSpec((tm, LANE), lambda m: (jnp.minimum(m, nb - 1), 0))
    rtail = pl.BlockSpec((TAIL, LANE), lambda m: (0, 0))
    return pl.pallas_call(
        functools.partial(_router_body, nb=nb, n_exp=n_exp),
        grid=(nb + 1,),
        in_specs=[main, tail, pl.BlockSpec((1, d), lambda m: (0, 0)),
                  pl.BlockSpec((d, LANE), lambda m: (0, 0))],
        out_specs=[pl.BlockSpec((tm * (d // (2 * LANE)), LANE), lambda m: (m, 0)), rmain, rtail, rmain, rtail],
        out_shape=[jax.ShapeDtypeStruct(((npr + tm) * (d // (2 * LANE)), LANE), jnp.uint32),
                   jax.ShapeDtypeStruct((npr, LANE), jnp.int32), jax.ShapeDtypeStruct((TAIL, LANE), jnp.int32),
                   jax.ShapeDtypeStruct((npr, LANE), F32), jax.ShapeDtypeStruct((TAIL, LANE), F32)],
        compiler_params=_params("arbitrary"), name="rmsnorm_router",
    )(xm, xt, gain.reshape(1, d), wr)


def _mm_body(*refs, nb, has_res, has_gain, n_out):
    refs = list(refs)
    xm, xt, w = refs[:3]
    pos = 3
    rm = rt = gain = None
    if has_res:
        rm, rt = refs[pos:pos + 2]
        pos += 2
    if has_gain:
        gain = refs[pos]
        pos += 1
    outs = refs[pos:pos + 2 * n_out]
    wb = refs[pos + 2 * n_out]
    m = pl.program_id(1)

    @pl.when(m == 0)
    def _():
        wb[...] = w[...].astype(BF16)

    def run(x_ref, r_ref, o_refs):
        acc = _dot(x_ref[...], wb[...])
        if has_gain:
            acc = _group_rms(acc, HEAD_DIM) * gain[...]
        if has_res:
            acc = acc + r_ref[...]
        for o in o_refs:
            o[...] = acc.astype(o.dtype)

    @pl.when(m < nb)
    def _():
        run(xm, rm, outs[0::2])

    @pl.when(m == nb)
    def _():
        run(xt, rt, outs[1::2])


def _matmul(xm, xt, w, layer, col0, n_cols, tm, tn, out_dtypes, res=None, gain=None):
    npr, k = xm.shape
    nb = npr // tm
    cb = col0 // tn
    xmain = pl.BlockSpec((tm, k), lambda n, m: (jnp.minimum(m, nb - 1), 0))
    xtail = pl.BlockSpec((TAIL, k), lambda n, m: (0, 0))
    omain = pl.BlockSpec((tm, tn), lambda n, m: (jnp.minimum(m, nb - 1), n))
    otail = pl.BlockSpec((TAIL, tn), lambda n, m: (0, n))
    if w.ndim == 3:
        wspec = pl.BlockSpec((None, k, tn), lambda n, m: (layer, 0, cb + n))
    else:
        wspec = pl.BlockSpec((k, tn), lambda n, m: (0, cb + n))
    ins, in_specs = [xm, xt, w], [xmain, xtail, wspec]
    if res is not None:
        ins += list(res)
        in_specs += [omain, otail]
    if gain is not None:
        ins.append(gain.reshape(1, n_cols))
        in_specs.append(pl.BlockSpec((1, tn), lambda n, m: (0, n)))
    out_shape, out_specs = [], []
    for dm, dt in out_dtypes:
        out_shape += [jax.ShapeDtypeStruct((npr, n_cols), dm), jax.ShapeDtypeStruct((TAIL, n_cols), dt)]
        out_specs += [omain, otail]
    return pl.pallas_call(
        functools.partial(_mm_body, nb=nb, has_res=res is not None, has_gain=gain is not None,
                          n_out=len(out_dtypes)),
        grid=(n_cols // tn, nb + 1), in_specs=in_specs, out_specs=out_specs, out_shape=out_shape,
        scratch_shapes=[pltpu.VMEM((k, tn), BF16)],
        compiler_params=_params("arbitrary", "arbitrary"), name="matmul",
    )(*ins)


def _conv_body(xm, xt, w_b, w_c, w_v, cw, p0, p1, gm, gt, ut, wbs, wcs, wvs, carry, *, nb, bps):
    m = pl.program_id(1)

    @pl.when(m == 0)
    def _():
        wbs[...] = w_b[...].astype(BF16)
        wcs[...] = w_c[...].astype(BF16)
        wvs[...] = w_v[...].astype(BF16)

    w0, w1, w2 = cw[0:1, :], cw[1:2, :], cw[2:3, :]

    @pl.when(m < nb)
    def _():
        x = xm[...]
        u = _dot(x, wcs[...]) * _dot(x, wvs[...])
        b = _dot(x, wbs[...])

        @pl.when(m % bps == 0)
        def _():
            carry[...] = jnp.zeros_like(carry)

        prev1 = carry[7:8, :]
        prev2 = carry[6:7, :]
        row = lax.broadcasted_iota(jnp.int32, u.shape, 0)
        u1 = jnp.where(row == 0, prev1, pltpu.roll(u, 1, 0))
        u2 = jnp.where(row == 0, prev2, jnp.where(row == 1, prev1, pltpu.roll(u, 2, 0)))
        gm[...] = (b * (w0 * u2 + w1 * u1 + w2 * u)).astype(gm.dtype)
        tail_rows = u[u.shape[0] - 8:, :]
        carry[...] = tail_rows
        ut[...] = tail_rows

    @pl.when(m == nb)
    def _():
        x = xt[...]
        u = _dot(x, wcs[...]) * _dot(x, wvs[...])
        b = _dot(x, wbs[...])
        gt[...] = (b * (w0 * p0[...] + w1 * p1[...] + w2 * u)).astype(gt.dtype)
        ut[...] = u[0:8, :]


def _conv_mixer(hm, ht, w_in, conv_w, layer, prev0, prev1, tm, tn, seq_len):
    npr, d = hm.shape
    nb = npr // tm
    nn = d // tn
    xmain = pl.BlockSpec((tm, d), lambda n, m: (jnp.minimum(m, nb - 1), 0))
    xtail = pl.BlockSpec((TAIL, d), lambda n, m: (0, 0))
    omain = pl.BlockSpec((tm, tn), lambda n, m: (jnp.minimum(m, nb - 1), n))
    otail = pl.BlockSpec((TAIL, tn), lambda n, m: (0, n))

    def wspec(part):
        return pl.BlockSpec((None, d, tn), lambda n, m: (layer, 0, part * nn + n))

    return pl.pallas_call(
        functools.partial(_conv_body, nb=nb, bps=seq_len // tm),
        grid=(nn, nb + 1),
        in_specs=[xmain, xtail, wspec(0), wspec(1), wspec(2),
                  pl.BlockSpec((None, CONV_W, tn), lambda n, m: (layer, 0, n)), otail, otail],
        out_specs=[omain, otail, pl.BlockSpec((8, tn), lambda n, m: (m, n))],
        out_shape=[jax.ShapeDtypeStruct((npr, d), BF16), jax.ShapeDtypeStruct((TAIL, d), BF16),
                   jax.ShapeDtypeStruct(((nb + 1) * 8, d), F32)],
        scratch_shapes=[pltpu.VMEM((d, tn), BF16)] * 3 + [pltpu.VMEM((8, tn), F32)],
        compiler_params=_params("arbitrary", "arbitrary"), name="conv_mixer",
    )(hm, ht, w_in, w_in, w_in, conv_w, prev0, prev1)


def _ffn_rows(x, wgs, wus, wds):
    g = _dot(x, wgs[...])
    a = (g * jax.nn.sigmoid(g) * _dot(x, wus[...])).astype(BF16)
    return _dot(a, wds[...])


def _cast_weights(wg, wu, wd, wgs, wus, wds):
    wgs[...] = wg[...].astype(BF16)
    wus[...] = wu[...].astype(BF16)
    wds[...] = wd[...].astype(BF16)


def _ffn_dense_body(xm, xt, wg, wu, wd, om, ot, wgs, wus, wds, *, nb):
    c = pl.program_id(0)
    f = pl.program_id(1)

    @pl.when(c < nb)
    def _():
        @pl.when(f == 0)
        def _():
            om[...] = jnp.zeros_like(om)

        _cast_weights(wg, wu, wd, wgs, wus, wds)
        om[...] += _ffn_rows(xm[...], wgs, wus, wds)

    @pl.when(c == nb)
    def _():
        @pl.when(f == 0)
        def _():
            ot[...] = jnp.zeros_like(ot)

        _cast_weights(wg, wu, wd, wgs, wus, wds)
        ot[...] += _ffn_rows(xt[...], wgs, wus, wds)


def _ffn_dense(hm, ht, w_gate, w_up, w_down, layer, tm, tf):
    npr, d = hm.shape
    nb = npr // tm
    ff = w_gate.shape[-1]
    main = pl.BlockSpec((tm, d), lambda c, f: (jnp.minimum(c, nb - 1), 0))
    tail = pl.BlockSpec((TAIL, d), lambda c, f: (0, 0))
    return pl.pallas_call(
        functools.partial(_ffn_dense_body, nb=nb),
        grid=(nb + 1, ff // tf),
        in_specs=[main, tail,
                  pl.BlockSpec((None, d, tf), lambda c, f: (layer, 0, f)),
                  pl.BlockSpec((None, d, tf), lambda c, f: (layer, 0, f)),
                  pl.BlockSpec((None, tf, d), lambda c, f: (layer, f, 0))],
        out_specs=[main, tail],
        out_shape=[jax.ShapeDtypeStruct((npr, d), F32), jax.ShapeDtypeStruct((TAIL, d), F32)],
        scratch_shapes=[pltpu.VMEM((d, tf), BF16), pltpu.VMEM((d, tf), BF16), pltpu.VMEM((tf, d), BF16)],
        compiler_params=_params("arbitrary", "arbitrary"), name="ffn_dense",
    )(hm, ht, w_gate, w_up, w_down)


def _ffn_moe_body(ce, cn, tok, h, wg, wu, wd, o, gbuf, xsb, wgs, wus, wds, sem, *, sub, per_step):
    c = pl.program_id(0)
    f = pl.program_id(1)
    rows, d = o.shape
    pk = d // (2 * LANE)
    nv = cn[c]
    slot = lax.rem(c, 2)

    def issue(chunk, first, count, s):
        def body(i, carry):
            r = first + i
            t = tok[chunk * rows + r]
            pltpu.make_async_copy(h.at[pl.ds(t * pk, pk)], gbuf.at[s, pl.ds(r * pk, pk)], sem.at[s]).start()
            return carry

        lax.fori_loop(0, count, body, 0, unroll=8)

    @pl.when(jnp.logical_and(jnp.logical_and(c == 0, f == 0), nv > 0))
    def _():
        issue(0, 0, rows, 0)

    @pl.when(f == 0)
    def _():
        o[...] = jnp.zeros_like(o)

        @pl.when(nv > 0)
        def _():
            pltpu.make_async_copy(h.at[pl.ds(0, rows * pk)], gbuf.at[slot], sem.at[slot]).wait()
            for j in range(pk):
                w = gbuf[slot, pl.ds(j, rows, stride=pk), :]
                lo = lax.bitcast_convert_type(lax.shift_left(w, jnp.uint32(16)), F32)
                hi = lax.bitcast_convert_type(jnp.bitwise_and(w, jnp.uint32(0xFFFF0000)), F32)
                xsb[:, j * LANE:(j + 1) * LANE] = lo.astype(BF16)
                xsb[:, d // 2 + j * LANE:d // 2 + (j + 1) * LANE] = hi.astype(BF16)

    @pl.when(jnp.logical_and(f < rows // per_step, cn[c + 1] > 0))
    def _():
        issue(c + 1, f * per_step, per_step, 1 - slot)

    @pl.when(nv == rows)
    def _():
        _cast_weights(wg, wu, wd, wgs, wus, wds)
        o[...] += _ffn_rows(xsb[...], wgs, wus, wds)

    @pl.when(jnp.logical_and(nv > 0, nv < rows))
    def _():
        _cast_weights(wg, wu, wd, wgs, wus, wds)

        def step(i, carry):
            r = pl.ds(pl.multiple_of(i * sub, sub), sub)
            o[r, :] += _ffn_rows(xsb[r, :], wgs, wus, wds)
            return carry

        lax.fori_loop(0, (nv + sub - 1) // sub, step, 0)


def _ffn_moe(h, tok_of_slot, chunk_expert, chunk_rows, w_gate, w_up, w_down, layer, d, rows, tf, sub):
    p = tok_of_slot.shape[0]
    n_chunks = p // rows
    ff = w_gate.shape[-1]
    nf = ff // tf
    pk = d // (2 * LANE)
    per_step = next(q for q in range(-(-rows // nf), rows + 1) if rows % q == 0)

    def fsel(c, f, cn):
        return jnp.where(cn[c] > 0, f, nf - 1)

    return pl.pallas_call(
        functools.partial(_ffn_moe_body, sub=sub, per_step=per_step),
        grid_spec=pltpu.PrefetchScalarGridSpec(
            num_scalar_prefetch=3, grid=(n_chunks, nf),
            in_specs=[pl.BlockSpec(memory_space=pl.ANY),
                      pl.BlockSpec((None, None, d, tf),
                                   lambda c, f, ce, cn, tok: (layer, ce[c], 0, fsel(c, f, cn))),
                      pl.BlockSpec((None, None, d, tf),
                                   lambda c, f, ce, cn, tok: (layer, ce[c], 0, fsel(c, f, cn))),
                      pl.BlockSpec((None, None, tf, d),
                                   lambda c, f, ce, cn, tok: (layer, ce[c], fsel(c, f, cn), 0))],
            out_specs=pl.BlockSpec((rows, d), lambda c, f, ce, cn, tok: (c, 0)),
            scratch_shapes=[pltpu.VMEM((2, rows * pk, LANE), jnp.uint32), pltpu.VMEM((rows, d), BF16),
                            pltpu.VMEM((d, tf), BF16), pltpu.VMEM((d, tf), BF16), pltpu.VMEM((tf, d), BF16),
                            pltpu.SemaphoreType.DMA((2,))]),
        out_shape=jax.ShapeDtypeStruct((p, d), F32),
        compiler_params=_params("arbitrary", "arbitrary"), name="ffn_moe",
    )(chunk_expert, jnp.pad(chunk_rows, (0, 1)), tok_of_slot, h, w_gate, w_up, w_down)


def _row_copy(src, src_row, dst, dst_row, sem):
    return pltpu.make_async_copy(src.at[pl.ds(src_row, 1)], dst.at[pl.ds(dst_row, 1)], sem)


def _combine_body(s1, s2, xm, xt, gm, gt, yb, om, ot, abuf, bbuf, sem, *, nb, tb, npr):
    m = pl.program_id(0)

    def run(x_ref, g_ref, o_ref, base, rows):
        def issue(r, carry):
            _row_copy(yb, s1[base + r], abuf, r, sem).start()
            _row_copy(yb, s2[base + r], bbuf, r, sem).start()
            return carry

        lax.fori_loop(0, rows, issue, 0, unroll=8)

        def drain(r, carry):
            _row_copy(yb, 0, abuf, r, sem).wait()
            _row_copy(yb, 0, bbuf, r, sem).wait()
            return carry

        lax.fori_loop(0, rows, drain, 0, unroll=8)
        g = g_ref[...]
        o_ref[...] = x_ref[...] + g[:, 0:1] * abuf[0:rows, :] + g[:, 1:2] * bbuf[0:rows, :]

    @pl.when(m < nb)
    def _():
        run(xm, gm, om, m * tb, tb)

    @pl.when(m == nb)
    def _():
        run(xt, gt, ot, npr, TAIL)


def _combine(xm, xt, gm, gt, yb, slot1, slot2, tb):
    npr, d = xm.shape
    nb = npr // tb
    main = pl.BlockSpec((tb, d), lambda m, s1, s2: (jnp.minimum(m, nb - 1), 0))
    tail = pl.BlockSpec((TAIL, d), lambda m, s1, s2: (0, 0))
    gmain = pl.BlockSpec((tb, LANE), lambda m, s1, s2: (jnp.minimum(m, nb - 1), 0))
    gtail = pl.BlockSpec((TAIL, LANE), lambda m, s1, s2: (0, 0))
    return pl.pallas_call(
        functools.partial(_combine_body, nb=nb, tb=tb, npr=npr),
        grid_spec=pltpu.PrefetchScalarGridSpec(
            num_scalar_prefetch=2, grid=(nb + 1,),
            in_specs=[main, tail, gmain, gtail, pl.BlockSpec(memory_space=pl.ANY)],
            out_specs=[main, tail],
            scratch_shapes=[pltpu.VMEM((tb, d), F32), pltpu.VMEM((tb, d), F32), pltpu.SemaphoreType.DMA]),
        out_shape=[jax.ShapeDtypeStruct((npr, d), F32), jax.ShapeDtypeStruct((TAIL, d), F32)],
        compiler_params=_params("arbitrary"), name="moe_combine",
    )(slot1, slot2, xm, xt, gm, gt, yb)


def _route(route_i, route_i_tail, n_tok, n_exp, rows):
    npr = route_i.shape[0]
    ids = jnp.concatenate([route_i[:, :TOP_K], route_i_tail[:n_tok - npr, :TOP_K]], axis=0)
    flat_e = ids.reshape(-1)
    onehot = (flat_e[:, None] == jnp.arange(n_exp, dtype=jnp.int32)[None, :]).astype(jnp.int32)
    before = jnp.cumsum(onehot, axis=0) - onehot
    rank = jnp.sum(before * onehot, axis=1)
    counts = jnp.sum(onehot, axis=0)
    seg = (counts + rows - 1) // rows * rows
    seg_end = jnp.cumsum(seg)
    seg_start = seg_end - seg
    slot = (seg_start[flat_e] + rank).astype(jnp.int32)
    n_chunks = (n_tok * TOP_K) // rows + n_exp
    tok = jnp.repeat(jnp.arange(n_tok, dtype=jnp.int32), TOP_K)
    tok_of_slot = jnp.zeros((n_chunks * rows,), jnp.int32).at[slot].set(tok)
    chunk_start = jnp.arange(n_chunks, dtype=jnp.int32) * rows
    chunk_e = jnp.minimum(jnp.searchsorted(seg_end, chunk_start, side="right"), n_exp - 1).astype(jnp.int32)
    chunk_rows = jnp.clip(counts[chunk_e] - (chunk_start - seg_start[chunk_e]), 0, rows).astype(jnp.int32)
    last_used = jnp.max(jnp.where(chunk_rows > 0, jnp.arange(n_chunks), 0))
    chunk_e = jnp.where(chunk_rows > 0, chunk_e, chunk_e[last_used]).astype(jnp.int32)
    slots = slot.reshape(n_tok, TOP_K)
    pad = npr + TAIL - n_tok
    slot1 = jnp.pad(slots[:, 0], (0, pad))
    slot2 = jnp.pad(slots[:, 1], (0, pad))
    return tok_of_slot, chunk_e, chunk_rows, slot1, slot2


def _lam(lam_ref, lam0):
    lp = lam_ref[...]
    a = jnp.sum(lp[0:1, :] * lp[1:2, :], axis=-1, keepdims=True)
    b = jnp.sum(lp[2:3, :] * lp[3:4, :], axis=-1, keepdims=True)
    return jnp.exp(a) - jnp.exp(b) + lam0


def _attn_body(q_ref, k_ref, v_ref, sl_ref, lam_ref, sg_ref, o_ref, vt_sc, kp_sc, s_a, s_b, p_a, p_b, al_a, al_b,
               m_sc, l_sc, acc_sc, *, blk, lam0):
    i = pl.program_id(2)
    rows = 64

    @pl.when(i == 0)
    def _():
        lane = lax.broadcasted_iota(jnp.int32, (blk, HEAD_DIM), 1)
        for c in range(vt_sc.shape[0]):
            vt_sc[c] = v_ref[c * blk:(c + 1) * blk, :].T.astype(BF16)
            pos = c * blk + lax.broadcasted_iota(jnp.int32, (blk, HEAD_DIM), 0)
            hi = lax.shift_right_logical(pos, 6)
            lo = jnp.bitwise_and(pos, 63)
            digits = jnp.where(lane < 4, jnp.where(jnp.bitwise_and(lane, 1) == 0, hi, lo), 0)
            kp_sc[c] = digits.astype(F32).astype(BF16)

    pos_cols = jnp.broadcast_to(sl_ref[...], (blk, HEAD_DIM)).astype(BF16)
    q = [jnp.concatenate([q_ref[:, c * HEAD_DIM:(c + 1) * HEAD_DIM], pos_cols], axis=1) for c in range(2)]
    m_sc[...] = jnp.full_like(m_sc, NEG_INF)
    l_sc[...] = jnp.zeros_like(l_sc)
    acc_sc[...] = jnp.zeros_like(acc_sc)
    p_b[...] = jnp.zeros_like(p_b)
    al_b[...] = jnp.ones_like(al_b)

    def scores(j, s_ref):
        k = k_ref[pl.ds(pl.multiple_of(j * blk, blk), blk), :]
        kp = kp_sc[j]
        for c in range(2):
            kc = jnp.concatenate([k[:, c * HEAD_DIM:(c + 1) * HEAD_DIM], kp], axis=1)
            s_ref[c] = _dot_nt(kc, q[c])

    def softmax(j, s_ref, p_ref, al_ref, masked):
        def piece(c, r):
            s = s_ref[c, r * rows:(r + 1) * rows, :]
            if masked:
                ahead = (lax.broadcasted_iota(jnp.int32, s.shape, 0) - lax.broadcasted_iota(jnp.int32, s.shape, 1)
                         + (r * rows + (j - i) * blk))
                s = jnp.where(ahead <= 0, s, NEG_INF)
            return s

        for c in range(2):
            top = piece(c, 0).reshape(rows // 8, 8, blk).max(axis=0)
            for r in range(1, blk // rows):
                top = jnp.maximum(top, piece(c, r).reshape(rows // 8, 8, blk).max(axis=0))
            m_old = m_sc[c]
            m_new = jnp.maximum(m_old, jnp.max(top, axis=0, keepdims=True))
            alpha = jnp.exp2(m_old - m_new)
            total = jnp.zeros((8, blk), F32)
            for r in range(blk // rows):
                p = jnp.exp2(piece(c, r) - m_new)
                total = total + p.reshape(rows // 8, 8, blk).sum(axis=0)
                p_ref[c, r * rows:(r + 1) * rows, :] = p.astype(BF16)
            l_sc[c] = alpha * l_sc[c] + jnp.sum(total, axis=0, keepdims=True)
            m_sc[c] = m_new
            al_ref[c] = alpha

    def values(j, p_ref, al_ref):
        vt = vt_sc[j]
        for c in range(2):
            acc_sc[c] = al_ref[c] * acc_sc[c] + _dot(vt, p_ref[c])

    n_pairs = (i + 2) // 2
    scores(0, s_a)

    def pair(t, carry):
        values(jnp.maximum(2 * t - 1, 0), p_b, al_b)
        softmax(2 * t, s_a, p_a, al_a, False)
        scores(2 * t + 1, s_b)
        values(2 * t, p_a, al_a)
        softmax(2 * t + 1, s_b, p_b, al_b, False)
        scores(2 * t + 2, s_a)
        return carry

    lax.fori_loop(0, n_pairs - 1, pair, 0)

    j0 = 2 * n_pairs - 2
    values(jnp.maximum(j0 - 1, 0), p_b, al_b)
    softmax(j0, s_a, p_a, al_a, True)

    @pl.when(j0 < i)
    def _():
        scores(i, s_b)
        values(j0, p_a, al_a)
        softmax(i, s_b, p_b, al_b, True)
        values(i, p_b, al_b)

    @pl.when(j0 == i)
    def _():
        values(j0, p_a, al_a)

    o = acc_sc[0] * (1.0 / l_sc[0]) - _lam(lam_ref, lam0) * (acc_sc[1] * (1.0 / l_sc[1]))
    ms = jnp.mean(o * o, axis=0, keepdims=True)
    o = (o * lax.rsqrt(ms + EPS)).T
    o_ref[...] = (o * sg_ref[...] * (1.0 - lam0)).astype(o_ref.dtype)


def _attn_prompt(q, k, v, slopes, lam_p, sub_gain, n_batch, seq_len, blk, lam0):
    npr, d = q.shape
    assert seq_len <= 64 * 64 and blk % 64 == 0
    s_hi = slopes.astype(BF16).astype(F32)
    s_lo = (slopes - s_hi).astype(BF16).astype(F32)
    pos_mult = jnp.pad(jnp.stack([64.0 * s_hi, s_hi, 64.0 * s_lo, s_lo], axis=1), ((0, 0), (0, LANE - 4)))
    slopes = pos_mult[:, None, :]
    n_heads = d // V_DIM
    nq = seq_len // blk
    kv_spec = pl.BlockSpec((seq_len, V_DIM), lambda b, h, i: (b, h))
    return pl.pallas_call(
        functools.partial(_attn_body, blk=blk, lam0=lam0),
        grid=(n_batch, n_heads, nq),
        in_specs=[pl.BlockSpec((blk, V_DIM), lambda b, h, i: (b * nq + i, h)), kv_spec, kv_spec,
                  pl.BlockSpec((None, 1, LANE), lambda b, h, i: (h, 0, 0)),
                  pl.BlockSpec((4, HEAD_DIM), lambda b, h, i: (0, 0)),
                  pl.BlockSpec((1, V_DIM), lambda b, h, i: (0, 0))],
        out_specs=pl.BlockSpec((blk, V_DIM), lambda b, h, i: (b * nq + i, h)),
        out_shape=jax.ShapeDtypeStruct((npr, d), BF16),
        scratch_shapes=[pltpu.VMEM((nq, V_DIM, blk), BF16), pltpu.VMEM((nq, blk, HEAD_DIM), BF16),
                        pltpu.VMEM((2, blk, blk), F32), pltpu.VMEM((2, blk, blk), F32),
                        pltpu.VMEM((2, blk, blk), BF16), pltpu.VMEM((2, blk, blk), BF16),
                        pltpu.VMEM((2, 1, blk), F32), pltpu.VMEM((2, 1, blk), F32),
                        pltpu.VMEM((2, 1, blk), F32), pltpu.VMEM((2, 1, blk), F32),
                        pltpu.VMEM((2, V_DIM, blk), F32)],
        compiler_params=_params("arbitrary", "arbitrary", "arbitrary"), name="attn_prompt",
    )(q, k, v, slopes, lam_p, sub_gain)


def _attn_sample_body(pt, q_ref, kn_ref, vn_ref, sl_ref, lam_ref, sg_ref, *refs, pps, page, q_pos, lam0):
    k_refs = refs[:pps]
    v_refs = refs[pps:2 * pps]
    o_ref, m_sc, l_sc, acc_sc = refs[2 * pps:]
    p = pl.program_id(1)
    n_rows = q_ref.shape[0]
    n_heads = n_rows // 2
    cols = page * n_heads
    row = lax.broadcasted_iota(jnp.int32, (n_rows, cols), 0)
    col = lax.broadcasted_iota(jnp.int32, (n_rows, cols), 1)
    own = jnp.bitwise_and(col, n_heads - 1) == jnp.bitwise_and(row, n_heads - 1)
    key = lax.shift_right_logical(lax.broadcasted_iota(jnp.int32, (1, cols), 1), n_heads.bit_length() - 1)
    slope = sl_ref[:, 0:1]

    @pl.when(p == 0)
    def _():
        m_sc[...] = jnp.full_like(m_sc, NEG_INF)
        l_sc[...] = jnp.zeros_like(l_sc)
        acc_sc[...] = jnp.zeros_like(acc_sc)

    def update(s, pv_fn):
        m_old = m_sc[...]
        m_new = jnp.maximum(m_old, jnp.max(s, axis=-1, keepdims=True))
        alpha = jnp.exp2(m_old - m_new)
        pr = jnp.exp2(s - m_new)
        l_sc[...] = alpha * l_sc[...] + jnp.sum(pr, axis=-1, keepdims=True)
        acc_sc[...] = alpha * acc_sc[...] + pv_fn(pr)
        m_sc[...] = m_new

    qb = q_ref[...].astype(BF16)
    scores = []
    for i in range(pps):
        s_c = [_dot_nt(qb, k_refs[i][pl.ds(c, cols, stride=2), :].astype(BF16)) for c in range(2)]
        s = jnp.concatenate([s_c[0][:n_heads], s_c[1][n_heads:]], axis=0)
        dist = (q_pos - ((p * pps + i) * page + key)).astype(F32)
        scores.append(jnp.where(own, s - slope * dist, NEG_INF))

    def weighted_values(pr):
        pr = pr.astype(BF16)
        return sum(_dot(pr[:, i * cols:(i + 1) * cols], v_refs[i][...].astype(BF16)) for i in range(pps))

    update(jnp.concatenate(scores, axis=1), weighted_values)

    @pl.when(p == pl.num_programs(1) - 1)
    def _():
        s_new = jnp.sum(q_ref[...] * kn_ref[...], axis=-1, keepdims=True)
        vn = vn_ref[...]
        update(s_new, lambda pr: pr * jnp.concatenate([vn, vn], axis=0))
        o_hc = acc_sc[...] * (1.0 / l_sc[...])
        o = o_hc[:n_heads] - _lam(lam_ref, lam0) * o_hc[n_heads:]
        o_ref[...] = _rms(o, sg_ref[...]) * (1.0 - lam0)


def _attn_sample(q, k_new, v_new, cache_k, cache_v, page_table, slope_rows, lam_p, sub_gain, pps, lam0):
    n_seq, n_rows, _ = q.shape
    n_heads = n_rows // 2
    assert n_heads & (n_heads - 1) == 0
    n_pages = page_table.shape[1]
    n_pool, page = cache_k.shape[:2]
    ck = cache_k.reshape(n_pool * page * n_rows, HEAD_DIM)
    cv = cache_v.reshape(n_pool * page * n_heads, V_DIM)
    qspec = pl.BlockSpec((None, n_rows, HEAD_DIM), lambda b, p, pt: (b, 0, 0))
    vspec = pl.BlockSpec((None, n_heads, V_DIM), lambda b, p, pt: (b, 0, 0))

    def page_idx(b, p, pt, i):
        return pt[b * n_pages + p * pps + i]

    k_specs = [pl.BlockSpec((page * n_rows, HEAD_DIM), lambda b, p, pt, i=i: (page_idx(b, p, pt, i), 0))
               for i in range(pps)]
    v_specs = [pl.BlockSpec((page * n_heads, V_DIM), lambda b, p, pt, i=i: (page_idx(b, p, pt, i), 0))
               for i in range(pps)]
    return pl.pallas_call(
        functools.partial(_attn_sample_body, pps=pps, page=page, q_pos=n_pages * page, lam0=lam0),
        grid_spec=pltpu.PrefetchScalarGridSpec(
            num_scalar_prefetch=1, grid=(n_seq, n_pages // pps),
            in_specs=[qspec, qspec, vspec,
                      pl.BlockSpec((n_rows, LANE), lambda b, p, pt: (0, 0)),
                      pl.BlockSpec((4, HEAD_DIM), lambda b, p, pt: (0, 0)),
                      pl.BlockSpec((1, V_DIM), lambda b, p, pt: (0, 0))] + k_specs + v_specs,
            out_specs=vspec,
            scratch_shapes=[pltpu.VMEM((n_rows, 1), F32), pltpu.VMEM((n_rows, 1), F32),
                            pltpu.VMEM((n_rows, V_DIM), F32)]),
        out_shape=jax.ShapeDtypeStruct((n_seq, n_heads, V_DIM), F32),
        compiler_params=_params("arbitrary", "arbitrary"), name="attn_sample",
    )(page_table.reshape(-1), q, k_new, v_new, slope_rows, lam_p, sub_gain, *([ck] * pps), *([cv] * pps))


def _tiles(seq_len, d_model, d_ff, d_ff_e):
    tm = min(1024, seq_len)
    return dict(
        tm=tm,
        tr=min(256, tm),
        tn=min(512, d_model),
        tn_conv=min(256, d_model),
        tf=256,
        sub=min(512, tm),
        moe_rows=tm,
        gather_rows=min(256, tm),
        attn_blk=min(512, seq_len),
        pages_per_step=4,
    )


def kernel(x_prompt, x_sample, state_conv, cache_k, cache_v, page_table, mix_norm, ffn_norm, a_w_in, a_conv,
           a_w_out, kv_norm, w_kv, k_norm, b_w_q, b_q_norm, b_lam, b_subln, b_w_o, ffn_w_gate, ffn_w_up,
           ffn_w_down, moe_router, moe_w_gate, moe_w_up, moe_w_down):
    n_batch, seq_len, d = x_prompt.shape
    n_seq = x_sample.shape[0]
    assert x_sample.shape[1] == 1 and n_seq <= TAIL
    depth = mix_norm.shape[0]
    n_a = a_w_in.shape[0]
    n_heads = d // V_DIM
    n_exp = moe_router.shape[-1]
    k_width = n_heads * 2 * HEAD_DIM
    npr = n_batch * seq_len
    n_tok = npr + n_seq
    t = _tiles(seq_len, d, ffn_w_gate.shape[-1], moe_w_gate.shape[-1])
    tm, tn, tr = t["tm"], t["tn"], t["tr"]
    nb = npr // tm
    bps = seq_len // tm

    xm = x_prompt.reshape(npr, d)
    xt = jnp.pad(x_sample.reshape(n_seq, d), ((0, TAIL - n_seq), (0, 0)))

    slopes = jnp.exp2(-8.0 * jnp.arange(1, n_heads + 1, dtype=F32) / n_heads) * LOG2E
    slope_rows = jnp.broadcast_to(jnp.tile(slopes, 2)[:, None], (2 * n_heads, LANE))
    k_gain = jnp.tile(k_norm.reshape(-1), n_heads)

    def comp_major(a):
        return a[:n_seq].reshape(n_seq, n_heads, 2, HEAD_DIM).transpose(0, 2, 1, 3).reshape(n_seq, 2 * n_heads, HEAD_DIM)

    def pad_tail(a):
        return jnp.pad(a, ((0, TAIL - n_seq), (0, 0)))

    conv_prompt, conv_sample = [], []
    pending = None
    k_f32 = v_f32 = k_bf = v_bf = None
    for l in range(depth):
        if pending is None:
            hm, ht = _norm(xm, xt, mix_norm[l], tr)
        else:
            hm, ht, xm, xt = _norm(xm, xt, mix_norm[l], tr, add=pending)
            pending = None
        if l < n_a:
            prev0 = pad_tail(state_conv[l, :, 0, :])
            prev1 = pad_tail(state_conv[l, :, 1, :])
            gm, gt, ut = _conv_mixer(hm, ht, a_w_in, a_conv, l, prev0, prev1, tm, t["tn_conv"], seq_len)
            ut = ut.reshape(nb + 1, 8, d)
            conv_prompt.append(ut[bps - 1:nb:bps, 6:8, :])
            conv_sample.append(jnp.stack([state_conv[l, :, 1, :], ut[nb, :n_seq, :]], axis=1))
            xm, xt = _matmul(gm, gt, a_w_out, l, 0, d, tm, tn, [(F32, F32)], res=(xm, xt))
        else:
            j = l - n_a
            if k_f32 is None:
                nm, nt = _norm(xm, xt, kv_norm, tr)
                k_f32, kt_f32, k_bf, _ = _matmul(nm, nt, w_kv, 0, 0, k_width, tm, tn,
                                                 [(F32, F32), (BF16, BF16)], gain=k_gain)
                v_f32, vt_f32 = _matmul(nm, nt, w_kv, 0, k_width, d, tm, tn, [(F32, F32)])
            lam0 = 0.8 - 0.6 * math.exp(-0.3 * l)
            q_gain = jnp.tile(b_q_norm[j].reshape(-1), n_heads) * (HEAD_DIM ** -0.5 * LOG2E)
            qm, qt = _matmul(hm, ht, b_w_q, j, 0, k_width, tm, tn, [(BF16, F32)], gain=q_gain)
            sub_gain = b_subln[j].reshape(1, V_DIM)
            om = _attn_prompt(qm, k_bf, v_f32, slopes, b_lam[j], sub_gain, n_batch, seq_len,
                              t["attn_blk"], lam0)
            os_ = _attn_sample(comp_major(qt), comp_major(kt_f32), vt_f32[:n_seq].reshape(n_seq, n_heads, V_DIM),
                               cache_k, cache_v, page_table, slope_rows, b_lam[j], sub_gain,
                               t["pages_per_step"], lam0)
            ot = pad_tail(os_.reshape(n_seq, d)).astype(BF16)
            xm, xt = _matmul(om, ot, b_w_o, j, 0, d, tm, tn, [(F32, F32)], res=(xm, xt))
        i = l // 2
        if l % 2 == 0:
            h2m, h2t = _norm(xm, xt, ffn_norm[l], tr)
            pending = _ffn_dense(h2m, h2t, ffn_w_gate, ffn_w_up, ffn_w_down, i, tm, t["tf"])
        else:
            h2, rim, rit, rgm, rgt = _norm_router(xm, xt, ffn_norm[l], moe_router[i], tr)
            tok_of_slot, chunk_e, chunk_rows, slot1, slot2 = _route(rim, rit, n_tok, n_exp, t["moe_rows"])
            yb = _ffn_moe(h2, tok_of_slot, chunk_e, chunk_rows, moe_w_gate, moe_w_up, moe_w_down, i,
                          d, t["moe_rows"], t["tf"], t["sub"])
            valid = (jnp.arange(TAIL) < n_seq)[:, None]
            xm, xt = _combine(xm, xt, rgm, jnp.where(valid, rgt, 0.0), yb, slot1, slot2, t["gather_rows"])
    if pending is not None:
        xm, xt = xm + pending[0], xt + pending[1]

    y_prompt = xm.reshape(n_batch, seq_len, d)
    y_sample = xt[:n_seq].reshape(n_seq, 1, d)
    return (y_prompt, y_sample, jnp.stack(conv_prompt), jnp.stack(conv_sample),
            k_f32.reshape(n_batch, seq_len, n_heads, 2, HEAD_DIM),
            v_f32.reshape(n_batch, seq_len, n_heads, V_DIM),
            kt_f32[:n_seq].reshape(n_seq, 1, n_heads, 2, HEAD_DIM),
            vt_f32[:n_seq].reshape(n_seq, 1, n_heads, V_DIM))
```

```python
import functools
import math

import jax
import jax.numpy as jnp
from jax import lax
from jax.experimental import pallas as pl
from jax.experimental.pallas import tpu as pltpu

F32 = jnp.float32
BF16 = jnp.bfloat16
EPS = 1e-6
HEAD_DIM = 128
V_DIM = 2 * HEAD_DIM
TOP_K = 2
CONV_W = 3
LANE = 128
TAIL = 16
V7X_VMEM_BYTES = 64 * 1024 * 1024
VMEM_LIMIT = V7X_VMEM_BYTES - 8 * 1024 * 1024
NEG_INF = float("-inf")
LOG2E = 1.0 / math.log(2.0)


def _params(*sem):
    return pltpu.CompilerParams(dimension_semantics=sem, vmem_limit_bytes=VMEM_LIMIT)


def _dot(a, b):
    return jnp.dot(a, b, preferred_element_type=F32)


def _dot_nt(a, b):
    return lax.dot_general(a, b, (((1,), (1,)), ((), ())), preferred_element_type=F32)


def _rms(x, g):
    ms = jnp.mean(x * x, axis=-1, keepdims=True)
    return x * lax.rsqrt(ms + EPS) * g


def _group_rms(x, gsize):
    outs = []
    for g in range(x.shape[-1] // gsize):
        blk = x[:, g * gsize:(g + 1) * gsize]
        ms = jnp.mean(blk * blk, axis=-1, keepdims=True)
        outs.append(blk * lax.rsqrt(ms + EPS))
    return outs[0] if len(outs) == 1 else jnp.concatenate(outs, axis=-1)


def _norm_body(*refs, nb, has_add):
    if has_add:
        xm, xt, ym, yt, g, om, ot, sm, st = refs
    else:
        xm, xt, g, om, ot = refs
        ym = yt = sm = st = None
    m = pl.program_id(0)

    def run(x_ref, y_ref, o_ref, s_ref):
        x = x_ref[...]
        if has_add:
            x = x + y_ref[...]
            s_ref[...] = x
        o_ref[...] = _rms(x, g[...]).astype(o_ref.dtype)

    @pl.when(m < nb)
    def _():
        run(xm, ym, om, sm)

    @pl.when(m == nb)
    def _():
        run(xt, yt, ot, st)


def _norm(xm, xt, gain, tm, add=None, out_dtype=BF16):
    npr, d = xm.shape
    nb = npr // tm
    main = pl.BlockSpec((tm, d), lambda m: (jnp.minimum(m, nb - 1), 0))
    tail = pl.BlockSpec((TAIL, d), lambda m: (0, 0))
    gspec = pl.BlockSpec((1, d), lambda m: (0, 0))
    ins = [xm, xt]
    in_specs = [main, tail]
    out_shape = [jax.ShapeDtypeStruct((npr, d), out_dtype), jax.ShapeDtypeStruct((TAIL, d), out_dtype)]
    out_specs = [main, tail]
    if add is not None:
        ins += list(add)
        in_specs += [main, tail]
        out_shape += [jax.ShapeDtypeStruct((npr, d), F32), jax.ShapeDtypeStruct((TAIL, d), F32)]
        out_specs += [main, tail]
    ins.append(gain.reshape(1, d))
    in_specs.append(gspec)
    return pl.pallas_call(
        functools.partial(_norm_body, nb=nb, has_add=add is not None),
        grid=(nb + 1,), in_specs=in_specs, out_specs=out_specs, out_shape=out_shape,
        compiler_params=_params("arbitrary"), name="rmsnorm",
    )(*ins)


def _router_body(xm, xt, g, wr, h_all, im, it, gm, gt, *, nb, n_exp):
    m = pl.program_id(0)

    def run(x_ref, i_ref, g_ref):
        h = _rms(x_ref[...], g[...])
        rows, d = h.shape
        pk = d // (2 * LANE)
        bits = lax.bitcast_convert_type(h.astype(BF16).astype(F32), jnp.uint32)
        for j in range(pk):
            lo = lax.shift_right_logical(bits[:, j * LANE:(j + 1) * LANE], jnp.uint32(16))
            hi = bits[:, d // 2 + j * LANE:d // 2 + (j + 1) * LANE]
            h_all[pl.ds(j, rows, stride=pk), :] = jnp.bitwise_or(hi, lo)
        if rows * pk < h_all.shape[0]:
            h_all[rows * pk:, :] = jnp.zeros((h_all.shape[0] - rows * pk, LANE), jnp.uint32)
        logits = jnp.dot(h, wr[...], preferred_element_type=F32, precision=lax.Precision.HIGHEST)
        lane = lax.broadcasted_iota(jnp.int32, logits.shape, 1)
        logits = jnp.where(lane < n_exp, logits, NEG_INF)
        v1 = jnp.max(logits, axis=-1, keepdims=True)
        i1 = jnp.min(jnp.where(logits == v1, lane, LANE), axis=-1, keepdims=True)
        rest = jnp.where(lane == i1, NEG_INF, logits)
        v2 = jnp.max(rest, axis=-1, keepdims=True)
        i2 = jnp.min(jnp.where(rest == v2, lane, LANE), axis=-1, keepdims=True)
        e = jnp.exp(v2 - v1)
        g1 = 1.0 / (1.0 + e)
        g2 = e / (1.0 + e)
        i_ref[...] = jnp.where(lane == 0, i1, jnp.where(lane == 1, i2, 0))
        g_ref[...] = jnp.where(lane == 0, g1, jnp.where(lane == 1, g2, 0.0))

    @pl.when(m < nb)
    def _():
        run(xm, im, gm)

    @pl.when(m == nb)
    def _():
        run(xt, it, gt)


def _norm_router(xm, xt, gain, w_router, tm):
    npr, d = xm.shape
    n_exp = w_router.shape[-1]
    nb = npr // tm
    wr = jnp.pad(w_router, ((0, 0), (0, LANE - n_exp)))
    main = pl.BlockSpec((tm, d), lambda m: (jnp.minimum(m, nb - 1), 0))
    tail = pl.BlockSpec((TAIL, d), lambda m: (0, 0))
    rmain = pl.BlockSpec((tm, LANE), lambda m: (jnp.minimum(m, nb - 1), 0))
    rtail = pl.BlockSpec((TAIL, LANE), lambda m: (0, 0))
    return pl.pallas_call(
        functools.partial(_router_body, nb=nb, n_exp=n_exp),
        grid=(nb + 1,),
        in_specs=[main, tail, pl.BlockSpec((1, d), lambda m: (0, 0)),
                  pl.BlockSpec((d, LANE), lambda m: (0, 0))],
        out_specs=[pl.BlockSpec((tm * (d // (2 * LANE)), LANE), lambda m: (m, 0)), rmain, rtail, rmain, rtail],
        out_shape=[jax.ShapeDtypeStruct(((npr + tm) * (d // (2 * LANE)), LANE), jnp.uint32),
                   jax.ShapeDtypeStruct((npr, LANE), jnp.int32), jax.ShapeDtypeStruct((TAIL, LANE), jnp.int32),
                   jax.ShapeDtypeStruct((npr, LANE), F32), jax.ShapeDtypeStruct((TAIL, LANE), F32)],
        compiler_params=_params("arbitrary"), name="rmsnorm_router",
    )(xm, xt, gain.reshape(1, d), wr)


def _mm_body(*refs, nb, has_res, has_gain, n_out):
    refs = list(refs)
    xm, xt, w = refs[:3]
    pos = 3
    rm = rt = gain = None
    if has_res:
        rm, rt = refs[pos:pos + 2]
        pos += 2
    if has_gain:
        gain = refs[pos]
        pos += 1
    outs = refs[pos:pos + 2 * n_out]
    wb = refs[pos + 2 * n_out]
    m = pl.program_id(1)

    @pl.when(m == 0)
    def _():
        wb[...] = w[...].astype(BF16)

    def run(x_ref, r_ref, o_refs):
        acc = _dot(x_ref[...], wb[...])
        if has_gain:
            acc = _group_rms(acc, HEAD_DIM) * gain[...]
        if has_res:
            acc = acc + r_ref[...]
        for o in o_refs:
            o[...] = acc.astype(o.dtype)

    @pl.when(m < nb)
    def _():
        run(xm, rm, outs[0::2])

    @pl.when(m == nb)
    def _():
        run(xt, rt, outs[1::2])


def _matmul(xm, xt, w, layer, col0, n_cols, tm, tn, out_dtypes, res=None, gain=None):
    npr, k = xm.shape
    nb = npr // tm
    cb = col0 // tn
    xmain = pl.BlockSpec((tm, k), lambda n, m: (jnp.minimum(m, nb - 1), 0))
    xtail = pl.BlockSpec((TAIL, k), lambda n, m: (0, 0))
    omain = pl.BlockSpec((tm, tn), lambda n, m: (jnp.minimum(m, nb - 1), n))
    otail = pl.BlockSpec((TAIL, tn), lambda n, m: (0, n))
    if w.ndim == 3:
        wspec = pl.BlockSpec((None, k, tn), lambda n, m: (layer, 0, cb + n))
    else:
        wspec = pl.BlockSpec((k, tn), lambda n, m: (0, cb + n))
    ins, in_specs = [xm, xt, w], [xmain, xtail, wspec]
    if res is not None:
        ins += list(res)
        in_specs += [omain, otail]
    if gain is not None:
        ins.append(gain.reshape(1, n_cols))
        in_specs.append(pl.BlockSpec((1, tn), lambda n, m: (0, n)))
    out_shape, out_specs = [], []
    for dm, dt in out_dtypes:
        out_shape += [jax.ShapeDtypeStruct((npr, n_cols), dm), jax.ShapeDtypeStruct((TAIL, n_cols), dt)]
        out_specs += [omain, otail]
    return pl.pallas_call(
        functools.partial(_mm_body, nb=nb, has_res=res is not None, has_gain=gain is not None,
                          n_out=len(out_dtypes)),
        grid=(n_cols // tn, nb + 1), in_specs=in_specs, out_specs=out_specs, out_shape=out_shape,
        scratch_shapes=[pltpu.VMEM((k, tn), BF16)],
        compiler_params=_params("arbitrary", "arbitrary"), name="matmul",
    )(*ins)


def _conv_body(xm, xt, w_b, w_c, w_v, cw, p0, p1, gm, gt, ut, wbs, wcs, wvs, carry, *, nb, bps):
    m = pl.program_id(1)

    @pl.when(m == 0)
    def _():
        wbs[...] = w_b[...].astype(BF16)
        wcs[...] = w_c[...].astype(BF16)
        wvs[...] = w_v[...].astype(BF16)

    w0, w1, w2 = cw[0:1, :], cw[1:2, :], cw[2:3, :]

    @pl.when(m < nb)
    def _():
        x = xm[...]
        u = _dot(x, wcs[...]) * _dot(x, wvs[...])
        b = _dot(x, wbs[...])

        @pl.when(m % bps == 0)
        def _():
            carry[...] = jnp.zeros_like(carry)

        prev1 = carry[7:8, :]
        prev2 = carry[6:7, :]
        row = lax.broadcasted_iota(jnp.int32, u.shape, 0)
        u1 = jnp.where(row == 0, prev1, pltpu.roll(u, 1, 0))
        u2 = jnp.where(row == 0, prev2, jnp.where(row == 1, prev1, pltpu.roll(u, 2, 0)))
        gm[...] = (b * (w0 * u2 + w1 * u1 + w2 * u)).astype(gm.dtype)
        tail_rows = u[u.shape[0] - 8:, :]
        carry[...] = tail_rows
        ut[...] = tail_rows

    @pl.when(m == nb)
    def _():
        x = xt[...]
        u = _dot(x, wcs[...]) * _dot(x, wvs[...])
        b = _dot(x, wbs[...])
        gt[...] = (b * (w0 * p0[...] + w1 * p1[...] + w2 * u)).astype(gt.dtype)
        ut[...] = u[0:8, :]


def _conv_mixer(hm, ht, w_in, conv_w, layer, prev0, prev1, tm, tn, seq_len):
    npr, d = hm.shape
    nb = npr // tm
    nn = d // tn
    xmain = pl.BlockSpec((tm, d), lambda n, m: (jnp.minimum(m, nb - 1), 0))
    xtail = pl.BlockSpec((TAIL, d), lambda n, m: (0, 0))
    omain = pl.BlockSpec((tm, tn), lambda n, m: (jnp.minimum(m, nb - 1), n))
    otail = pl.BlockSpec((TAIL, tn), lambda n, m: (0, n))

    def wspec(part):
        return pl.BlockSpec((None, d, tn), lambda n, m: (layer, 0, part * nn + n))

    return pl.pallas_call(
        functools.partial(_conv_body, nb=nb, bps=seq_len // tm),
        grid=(nn, nb + 1),
        in_specs=[xmain, xtail, wspec(0), wspec(1), wspec(2),
                  pl.BlockSpec((None, CONV_W, tn), lambda n, m: (layer, 0, n)), otail, otail],
        out_specs=[omain, otail, pl.BlockSpec((8, tn), lambda n, m: (m, n))],
        out_shape=[jax.ShapeDtypeStruct((npr, d), BF16), jax.ShapeDtypeStruct((TAIL, d), BF16),
                   jax.ShapeDtypeStruct(((nb + 1) * 8, d), F32)],
        scratch_shapes=[pltpu.VMEM((d, tn), BF16)] * 3 + [pltpu.VMEM((8, tn), F32)],
        compiler_params=_params("arbitrary", "arbitrary"), name="conv_mixer",
    )(hm, ht, w_in, w_in, w_in, conv_w, prev0, prev1)


def _ffn_rows(x, wgs, wus, wds):
    g = _dot(x, wgs[...])
    a = (g * jax.nn.sigmoid(g) * _dot(x, wus[...])).astype(BF16)
    return _dot(a, wds[...])


def _cast_weights(wg, wu, wd, wgs, wus, wds):
    wgs[...] = wg[...].astype(BF16)
    wus[...] = wu[...].astype(BF16)
    wds[...] = wd[...].astype(BF16)


def _ffn_dense_body(xm, xt, wg, wu, wd, om, ot, wgs, wus, wds, *, nb):
    c = pl.program_id(0)
    f = pl.program_id(1)

    @pl.when(c < nb)
    def _():
        @pl.when(f == 0)
        def _():
            om[...] = jnp.zeros_like(om)

        _cast_weights(wg, wu, wd, wgs, wus, wds)
        om[...] += _ffn_rows(xm[...], wgs, wus, wds)

    @pl.when(c == nb)
    def _():
        @pl.when(f == 0)
        def _():
            ot[...] = jnp.zeros_like(ot)

        _cast_weights(wg, wu, wd, wgs, wus, wds)
        ot[...] += _ffn_rows(xt[...], wgs, wus, wds)


def _ffn_dense(hm, ht, w_gate, w_up, w_down, layer, tm, tf):
    npr, d = hm.shape
    nb = npr // tm
    ff = w_gate.shape[-1]
    main = pl.BlockSpec((tm, d), lambda c, f: (jnp.minimum(c, nb - 1), 0))
    tail = pl.BlockSpec((TAIL, d), lambda c, f: (0, 0))
    return pl.pallas_call(
        functools.partial(_ffn_dense_body, nb=nb),
        grid=(nb + 1, ff // tf),
        in_specs=[main, tail,
                  pl.BlockSpec((None, d, tf), lambda c, f: (layer, 0, f)),
                  pl.BlockSpec((None, d, tf), lambda c, f: (layer, 0, f)),
                  pl.BlockSpec((None, tf, d), lambda c, f: (layer, f, 0))],
        out_specs=[main, tail],
        out_shape=[jax.ShapeDtypeStruct((npr, d), F32), jax.ShapeDtypeStruct((TAIL, d), F32)],
        scratch_shapes=[pltpu.VMEM((d, tf), BF16), pltpu.VMEM((d, tf), BF16), pltpu.VMEM((tf, d), BF16)],
        compiler_params=_params("arbitrary", "arbitrary"), name="ffn_dense",
    )(hm, ht, w_gate, w_up, w_down)


def _ffn_moe_body(ce, cn, tok, h, wg, wu, wd, o, gbuf, xsb, wgs, wus, wds, sem, *, sub, per_step):
    c = pl.program_id(0)
    f = pl.program_id(1)
    rows, d = o.shape
    pk = d // (2 * LANE)
    nv = cn[c]
    slot = lax.rem(c, 2)

    def issue(chunk, first, count, s):
        def body(i, carry):
            r = first + i
            t = tok[chunk * rows + r]
            pltpu.make_async_copy(h.at[pl.ds(t * pk, pk)], gbuf.at[s, pl.ds(r * pk, pk)], sem.at[s]).start()
            return carry

        lax.fori_loop(0, count, body, 0, unroll=8)

    @pl.when(jnp.logical_and(jnp.logical_and(c == 0, f == 0), nv > 0))
    def _():
        issue(0, 0, rows, 0)

    @pl.when(f == 0)
    def _():
        o[...] = jnp.zeros_like(o)

        @pl.when(nv > 0)
        def _():
            pltpu.make_async_copy(h.at[pl.ds(0, rows * pk)], gbuf.at[slot], sem.at[slot]).wait()
            for j in range(pk):
                w = gbuf[slot, pl.ds(j, rows, stride=pk), :]
                lo = lax.bitcast_convert_type(lax.shift_left(w, jnp.uint32(16)), F32)
                hi = lax.bitcast_convert_type(jnp.bitwise_and(w, jnp.uint32(0xFFFF0000)), F32)
                xsb[:, j * LANE:(j + 1) * LANE] = lo.astype(BF16)
                xsb[:, d // 2 + j * LANE:d // 2 + (j + 1) * LANE] = hi.astype(BF16)

    @pl.when(jnp.logical_and(f < rows // per_step, cn[c + 1] > 0))
    def _():
        issue(c + 1, f * per_step, per_step, 1 - slot)

    @pl.when(nv == rows)
    def _():
        _cast_weights(wg, wu, wd, wgs, wus, wds)
        o[...] += _ffn_rows(xsb[...], wgs, wus, wds)

    @pl.when(jnp.logical_and(nv > 0, nv < rows))
    def _():
        _cast_weights(wg, wu, wd, wgs, wus, wds)

        def step(i, carry):
            r = pl.ds(pl.multiple_of(i * sub, sub), sub)
            o[r, :] += _ffn_rows(xsb[r, :], wgs, wus, wds)
            return carry

        lax.fori_loop(0, (nv + sub - 1) // sub, step, 0)


def _ffn_moe(h, tok_of_slot, chunk_expert, chunk_rows, w_gate, w_up, w_down, layer, d, rows, tf, sub):
    p = tok_of_slot.shape[0]
    n_chunks = p // rows
    ff = w_gate.shape[-1]
    nf = ff // tf
    pk = d // (2 * LANE)
    per_step = next(q for q in range(-(-rows // nf), rows + 1) if rows % q == 0)

    def fsel(c, f, cn):
        return jnp.where(cn[c] > 0, f, nf - 1)

    return pl.pallas_call(
        functools.partial(_ffn_moe_body, sub=sub, per_step=per_step),
        grid_spec=pltpu.PrefetchScalarGridSpec(
            num_scalar_prefetch=3, grid=(n_chunks, nf),
            in_specs=[pl.BlockSpec(memory_space=pl.ANY),
                      pl.BlockSpec((None, None, d, tf),
                                   lambda c, f, ce, cn, tok: (layer, ce[c], 0, fsel(c, f, cn))),
                      pl.BlockSpec((None, None, d, tf),
                                   lambda c, f, ce, cn, tok: (layer, ce[c], 0, fsel(c, f, cn))),
                      pl.BlockSpec((None, None, tf, d),
                                   lambda c, f, ce, cn, tok: (layer, ce[c], fsel(c, f, cn), 0))],
            out_specs=pl.BlockSpec((rows, d), lambda c, f, ce, cn, tok: (c, 0)),
            scratch_shapes=[pltpu.VMEM((2, rows * pk, LANE), jnp.uint32), pltpu.VMEM((rows, d), BF16),
                            pltpu.VMEM((d, tf), BF16), pltpu.VMEM((d, tf), BF16), pltpu.VMEM((tf, d), BF16),
                            pltpu.SemaphoreType.DMA((2,))]),
        out_shape=jax.ShapeDtypeStruct((p, d), F32),
        compiler_params=_params("arbitrary", "arbitrary"), name="ffn_moe",
    )(chunk_expert, jnp.pad(chunk_rows, (0, 1)), tok_of_slot, h, w_gate, w_up, w_down)


def _row_copy(src, src_row, dst, dst_row, sem):
    return pltpu.make_async_copy(src.at[pl.ds(src_row, 1)], dst.at[pl.ds(dst_row, 1)], sem)


def _combine_body(s1, s2, xm, xt, gm, gt, yb, om, ot, abuf, bbuf, sem, *, nb, tb, npr):
    m = pl.program_id(0)

    def run(x_ref, g_ref, o_ref, base, rows):
        def issue(r, carry):
            _row_copy(yb, s1[base + r], abuf, r, sem).start()
            _row_copy(yb, s2[base + r], bbuf, r, sem).start()
            return carry

        lax.fori_loop(0, rows, issue, 0, unroll=8)

        def drain(r, carry):
            _row_copy(yb, 0, abuf, r, sem).wait()
            _row_copy(yb, 0, bbuf, r, sem).wait()
            return carry

        lax.fori_loop(0, rows, drain, 0, unroll=8)
        g = g_ref[...]
        o_ref[...] = x_ref[...] + g[:, 0:1] * abuf[0:rows, :] + g[:, 1:2] * bbuf[0:rows, :]

    @pl.when(m < nb)
    def _():
        run(xm, gm, om, m * tb, tb)

    @pl.when(m == nb)
    def _():
        run(xt, gt, ot, npr, TAIL)


def _combine(xm, xt, gm, gt, yb, slot1, slot2, tb):
    npr, d = xm.shape
    nb = npr // tb
    main = pl.BlockSpec((tb, d), lambda m, s1, s2: (jnp.minimum(m, nb - 1), 0))
    tail = pl.BlockSpec((TAIL, d), lambda m, s1, s2: (0, 0))
    gmain = pl.BlockSpec((tb, LANE), lambda m, s1, s2: (jnp.minimum(m, nb - 1), 0))
    gtail = pl.BlockSpec((TAIL, LANE), lambda m, s1, s2: (0, 0))
    return pl.pallas_call(
        functools.partial(_combine_body, nb=nb, tb=tb, npr=npr),
        grid_spec=pltpu.PrefetchScalarGridSpec(
            num_scalar_prefetch=2, grid=(nb + 1,),
            in_specs=[main, tail, gmain, gtail, pl.BlockSpec(memory_space=pl.ANY)],
            out_specs=[main, tail],
            scratch_shapes=[pltpu.VMEM((tb, d), F32), pltpu.VMEM((tb, d), F32), pltpu.SemaphoreType.DMA]),
        out_shape=[jax.ShapeDtypeStruct((npr, d), F32), jax.ShapeDtypeStruct((TAIL, d), F32)],
        compiler_params=_params("arbitrary"), name="moe_combine",
    )(slot1, slot2, xm, xt, gm, gt, yb)


def _route(route_i, route_i_tail, n_tok, n_exp, rows):
    npr = route_i.shape[0]
    ids = jnp.concatenate([route_i[:, :TOP_K], route_i_tail[:n_tok - npr, :TOP_K]], axis=0)
    flat_e = ids.reshape(-1)
    onehot = (flat_e[:, None] == jnp.arange(n_exp, dtype=jnp.int32)[None, :]).astype(jnp.int32)
    before = jnp.cumsum(onehot, axis=0) - onehot
    rank = jnp.sum(before * onehot, axis=1)
    counts = jnp.sum(onehot, axis=0)
    seg = (counts + rows - 1) // rows * rows
    seg_end = jnp.cumsum(seg)
    seg_start = seg_end - seg
    slot = (seg_start[flat_e] + rank).astype(jnp.int32)
    n_chunks = (n_tok * TOP_K) // rows + n_exp
    tok = jnp.repeat(jnp.arange(n_tok, dtype=jnp.int32), TOP_K)
    tok_of_slot = jnp.zeros((n_chunks * rows,), jnp.int32).at[slot].set(tok)
    chunk_start = jnp.arange(n_chunks, dtype=jnp.int32) * rows
    chunk_e = jnp.minimum(jnp.searchsorted(seg_end, chunk_start, side="right"), n_exp - 1).astype(jnp.int32)
    chunk_rows = jnp.clip(counts[chunk_e] - (chunk_start - seg_start[chunk_e]), 0, rows).astype(jnp.int32)
    last_used = jnp.max(jnp.where(chunk_rows > 0, jnp.arange(n_chunks), 0))
    chunk_e = jnp.where(chunk_rows > 0, chunk_e, chunk_e[last_used]).astype(jnp.int32)
    slots = slot.reshape(n_tok, TOP_K)
    pad = npr + TAIL - n_tok
    slot1 = jnp.pad(slots[:, 0], (0, pad))
    slot2 = jnp.pad(slots[:, 1], (0, pad))
    return tok_of_slot, chunk_e, chunk_rows, slot1, slot2


def _lam(lam_ref, lam0):
    lp = lam_ref[...]
    a = jnp.sum(lp[0:1, :] * lp[1:2, :], axis=-1, keepdims=True)
    b = jnp.sum(lp[2:3, :] * lp[3:4, :], axis=-1, keepdims=True)
    return jnp.exp(a) - jnp.exp(b) + lam0


def _attn_body(q_ref, k_ref, v_ref, sl_ref, lam_ref, sg_ref, o_ref, vt_sc, kp_sc, s_a, s_b, p_a, p_b, al_a, al_b,
               m_sc, l_sc, acc_sc, *, blk, lam0):
    i = pl.program_id(2)
    rows = 64

    @pl.when(i == 0)
    def _():
        lane = lax.broadcasted_iota(jnp.int32, (blk, HEAD_DIM), 1)
        for c in range(vt_sc.shape[0]):
            vt_sc[c] = v_ref[c * blk:(c + 1) * blk, :].T.astype(BF16)
            pos = c * blk + lax.broadcasted_iota(jnp.int32, (blk, HEAD_DIM), 0)
            hi = lax.shift_right_logical(pos, 6)
            lo = jnp.bitwise_and(pos, 63)
            digits = jnp.where(lane < 4, jnp.where(jnp.bitwise_and(lane, 1) == 0, hi, lo), 0)
            kp_sc[c] = digits.astype(F32).astype(BF16)

    pos_cols = jnp.broadcast_to(sl_ref[...], (blk, HEAD_DIM)).astype(BF16)
    q = [jnp.concatenate([q_ref[:, c * HEAD_DIM:(c + 1) * HEAD_DIM], pos_cols], axis=1) for c in range(2)]
    m_sc[...] = jnp.full_like(m_sc, NEG_INF)
    l_sc[...] = jnp.zeros_like(l_sc)
    acc_sc[...] = jnp.zeros_like(acc_sc)
    p_b[...] = jnp.zeros_like(p_b)
    al_b[...] = jnp.ones_like(al_b)

    def scores(j, s_ref):
        k = k_ref[pl.ds(pl.multiple_of(j * blk, blk), blk), :]
        kp = kp_sc[j]
        for c in range(2):
            kc = jnp.concatenate([k[:, c * HEAD_DIM:(c + 1) * HEAD_DIM], kp], axis=1)
            s_ref[c] = _dot_nt(kc, q[c])

    def softmax(j, s_ref, p_ref, al_ref, masked):
        def piece(c, r):
            s = s_ref[c, r * rows:(r + 1) * rows, :]
            if masked:
                ahead = (lax.broadcasted_iota(jnp.int32, s.shape, 0) - lax.broadcasted_iota(jnp.int32, s.shape, 1)
                         + (r * rows + (j - i) * blk))
                s = jnp.where(ahead <= 0, s, NEG_INF)
            return s

        for c in range(2):
            top = piece(c, 0).reshape(rows // 8, 8, blk).max(axis=0)
            for r in range(1, blk // rows):
                top = jnp.maximum(top, piece(c, r).reshape(rows // 8, 8, blk).max(axis=0))
            m_old = m_sc[c]
            m_new = jnp.maximum(m_old, jnp.max(top, axis=0, keepdims=True))
            alpha = jnp.exp2(m_old - m_new)
            total = jnp.zeros((8, blk), F32)
            for r in range(blk // rows):
                p = jnp.exp2(piece(c, r) - m_new)
                total = total + p.reshape(rows // 8, 8, blk).sum(axis=0)
                p_ref[c, r * rows:(r + 1) * rows, :] = p.astype(BF16)
            l_sc[c] = alpha * l_sc[c] + jnp.sum(total, axis=0, keepdims=True)
            m_sc[c] = m_new
            al_ref[c] = alpha

    def values(j, p_ref, al_ref):
        vt = vt_sc[j]
        for c in range(2):
            acc_sc[c] = al_ref[c] * acc_sc[c] + _dot(vt, p_ref[c])

    n_pairs = (i + 2) // 2
    scores(0, s_a)

    def pair(t, carry):
        values(jnp.maximum(2 * t - 1, 0), p_b, al_b)
        softmax(2 * t, s_a, p_a, al_a, False)
        scores(2 * t + 1, s_b)
        values(2 * t, p_a, al_a)
        softmax(2 * t + 1, s_b, p_b, al_b, False)
        scores(2 * t + 2, s_a)
        return carry

    lax.fori_loop(0, n_pairs - 1, pair, 0)

    j0 = 2 * n_pairs - 2
    values(jnp.maximum(j0 - 1, 0), p_b, al_b)
    softmax(j0, s_a, p_a, al_a, True)

    @pl.when(j0 < i)
    def _():
        scores(i, s_b)
        values(j0, p_a, al_a)
        softmax(i, s_b, p_b, al_b, True)
        values(i, p_b, al_b)

    @pl.when(j0 == i)
    def _():
        values(j0, p_a, al_a)

    o = acc_sc[0] * (1.0 / l_sc[0]) - _lam(lam_ref, lam0) * (acc_sc[1] * (1.0 / l_sc[1]))
    ms = jnp.mean(o * o, axis=0, keepdims=True)
    o = (o * lax.rsqrt(ms + EPS)).T
    o_ref[...] = (o * sg_ref[...] * (1.0 - lam0)).astype(o_ref.dtype)


def _attn_prompt(q, k, v, slopes, lam_p, sub_gain, n_batch, seq_len, blk, lam0):
    npr, d = q.shape
    assert seq_len <= 64 * 64 and blk % 64 == 0
    s_hi = slopes.astype(BF16).astype(F32)
    s_lo = (slopes - s_hi).astype(BF16).astype(F32)
    pos_mult = jnp.pad(jnp.stack([64.0 * s_hi, s_hi, 64.0 * s_lo, s_lo], axis=1), ((0, 0), (0, LANE - 4)))
    slopes = pos_mult[:, None, :]
    n_heads = d // V_DIM
    nq = seq_len // blk
    kv_spec = pl.BlockSpec((seq_len, V_DIM), lambda b, h, i: (b, h))
    return pl.pallas_call(
        functools.partial(_attn_body, blk=blk, lam0=lam0),
        grid=(n_batch, n_heads, nq),
        in_specs=[pl.BlockSpec((blk, V_DIM), lambda b, h, i: (b * nq + i, h)), kv_spec, kv_spec,
                  pl.BlockSpec((None, 1, LANE), lambda b, h, i: (h, 0, 0)),
                  pl.BlockSpec((4, HEAD_DIM), lambda b, h, i: (0, 0)),
                  pl.BlockSpec((1, V_DIM), lambda b, h, i: (0, 0))],
        out_specs=pl.BlockSpec((blk, V_DIM), lambda b, h, i: (b * nq + i, h)),
        out_shape=jax.ShapeDtypeStruct((npr, d), BF16),
        scratch_shapes=[pltpu.VMEM((nq, V_DIM, blk), BF16), pltpu.VMEM((nq, blk, HEAD_DIM), BF16),
                        pltpu.VMEM((2, blk, blk), F32), pltpu.VMEM((2, blk, blk), F32),
                        pltpu.VMEM((2, blk, blk), BF16), pltpu.VMEM((2, blk, blk), BF16),
                        pltpu.VMEM((2, 1, blk), F32), pltpu.VMEM((2, 1, blk), F32),
                        pltpu.VMEM((2, 1, blk), F32), pltpu.VMEM((2, 1, blk), F32),
                        pltpu.VMEM((2, V_DIM, blk), F32)],
        compiler_params=_params("arbitrary", "arbitrary", "arbitrary"), name="attn_prompt",
    )(q, k, v, slopes, lam_p, sub_gain)


def _attn_sample_body(pt, q_ref, kn_ref, vn_ref, sl_ref, lam_ref, sg_ref, *refs, pps, page, q_pos, lam0):
    k_refs = refs[:pps]
    v_refs = refs[pps:2 * pps]
    o_ref, m_sc, l_sc, acc_sc = refs[2 * pps:]
    p = pl.program_id(1)
    n_rows = q_ref.shape[0]
    n_heads = n_rows // 2
    cols = page * n_heads
    row = lax.broadcasted_iota(jnp.int32, (n_rows, cols), 0)
    col = lax.broadcasted_iota(jnp.int32, (n_rows, cols), 1)
    own = jnp.bitwise_and(col, n_heads - 1) == jnp.bitwise_and(row, n_heads - 1)
    key = lax.shift_right_logical(lax.broadcasted_iota(jnp.int32, (1, cols), 1), n_heads.bit_length() - 1)
    slope = sl_ref[:, 0:1]

    @pl.when(p == 0)
    def _():
        m_sc[...] = jnp.full_like(m_sc, NEG_INF)
        l_sc[...] = jnp.zeros_like(l_sc)
        acc_sc[...] = jnp.zeros_like(acc_sc)

    def update(s, pv_fn):
        m_old = m_sc[...]
        m_new = jnp.maximum(m_old, jnp.max(s, axis=-1, keepdims=True))
        alpha = jnp.exp2(m_old - m_new)
        pr = jnp.exp2(s - m_new)
        l_sc[...] = alpha * l_sc[...] + jnp.sum(pr, axis=-1, keepdims=True)
        acc_sc[...] = alpha * acc_sc[...] + pv_fn(pr)
        m_sc[...] = m_new

    qb = q_ref[...].astype(BF16)
    scores = []
    for i in range(pps):
        s_c = [_dot_nt(qb, k_refs[i][pl.ds(c, cols, stride=2), :].astype(BF16)) for c in range(2)]
        s = jnp.concatenate([s_c[0][:n_heads], s_c[1][n_heads:]], axis=0)
        dist = (q_pos - ((p * pps + i) * page + key)).astype(F32)
        scores.append(jnp.where(own, s - slope * dist, NEG_INF))

    def weighted_values(pr):
        pr = pr.astype(BF16)
        return sum(_dot(pr[:, i * cols:(i + 1) * cols], v_refs[i][...].astype(BF16)) for i in range(pps))

    update(jnp.concatenate(scores, axis=1), weighted_values)

    @pl.when(p == pl.num_programs(1) - 1)
    def _():
        s_new = jnp.sum(q_ref[...] * kn_ref[...], axis=-1, keepdims=True)
        vn = vn_ref[...]
        update(s_new, lambda pr: pr * jnp.concatenate([vn, vn], axis=0))
        o_hc = acc_sc[...] * (1.0 / l_sc[...])
        o = o_hc[:n_heads] - _lam(lam_ref, lam0) * o_hc[n_heads:]
        o_ref[...] = _rms(o, sg_ref[...]) * (1.0 - lam0)


def _attn_sample(q, k_new, v_new, cache_k, cache_v, page_table, slope_rows, lam_p, sub_gain, pps, lam0):
    n_seq, n_rows, _ = q.shape
    n_heads = n_rows // 2
    assert n_heads & (n_heads - 1) == 0
    n_pages = page_table.shape[1]
    n_pool, page = cache_k.shape[:2]
    ck = cache_k.reshape(n_pool * page * n_rows, HEAD_DIM)
    cv = cache_v.reshape(n_pool * page * n_heads, V_DIM)
    qspec = pl.BlockSpec((None, n_rows, HEAD_DIM), lambda b, p, pt: (b, 0, 0))
    vspec = pl.BlockSpec((None, n_heads, V_DIM), lambda b, p, pt: (b, 0, 0))

    def page_idx(b, p, pt, i):
        return pt[b * n_pages + p * pps + i]

    k_specs = [pl.BlockSpec((page * n_rows, HEAD_DIM), lambda b, p, pt, i=i: (page_idx(b, p, pt, i), 0))
               for i in range(pps)]
    v_specs = [pl.BlockSpec((page * n_heads, V_DIM), lambda b, p, pt, i=i: (page_idx(b, p, pt, i), 0))
               for i in range(pps)]
    return pl.pallas_call(
        functools.partial(_attn_sample_body, pps=pps, page=page, q_pos=n_pages * page, lam0=lam0),
        grid_spec=pltpu.PrefetchScalarGridSpec(
            num_scalar_prefetch=1, grid=(n_seq, n_pages // pps),
            in_specs=[qspec, qspec, vspec,
                      pl.BlockSpec((n_rows, LANE), lambda b, p, pt: (0, 0)),
                      pl.BlockSpec((4, HEAD_DIM), lambda b, p, pt: (0, 0)),
                      pl.BlockSpec((1, V_DIM), lambda b, p, pt: (0, 0))] + k_specs + v_specs,
            out_specs=vspec,
            scratch_shapes=[pltpu.VMEM((n_rows, 1), F32), pltpu.VMEM((n_rows, 1), F32),
                            pltpu.VMEM((n_rows, V_DIM), F32)]),
        out_shape=jax.ShapeDtypeStruct((n_seq, n_heads, V_DIM), F32),
        compiler_params=_params("arbitrary", "arbitrary"), name="attn_sample",
    )(page_table.reshape(-1), q, k_new, v_new, slope_rows, lam_p, sub_gain, *([ck] * pps), *([cv] * pps))


def _tiles(seq_len, d_model, d_ff, d_ff_e):
    tm = min(1024, seq_len)
    return dict(
        tm=tm,
        tr=min(512, tm),
        tn=min(1024, d_model),
        tn_conv=min(256, d_model),
        tf=256,
        sub=min(512, tm),
        moe_rows=tm,
        gather_rows=min(256, tm),
        attn_blk=min(512, seq_len),
        pages_per_step=8,
    )


def kernel(x_prompt, x_sample, state_conv, cache_k, cache_v, page_table, mix_norm, ffn_norm, a_w_in, a_conv,
           a_w_out, kv_norm, w_kv, k_norm, b_w_q, b_q_norm, b_lam, b_subln, b_w_o, ffn_w_gate, ffn_w_up,
           ffn_w_down, moe_router, moe_w_gate, moe_w_up, moe_w_down):
    n_batch, seq_len, d = x_prompt.shape
    n_seq = x_sample.shape[0]
    assert x_sample.shape[1] == 1 and n_seq <= TAIL
    depth = mix_norm.shape[0]
    n_a = a_w_in.shape[0]
    n_heads = d // V_DIM
    n_exp = moe_router.shape[-1]
    k_width = n_heads * 2 * HEAD_DIM
    npr = n_batch * seq_len
    n_tok = npr + n_seq
    t = _tiles(seq_len, d, ffn_w_gate.shape[-1], moe_w_gate.shape[-1])
    tm, tn, tr = t["tm"], t["tn"], t["tr"]
    nb = npr // tm
    bps = seq_len // tm

    xm = x_prompt.reshape(npr, d)
    xt = jnp.pad(x_sample.reshape(n_seq, d), ((0, TAIL - n_seq), (0, 0)))

    slopes = jnp.exp2(-8.0 * jnp.arange(1, n_heads + 1, dtype=F32) / n_heads) * LOG2E
    slope_rows = jnp.broadcast_to(jnp.tile(slopes, 2)[:, None], (2 * n_heads, LANE))
    k_gain = jnp.tile(k_norm.reshape(-1), n_heads)

    def comp_major(a):
        return a[:n_seq].reshape(n_seq, n_heads, 2, HEAD_DIM).transpose(0, 2, 1, 3).reshape(n_seq, 2 * n_heads, HEAD_DIM)

    def pad_tail(a):
        return jnp.pad(a, ((0, TAIL - n_seq), (0, 0)))

    conv_prompt, conv_sample = [], []
    pending = None
    k_f32 = v_f32 = k_bf = v_bf = None
    for l in range(depth):
        if pending is None:
            hm, ht = _norm(xm, xt, mix_norm[l], tr)
        else:
            hm, ht, xm, xt = _norm(xm, xt, mix_norm[l], tr, add=pending)
            pending = None
        if l < n_a:
            prev0 = pad_tail(state_conv[l, :, 0, :])
            prev1 = pad_tail(state_conv[l, :, 1, :])
            gm, gt, ut = _conv_mixer(hm, ht, a_w_in, a_conv, l, prev0, prev1, tm, t["tn_conv"], seq_len)
            ut = ut.reshape(nb + 1, 8, d)
            conv_prompt.append(ut[bps - 1:nb:bps, 6:8, :])
            conv_sample.append(jnp.stack([state_conv[l, :, 1, :], ut[nb, :n_seq, :]], axis=1))
            xm, xt = _matmul(gm, gt, a_w_out, l, 0, d, tm, tn, [(F32, F32)], res=(xm, xt))
        else:
            j = l - n_a
            if k_f32 is None:
                nm, nt = _norm(xm, xt, kv_norm, tr)
                k_f32, kt_f32, k_bf, _ = _matmul(nm, nt, w_kv, 0, 0, k_width, tm, tn,
                                                 [(F32, F32), (BF16, BF16)], gain=k_gain)
                v_f32, vt_f32 = _matmul(nm, nt, w_kv, 0, k_width, d, tm, tn, [(F32, F32)])
            lam0 = 0.8 - 0.6 * math.exp(-0.3 * l)
            q_gain = jnp.tile(b_q_norm[j].reshape(-1), n_heads) * (HEAD_DIM ** -0.5 * LOG2E)
            qm, qt = _matmul(hm, ht, b_w_q, j, 0, k_width, tm, tn, [(BF16, F32)], gain=q_gain)
            sub_gain = b_subln[j].reshape(1, V_DIM)
            om = _attn_prompt(qm, k_bf, v_f32, slopes, b_lam[j], sub_gain, n_batch, seq_len,
                              t["attn_blk"], lam0)
            os_ = _attn_sample(comp_major(qt), comp_major(kt_f32), vt_f32[:n_seq].reshape(n_seq, n_heads, V_DIM),
                               cache_k, cache_v, page_table, slope_rows, b_lam[j], sub_gain,
                               t["pages_per_step"], lam0)
            ot = pad_tail(os_.reshape(n_seq, d)).astype(BF16)
            xm, xt = _matmul(om, ot, b_w_o, j, 0, d, tm, tn, [(F32, F32)], res=(xm, xt))
        i = l // 2
        if l % 2 == 0:
            h2m, h2t = _norm(xm, xt, ffn_norm[l], tr)
            pending = _ffn_dense(h2m, h2t, ffn_w_gate, ffn_w_up, ffn_w_down, i, tm, t["tf"])
        else:
            h2, rim, rit, rgm, rgt = _norm_router(xm, xt, ffn_norm[l], moe_router[i], tr)
            tok_of_slot, chunk_e, chunk_rows, slot1, slot2 = _route(rim, rit, n_tok, n_exp, t["moe_rows"])
            yb = _ffn_moe(h2, tok_of_slot, chunk_e, chunk_rows, moe_w_gate, moe_w_up, moe_w_down, i,
                          d, t["moe_rows"], t["tf"], t["sub"])
            valid = (jnp.arange(TAIL) < n_seq)[:, None]
            xm, xt = _combine(xm, xt, rgm, jnp.where(valid, rgt, 0.0), yb, slot1, slot2, t["gather_rows"])
    if pending is not None:
        xm, xt = xm + pending[0], xt + pending[1]

    y_prompt = xm.reshape(n_batch, seq_len, d)
    y_sample = xt[:n_seq].reshape(n_seq, 1, d)
    return (y_prompt, y_sample, jnp.stack(conv_prompt), jnp.stack(conv_sample),
            k_f32.reshape(n_batch, seq_len, n_heads, 2, HEAD_DIM),
            v_f32.reshape(n_batch, seq_len, n_heads, V_DIM),
            kt_f32[:n_seq].reshape(n_seq, 1, n_heads, 2, HEAD_DIM),
            vt_f32[:n_seq].reshape(n_seq, 1, n_heads, V_DIM))
```

```python
import functools
import math

import jax
import jax.numpy as jnp
from jax import lax
from jax.experimental import pallas as pl
from jax.experimental.pallas import tpu as pltpu

F32 = jnp.float32
BF16 = jnp.bfloat16
EPS = 1e-6
HEAD_DIM = 128
V_DIM = 2 * HEAD_DIM
TOP_K = 2
CONV_W = 3
LANE = 128
TAIL = 16
V7X_VMEM_BYTES = 64 * 1024 * 1024
VMEM_LIMIT = V7X_VMEM_BYTES - 8 * 1024 * 1024
NEG_INF = float("-inf")
LOG2E = 1.0 / math.log(2.0)


def _params(*sem):
    return pltpu.CompilerParams(dimension_semantics=sem, vmem_limit_bytes=VMEM_LIMIT)


def _dot(a, b):
    return jnp.dot(a, b, preferred_element_type=F32)


def _dot_nt(a, b):
    return lax.dot_general(a, b, (((1,), (1,)), ((), ())), preferred_element_type=F32)


def _rms(x, g):
    ms = jnp.mean(x * x, axis=-1, keepdims=True)
    return x * lax.rsqrt(ms + EPS) * g


def _group_rms(x, gsize):
    outs = []
    for g in range(x.shape[-1] // gsize):
        blk = x[:, g * gsize:(g + 1) * gsize]
        ms = jnp.mean(blk * blk, axis=-1, keepdims=True)
        outs.append(blk * lax.rsqrt(ms + EPS))
    return outs[0] if len(outs) == 1 else jnp.concatenate(outs, axis=-1)


def _norm_body(*refs, nb, has_add):
    if has_add:
        xm, xt, ym, yt, g, om, ot, sm, st = refs
    else:
        xm, xt, g, om, ot = refs
        ym = yt = sm = st = None
    m = pl.program_id(0)

    def run(x_ref, y_ref, o_ref, s_ref):
        x = x_ref[...]
        if has_add:
            x = x + y_ref[...]
            s_ref[...] = x
        o_ref[...] = _rms(x, g[...]).astype(o_ref.dtype)

    @pl.when(m < nb)
    def _():
        run(xm, ym, om, sm)

    @pl.when(m == nb)
    def _():
        run(xt, yt, ot, st)


def _norm(xm, xt, gain, tm, add=None, out_dtype=BF16):
    npr, d = xm.shape
    nb = npr // tm
    main = pl.BlockSpec((tm, d), lambda m: (jnp.minimum(m, nb - 1), 0))
    tail = pl.BlockSpec((TAIL, d), lambda m: (0, 0))
    gspec = pl.BlockSpec((1, d), lambda m: (0, 0))
    ins = [xm, xt]
    in_specs = [main, tail]
    out_shape = [jax.ShapeDtypeStruct((npr, d), out_dtype), jax.ShapeDtypeStruct((TAIL, d), out_dtype)]
    out_specs = [main, tail]
    if add is not None:
        ins += list(add)
        in_specs += [main, tail]
        out_shape += [jax.ShapeDtypeStruct((npr, d), F32), jax.ShapeDtypeStruct((TAIL, d), F32)]
        out_specs += [main, tail]
    ins.append(gain.reshape(1, d))
    in_specs.append(gspec)
    return pl.pallas_call(
        functools.partial(_norm_body, nb=nb, has_add=add is not None),
        grid=(nb + 1,), in_specs=in_specs, out_specs=out_specs, out_shape=out_shape,
        compiler_params=_params("arbitrary"), name="rmsnorm",
    )(*ins)


def _router_body(xm, xt, g, wr, h_all, im, it, gm, gt, *, nb, n_exp):
    m = pl.program_id(0)

    def run(x_ref, i_ref, g_ref):
        h = _rms(x_ref[...], g[...])
        rows, d = h.shape
        pk = d // (2 * LANE)
        bits = lax.bitcast_convert_type(h.astype(BF16).astype(F32), jnp.uint32)
        for j in range(pk):
            lo = lax.shift_right_logical(bits[:, j * LANE:(j + 1) * LANE], jnp.uint32(16))
            hi = bits[:, d // 2 + j * LANE:d // 2 + (j + 1) * LANE]
            h_all[pl.ds(j, rows, stride=pk), :] = jnp.bitwise_or(hi, lo)
        if rows * pk < h_all.shape[0]:
            h_all[rows * pk:, :] = jnp.zeros((h_all.shape[0] - rows * pk, LANE), jnp.uint32)
        logits = jnp.dot(h, wr[...], preferred_element_type=F32, precision=lax.Precision.HIGHEST)
        lane = lax.broadcasted_iota(jnp.int32, logits.shape, 1)
        logits = jnp.where(lane < n_exp, logits, NEG_INF)
        v1 = jnp.max(logits, axis=-1, keepdims=True)
        i1 = jnp.min(jnp.where(logits == v1, lane, LANE), axis=-1, keepdims=True)
        rest = jnp.where(lane == i1, NEG_INF, logits)
        v2 = jnp.max(rest, axis=-1, keepdims=True)
        i2 = jnp.min(jnp.where(rest == v2, lane, LANE), axis=-1, keepdims=True)
        e = jnp.exp(v2 - v1)
        g1 = 1.0 / (1.0 + e)
        g2 = e / (1.0 + e)
        i_ref[...] = jnp.where(lane == 0, i1, jnp.where(lane == 1, i2, 0))
        g_ref[...] = jnp.where(lane == 0, g1, jnp.where(lane == 1, g2, 0.0))

    @pl.when(m < nb)
    def _():
        run(xm, im, gm)

    @pl.when(m == nb)
    def _():
        run(xt, it, gt)


def _norm_router(xm, xt, gain, w_router, tm):
    npr, d = xm.shape
    n_exp = w_router.shape[-1]
    nb = npr // tm
    wr = jnp.pad(w_router, ((0, 0), (0, LANE - n_exp)))
    main = pl.BlockSpec((tm, d), lambda m: (jnp.minimum(m, nb - 1), 0))
    tail = pl.BlockSpec((TAIL, d), lambda m: (0, 0))
    rmain = pl.BlockSpec((tm, LANE), lambda m: (jnp.minimum(m, nb - 1), 0))
    rtail = pl.BlockSpec((TAIL, LANE), lambda m: (0, 0))
    return pl.pallas_call(
        functools.partial(_router_body, nb=nb, n_exp=n_exp),
        grid=(nb + 1,),
        in_specs=[main, tail, pl.BlockSpec((1, d), lambda m: (0, 0)),
                  pl.BlockSpec((d, LANE), lambda m: (0, 0))],
        out_specs=[pl.BlockSpec((tm * (d // (2 * LANE)), LANE), lambda m: (m, 0)), rmain, rtail, rmain, rtail],
        out_shape=[jax.ShapeDtypeStruct(((npr + tm) * (d // (2 * LANE)), LANE), jnp.uint32),
                   jax.ShapeDtypeStruct((npr, LANE), jnp.int32), jax.ShapeDtypeStruct((TAIL, LANE), jnp.int32),
                   jax.ShapeDtypeStruct((npr, LANE), F32), jax.ShapeDtypeStruct((TAIL, LANE), F32)],
        compiler_params=_params("arbitrary"), name="rmsnorm_router",
    )(xm, xt, gain.reshape(1, d), wr)


def _mm_body(*refs, nb, has_res, has_gain, n_out):
    refs = list(refs)
    xm, xt, w = refs[:3]
    pos = 3
    rm = rt = gain = None
    if has_res:
        rm, rt = refs[pos:pos + 2]
        pos += 2
    if has_gain:
        gain = refs[pos]
        pos += 1
    outs = refs[pos:pos + 2 * n_out]
    wb = refs[pos + 2 * n_out]
    m = pl.program_id(1)

    @pl.when(m == 0)
    def _():
        wb[...] = w[...].astype(BF16)

    def run(x_ref, r_ref, o_refs):
        acc = _dot(x_ref[...], wb[...])
        if has_gain:
            acc = _group_rms(acc, HEAD_DIM) * gain[...]
        if has_res:
            acc = acc + r_ref[...]
        for o in o_refs:
            o[...] = acc.astype(o.dtype)

    @pl.when(m < nb)
    def _():
        run(xm, rm, outs[0::2])

    @pl.when(m == nb)
    def _():
        run(xt, rt, outs[1::2])


def _matmul(xm, xt, w, layer, col0, n_cols, tm, tn, out_dtypes, res=None, gain=None):
    npr, k = xm.shape
    nb = npr // tm
    cb = col0 // tn
    xmain = pl.BlockSpec((tm, k), lambda n, m: (jnp.minimum(m, nb - 1), 0))
    xtail = pl.BlockSpec((TAIL, k), lambda n, m: (0, 0))
    omain = pl.BlockSpec((tm, tn), lambda n, m: (jnp.minimum(m, nb - 1), n))
    otail = pl.BlockSpec((TAIL, tn), lambda n, m: (0, n))
    if w.ndim == 3:
        wspec = pl.BlockSpec((None, k, tn), lambda n, m: (layer, 0, cb + n))
    else:
        wspec = pl.BlockSpec((k, tn), lambda n, m: (0, cb + n))
    ins, in_specs = [xm, xt, w], [xmain, xtail, wspec]
    if res is not None:
        ins += list(res)
        in_specs += [omain, otail]
    if gain is not None:
        ins.append(gain.reshape(1, n_cols))
        in_specs.append(pl.BlockSpec((1, tn), lambda n, m: (0, n)))
    out_shape, out_specs = [], []
    for dm, dt in out_dtypes:
        out_shape += [jax.ShapeDtypeStruct((npr, n_cols), dm), jax.ShapeDtypeStruct((TAIL, n_cols), dt)]
        out_specs += [omain, otail]
    return pl.pallas_call(
        functools.partial(_mm_body, nb=nb, has_res=res is not None, has_gain=gain is not None,
                          n_out=len(out_dtypes)),
        grid=(n_cols // tn, nb + 1), in_specs=in_specs, out_specs=out_specs, out_shape=out_shape,
        scratch_shapes=[pltpu.VMEM((k, tn), BF16)],
        compiler_params=_params("arbitrary", "arbitrary"), name="matmul",
    )(*ins)


def _conv_body(xm, xt, w_b, w_c, w_v, cw, p0, p1, gm, gt, ut, wbs, wcs, wvs, carry, *, nb, bps):
    m = pl.program_id(1)

    @pl.when(m == 0)
    def _():
        wbs[...] = w_b[...].astype(BF16)
        wcs[...] = w_c[...].astype(BF16)
        wvs[...] = w_v[...].astype(BF16)

    w0, w1, w2 = cw[0:1, :], cw[1:2, :], cw[2:3, :]

    @pl.when(m < nb)
    def _():
        x = xm[...]
        u = _dot(x, wcs[...]) * _dot(x, wvs[...])
        b = _dot(x, wbs[...])

        @pl.when(m % bps == 0)
        def _():
            carry[...] = jnp.zeros_like(carry)

        prev1 = carry[7:8, :]
        prev2 = carry[6:7, :]
        row = lax.broadcasted_iota(jnp.int32, u.shape, 0)
        u1 = jnp.where(row == 0, prev1, pltpu.roll(u, 1, 0))
        u2 = jnp.where(row == 0, prev2, jnp.where(row == 1, prev1, pltpu.roll(u, 2, 0)))
        gm[...] = (b * (w0 * u2 + w1 * u1 + w2 * u)).astype(gm.dtype)
        tail_rows = u[u.shape[0] - 8:, :]
        carry[...] = tail_rows
        ut[...] = tail_rows

    @pl.when(m == nb)
    def _():
        x = xt[...]
        u = _dot(x, wcs[...]) * _dot(x, wvs[...])
        b = _dot(x, wbs[...])
        gt[...] = (b * (w0 * p0[...] + w1 * p1[...] + w2 * u)).astype(gt.dtype)
        ut[...] = u[0:8, :]


def _conv_mixer(hm, ht, w_in, conv_w, layer, prev0, prev1, tm, tn, seq_len):
    npr, d = hm.shape
    nb = npr // tm
    nn = d // tn
    xmain = pl.BlockSpec((tm, d), lambda n, m: (jnp.minimum(m, nb - 1), 0))
    xtail = pl.BlockSpec((TAIL, d), lambda n, m: (0, 0))
    omain = pl.BlockSpec((tm, tn), lambda n, m: (jnp.minimum(m, nb - 1), n))
    otail = pl.BlockSpec((TAIL, tn), lambda n, m: (0, n))

    def wspec(part):
        return pl.BlockSpec((None, d, tn), lambda n, m: (layer, 0, part * nn + n))

    return pl.pallas_call(
        functools.partial(_conv_body, nb=nb, bps=seq_len // tm),
        grid=(nn, nb + 1),
        in_specs=[xmain, xtail, wspec(0), wspec(1), wspec(2),
                  pl.BlockSpec((None, CONV_W, tn), lambda n, m: (layer, 0, n)), otail, otail],
        out_specs=[omain, otail, pl.BlockSpec((8, tn), lambda n, m: (m, n))],
        out_shape=[jax.ShapeDtypeStruct((npr, d), BF16), jax.ShapeDtypeStruct((TAIL, d), BF16),
                   jax.ShapeDtypeStruct(((nb + 1) * 8, d), F32)],
        scratch_shapes=[pltpu.VMEM((d, tn), BF16)] * 3 + [pltpu.VMEM((8, tn), F32)],
        compiler_params=_params("arbitrary", "arbitrary"), name="conv_mixer",
    )(hm, ht, w_in, w_in, w_in, conv_w, prev0, prev1)


def _ffn_rows(x, wgs, wus, wds):
    g = _dot(x, wgs[...])
    a = (g * jax.nn.sigmoid(g) * _dot(x, wus[...])).astype(BF16)
    return _dot(a, wds[...])


def _cast_weights(wg, wu, wd, wgs, wus, wds):
    wgs[...] = wg[...].astype(BF16)
    wus[...] = wu[...].astype(BF16)
    wds[...] = wd[...].astype(BF16)


def _ffn_dense_body(xm, xt, wg, wu, wd, om, ot, wgs, wus, wds, *, nb):
    c = pl.program_id(0)
    f = pl.program_id(1)

    @pl.when(c < nb)
    def _():
        @pl.when(f == 0)
        def _():
            om[...] = jnp.zeros_like(om)

        _cast_weights(wg, wu, wd, wgs, wus, wds)
        om[...] += _ffn_rows(xm[...], wgs, wus, wds)

    @pl.when(c == nb)
    def _():
        @pl.when(f == 0)
        def _():
            ot[...] = jnp.zeros_like(ot)

        _cast_weights(wg, wu, wd, wgs, wus, wds)
        ot[...] += _ffn_rows(xt[...], wgs, wus, wds)


def _ffn_dense(hm, ht, w_gate, w_up, w_down, layer, tm, tf):
    npr, d = hm.shape
    nb = npr // tm
    ff = w_gate.shape[-1]
    main = pl.BlockSpec((tm, d), lambda c, f: (jnp.minimum(c, nb - 1), 0))
    tail = pl.BlockSpec((TAIL, d), lambda c, f: (0, 0))
    return pl.pallas_call(
        functools.partial(_ffn_dense_body, nb=nb),
        grid=(nb + 1, ff // tf),
        in_specs=[main, tail,
                  pl.BlockSpec((None, d, tf), lambda c, f: (layer, 0, f)),
                  pl.BlockSpec((None, d, tf), lambda c, f: (layer, 0, f)),
                  pl.BlockSpec((None, tf, d), lambda c, f: (layer, f, 0))],
        out_specs=[main, tail],
        out_shape=[jax.ShapeDtypeStruct((npr, d), F32), jax.ShapeDtypeStruct((TAIL, d), F32)],
        scratch_shapes=[pltpu.VMEM((d, tf), BF16), pltpu.VMEM((d, tf), BF16), pltpu.VMEM((tf, d), BF16)],
        compiler_params=_params("arbitrary", "arbitrary"), name="ffn_dense",
    )(hm, ht, w_gate, w_up, w_down)


def _ffn_moe_body(ce, cn, tok, h, wg, wu, wd, o, gbuf, xsb, wgs, wus, wds, sem, *, sub, per_step):
    c = pl.program_id(0)
    f = pl.program_id(1)
    rows, d = o.shape
    pk = d // (2 * LANE)
    nv = cn[c]
    slot = lax.rem(c, 2)

    def issue(chunk, first, count, s):
        def body(i, carry):
            r = first + i
            t = tok[chunk * rows + r]
            pltpu.make_async_copy(h.at[pl.ds(t * pk, pk)], gbuf.at[s, pl.ds(r * pk, pk)], sem.at[s]).start()
            return carry

        lax.fori_loop(0, count, body, 0, unroll=8)

    @pl.when(jnp.logical_and(jnp.logical_and(c == 0, f == 0), nv > 0))
    def _():
        issue(0, 0, rows, 0)

    @pl.when(f == 0)
    def _():
        o[...] = jnp.zeros_like(o)

        @pl.when(nv > 0)
        def _():
            pltpu.make_async_copy(h.at[pl.ds(0, rows * pk)], gbuf.at[slot], sem.at[slot]).wait()
            for j in range(pk):
                w = gbuf[slot, pl.ds(j, rows, stride=pk), :]
                lo = lax.bitcast_convert_type(lax.shift_left(w, jnp.uint32(16)), F32)
                hi = lax.bitcast_convert_type(jnp.bitwise_and(w, jnp.uint32(0xFFFF0000)), F32)
                xsb[:, j * LANE:(j + 1) * LANE] = lo.astype(BF16)
                xsb[:, d // 2 + j * LANE:d // 2 + (j + 1) * LANE] = hi.astype(BF16)

    @pl.when(jnp.logical_and(f < rows // per_step, cn[c + 1] > 0))
    def _():
        issue(c + 1, f * per_step, per_step, 1 - slot)

    @pl.when(nv == rows)
    def _():
        _cast_weights(wg, wu, wd, wgs, wus, wds)
        o[...] += _ffn_rows(xsb[...], wgs, wus, wds)

    @pl.when(jnp.logical_and(nv > 0, nv < rows))
    def _():
        _cast_weights(wg, wu, wd, wgs, wus, wds)

        def step(i, carry):
            r = pl.ds(pl.multiple_of(i * sub, sub), sub)
            o[r, :] += _ffn_rows(xsb[r, :], wgs, wus, wds)
            return carry

        lax.fori_loop(0, (nv + sub - 1) // sub, step, 0)


def _ffn_moe(h, tok_of_slot, chunk_expert, chunk_rows, w_gate, w_up, w_down, layer, d, rows, tf, sub):
    p = tok_of_slot.shape[0]
    n_chunks = p // rows
    ff = w_gate.shape[-1]
    nf = ff // tf
    pk = d // (2 * LANE)
    per_step = next(q for q in range(-(-rows // nf), rows + 1) if rows % q == 0)

    def fsel(c, f, cn):
        return jnp.where(cn[c] > 0, f, nf - 1)

    return pl.pallas_call(
        functools.partial(_ffn_moe_body, sub=sub, per_step=per_step),
        grid_spec=pltpu.PrefetchScalarGridSpec(
            num_scalar_prefetch=3, grid=(n_chunks, nf),
            in_specs=[pl.BlockSpec(memory_space=pl.ANY),
                      pl.BlockSpec((None, None, d, tf),
                                   lambda c, f, ce, cn, tok: (layer, ce[c], 0, fsel(c, f, cn))),
                      pl.BlockSpec((None, None, d, tf),
                                   lambda c, f, ce, cn, tok: (layer, ce[c], 0, fsel(c, f, cn))),
                      pl.BlockSpec((None, None, tf, d),
                                   lambda c, f, ce, cn, tok: (layer, ce[c], fsel(c, f, cn), 0))],
            out_specs=pl.BlockSpec((rows, d), lambda c, f, ce, cn, tok: (c, 0)),
            scratch_shapes=[pltpu.VMEM((2, rows * pk, LANE), jnp.uint32), pltpu.VMEM((rows, d), BF16),
                            pltpu.VMEM((d, tf), BF16), pltpu.VMEM((d, tf), BF16), pltpu.VMEM((tf, d), BF16),
                            pltpu.SemaphoreType.DMA((2,))]),
        out_shape=jax.ShapeDtypeStruct((p, d), F32),
        compiler_params=_params("arbitrary", "arbitrary"), name="ffn_moe",
    )(chunk_expert, jnp.pad(chunk_rows, (0, 1)), tok_of_slot, h, w_gate, w_up, w_down)


def _row_copy(src, src_row, dst, dst_row, sem):
    return pltpu.make_async_copy(src.at[pl.ds(src_row, 1)], dst.at[pl.ds(dst_row, 1)], sem)


def _combine_body(s1, s2, xm, xt, gm, gt, yb, om, ot, abuf, bbuf, sem, *, nb, tb, npr):
    m = pl.program_id(0)

    def run(x_ref, g_ref, o_ref, base, rows):
        def issue(r, carry):
            _row_copy(yb, s1[base + r], abuf, r, sem).start()
            _row_copy(yb, s2[base + r], bbuf, r, sem).start()
            return carry

        lax.fori_loop(0, rows, issue, 0, unroll=8)

        def drain(r, carry):
            _row_copy(yb, 0, abuf, r, sem).wait()
            _row_copy(yb, 0, bbuf, r, sem).wait()
            return carry

        lax.fori_loop(0, rows, drain, 0, unroll=8)
        g = g_ref[...]
        o_ref[...] = x_ref[...] + g[:, 0:1] * abuf[0:rows, :] + g[:, 1:2] * bbuf[0:rows, :]

    @pl.when(m < nb)
    def _():
        run(xm, gm, om, m * tb, tb)

    @pl.when(m == nb)
    def _():
        run(xt, gt, ot, npr, TAIL)


def _combine(xm, xt, gm, gt, yb, slot1, slot2, tb):
    npr, d = xm.shape
    nb = npr // tb
    main = pl.BlockSpec((tb, d), lambda m, s1, s2: (jnp.minimum(m, nb - 1), 0))
    tail = pl.BlockSpec((TAIL, d), lambda m, s1, s2: (0, 0))
    gmain = pl.BlockSpec((tb, LANE), lambda m, s1, s2: (jnp.minimum(m, nb - 1), 0))
    gtail = pl.BlockSpec((TAIL, LANE), lambda m, s1, s2: (0, 0))
    return pl.pallas_call(
        functools.partial(_combine_body, nb=nb, tb=tb, npr=npr),
        grid_spec=pltpu.PrefetchScalarGridSpec(
            num_scalar_prefetch=2, grid=(nb + 1,),
            in_specs=[main, tail, gmain, gtail, pl.BlockSpec(memory_space=pl.ANY)],
            out_specs=[main, tail],
            scratch_shapes=[pltpu.VMEM((tb, d), F32), pltpu.VMEM((tb, d), F32), pltpu.SemaphoreType.DMA]),
        out_shape=[jax.ShapeDtypeStruct((npr, d), F32), jax.ShapeDtypeStruct((TAIL, d), F32)],
        compiler_params=_params("arbitrary"), name="moe_combine",
    )(slot1, slot2, xm, xt, gm, gt, yb)


def _route(route_i, route_i_tail, n_tok, n_exp, rows):
    npr = route_i.shape[0]
    ids = jnp.concatenate([route_i[:, :TOP_K], route_i_tail[:n_tok - npr, :TOP_K]], axis=0)
    flat_e = ids.reshape(-1)
    onehot = (flat_e[:, None] == jnp.arange(n_exp, dtype=jnp.int32)[None, :]).astype(jnp.int32)
    before = jnp.cumsum(onehot, axis=0) - onehot
    rank = jnp.sum(before * onehot, axis=1)
    counts = jnp.sum(onehot, axis=0)
    seg = (counts + rows - 1) // rows * rows
    seg_end = jnp.cumsum(seg)
    seg_start = seg_end - seg
    slot = (seg_start[flat_e] + rank).astype(jnp.int32)
    n_chunks = (n_tok * TOP_K) // rows + n_exp
    tok = jnp.repeat(jnp.arange(n_tok, dtype=jnp.int32), TOP_K)
    tok_of_slot = jnp.zeros((n_chunks * rows,), jnp.int32).at[slot].set(tok)
    chunk_start = jnp.arange(n_chunks, dtype=jnp.int32) * rows
    chunk_e = jnp.minimum(jnp.searchsorted(seg_end, chunk_start, side="right"), n_exp - 1).astype(jnp.int32)
    chunk_rows = jnp.clip(counts[chunk_e] - (chunk_start - seg_start[chunk_e]), 0, rows).astype(jnp.int32)
    last_used = jnp.max(jnp.where(chunk_rows > 0, jnp.arange(n_chunks), 0))
    chunk_e = jnp.where(chunk_rows > 0, chunk_e, chunk_e[last_used]).astype(jnp.int32)
    slots = slot.reshape(n_tok, TOP_K)
    pad = npr + TAIL - n_tok
    slot1 = jnp.pad(slots[:, 0], (0, pad))
    slot2 = jnp.pad(slots[:, 1], (0, pad))
    return tok_of_slot, chunk_e, chunk_rows, slot1, slot2


def _lam(lam_ref, lam0):
    lp = lam_ref[...]
    a = jnp.sum(lp[0:1, :] * lp[1:2, :], axis=-1, keepdims=True)
    b = jnp.sum(lp[2:3, :] * lp[3:4, :], axis=-1, keepdims=True)
    return jnp.exp(a) - jnp.exp(b) + lam0


def _attn_body(q_ref, k_ref, v_ref, sl_ref, lam_ref, sg_ref, o_ref, vt_sc, kp_sc, s_a, s_b, p_a, p_b, al_a, al_b,
               m_sc, l_sc, acc_sc, *, blk, lam0):
    i = pl.program_id(2)
    rows = 64

    @pl.when(i == 0)
    def _():
        lane = lax.broadcasted_iota(jnp.int32, (blk, HEAD_DIM), 1)
        for c in range(vt_sc.shape[0]):
            vt_sc[c] = v_ref[c * blk:(c + 1) * blk, :].T.astype(BF16)
            pos = c * blk + lax.broadcasted_iota(jnp.int32, (blk, HEAD_DIM), 0)
            hi = lax.shift_right_logical(pos, 6)
            lo = jnp.bitwise_and(pos, 63)
            digits = jnp.where(lane < 4, jnp.where(jnp.bitwise_and(lane, 1) == 0, hi, lo), 0)
            kp_sc[c] = digits.astype(F32).astype(BF16)

    pos_cols = jnp.broadcast_to(sl_ref[...], (blk, HEAD_DIM)).astype(BF16)
    q = [jnp.concatenate([q_ref[:, c * HEAD_DIM:(c + 1) * HEAD_DIM], pos_cols], axis=1) for c in range(2)]
    m_sc[...] = jnp.full_like(m_sc, NEG_INF)
    l_sc[...] = jnp.zeros_like(l_sc)
    acc_sc[...] = jnp.zeros_like(acc_sc)
    p_b[...] = jnp.zeros_like(p_b)
    al_b[...] = jnp.ones_like(al_b)

    def scores(j, s_ref):
        k = k_ref[pl.ds(pl.multiple_of(j * blk, blk), blk), :]
        kp = kp_sc[j]
        for c in range(2):
            kc = jnp.concatenate([k[:, c * HEAD_DIM:(c + 1) * HEAD_DIM], kp], axis=1)
            s_ref[c] = _dot_nt(kc, q[c])

    def softmax(j, s_ref, p_ref, al_ref, masked):
        def piece(c, r):
            s = s_ref[c, r * rows:(r + 1) * rows, :]
            if masked:
                ahead = (lax.broadcasted_iota(jnp.int32, s.shape, 0) - lax.broadcasted_iota(jnp.int32, s.shape, 1)
                         + (r * rows + (j - i) * blk))
                s = jnp.where(ahead <= 0, s, NEG_INF)
            return s

        for c in range(2):
            top = piece(c, 0).reshape(rows // 8, 8, blk).max(axis=0)
            for r in range(1, blk // rows):
                top = jnp.maximum(top, piece(c, r).reshape(rows // 8, 8, blk).max(axis=0))
            m_old = m_sc[c]
            m_new = jnp.maximum(m_old, jnp.max(top, axis=0, keepdims=True))
            alpha = jnp.exp2(m_old - m_new)
            total = jnp.zeros((8, blk), F32)
            for r in range(blk // rows):
                p = jnp.exp2(piece(c, r) - m_new)
                total = total + p.reshape(rows // 8, 8, blk).sum(axis=0)
                p_ref[c, r * rows:(r + 1) * rows, :] = p.astype(BF16)
            l_sc[c] = alpha * l_sc[c] + jnp.sum(total, axis=0, keepdims=True)
            m_sc[c] = m_new
            al_ref[c] = alpha

    def values(j, p_ref, al_ref):
        vt = vt_sc[j]
        for c in range(2):
            acc_sc[c] = al_ref[c] * acc_sc[c] + _dot(vt, p_ref[c])

    n_pairs = (i + 2) // 2
    scores(0, s_a)

    def pair(t, carry):
        values(jnp.maximum(2 * t - 1, 0), p_b, al_b)
        softmax(2 * t, s_a, p_a, al_a, False)
        scores(2 * t + 1, s_b)
        values(2 * t, p_a, al_a)
        softmax(2 * t + 1, s_b, p_b, al_b, False)
        scores(2 * t + 2, s_a)
        return carry

    lax.fori_loop(0, n_pairs - 1, pair, 0)

    j0 = 2 * n_pairs - 2
    values(jnp.maximum(j0 - 1, 0), p_b, al_b)
    softmax(j0, s_a, p_a, al_a, True)

    @pl.when(j0 < i)
    def _():
        scores(i, s_b)
        values(j0, p_a, al_a)
        softmax(i, s_b, p_b, al_b, True)
        values(i, p_b, al_b)

    @pl.when(j0 == i)
    def _():
        values(j0, p_a, al_a)

    o = acc_sc[0] * (1.0 / l_sc[0]) - _lam(lam_ref, lam0) * (acc_sc[1] * (1.0 / l_sc[1]))
    ms = jnp.mean(o * o, axis=0, keepdims=True)
    o = (o * lax.rsqrt(ms + EPS)).T
    o_ref[...] = (o * sg_ref[...] * (1.0 - lam0)).astype(o_ref.dtype)


def _attn_prompt(q, k, v, slopes, lam_p, sub_gain, n_batch, seq_len, blk, lam0):
    npr, d = q.shape
    assert seq_len <= 64 * 64 and blk % 64 == 0
    s_hi = slopes.astype(BF16).astype(F32)
    s_lo = (slopes - s_hi).astype(BF16).astype(F32)
    pos_mult = jnp.pad(jnp.stack([64.0 * s_hi, s_hi, 64.0 * s_lo, s_lo], axis=1), ((0, 0), (0, LANE - 4)))
    slopes = pos_mult[:, None, :]
    n_heads = d // V_DIM
    nq = seq_len // blk
    kv_spec = pl.BlockSpec((seq_len, V_DIM), lambda b, h, i: (b, h))
    return pl.pallas_call(
        functools.partial(_attn_body, blk=blk, lam0=lam0),
        grid=(n_batch, n_heads, nq),
        in_specs=[pl.BlockSpec((blk, V_DIM), lambda b, h, i: (b * nq + i, h)), kv_spec, kv_spec,
                  pl.BlockSpec((None, 1, LANE), lambda b, h, i: (h, 0, 0)),
                  pl.BlockSpec((4, HEAD_DIM), lambda b, h, i: (0, 0)),
                  pl.BlockSpec((1, V_DIM), lambda b, h, i: (0, 0))],
        out_specs=pl.BlockSpec((blk, V_DIM), lambda b, h, i: (b * nq + i, h)),
        out_shape=jax.ShapeDtypeStruct((npr, d), BF16),
        scratch_shapes=[pltpu.VMEM((nq, V_DIM, blk), BF16), pltpu.VMEM((nq, blk, HEAD_DIM), BF16),
                        pltpu.VMEM((2, blk, blk), F32), pltpu.VMEM((2, blk, blk), F32),
                        pltpu.VMEM((2, blk, blk), BF16), pltpu.VMEM((2, blk, blk), BF16),
                        pltpu.VMEM((2, 1, blk), F32), pltpu.VMEM((2, 1, blk), F32),
                        pltpu.VMEM((2, 1, blk), F32), pltpu.VMEM((2, 1, blk), F32),
                        pltpu.VMEM((2, V_DIM, blk), F32)],
        compiler_params=_params("arbitrary", "arbitrary", "arbitrary"), name="attn_prompt",
    )(q, k, v, slopes, lam_p, sub_gain)


def _attn_sample_body(pt, q_ref, kn_ref, vn_ref, sl_ref, lam_ref, sg_ref, *refs, pps, page, q_pos, lam0):
    k_refs = refs[:pps]
    v_refs = refs[pps:2 * pps]
    o_ref, m_sc, l_sc, acc_sc = refs[2 * pps:]
    p = pl.program_id(1)
    n_rows = q_ref.shape[0]
    n_heads = n_rows // 2
    cols = page * n_heads
    row = lax.broadcasted_iota(jnp.int32, (n_rows, cols), 0)
    col = lax.broadcasted_iota(jnp.int32, (n_rows, cols), 1)
    own = jnp.bitwise_and(col, n_heads - 1) == jnp.bitwise_and(row, n_heads - 1)
    key = lax.shift_right_logical(lax.broadcasted_iota(jnp.int32, (1, cols), 1), n_heads.bit_length() - 1)
    slope = sl_ref[:, 0:1]

    @pl.when(p == 0)
    def _():
        m_sc[...] = jnp.full_like(m_sc, NEG_INF)
        l_sc[...] = jnp.zeros_like(l_sc)
        acc_sc[...] = jnp.zeros_like(acc_sc)

    def update(s, pv_fn):
        m_old = m_sc[...]
        m_new = jnp.maximum(m_old, jnp.max(s, axis=-1, keepdims=True))
        alpha = jnp.exp2(m_old - m_new)
        pr = jnp.exp2(s - m_new)
        l_sc[...] = alpha * l_sc[...] + jnp.sum(pr, axis=-1, keepdims=True)
        acc_sc[...] = alpha * acc_sc[...] + pv_fn(pr)
        m_sc[...] = m_new

    qb = q_ref[...].astype(BF16)
    scores = []
    for i in range(pps):
        s_c = [_dot_nt(qb, k_refs[i][pl.ds(c, cols, stride=2), :].astype(BF16)) for c in range(2)]
        s = jnp.concatenate([s_c[0][:n_heads], s_c[1][n_heads:]], axis=0)
        dist = (q_pos - ((p * pps + i) * page + key)).astype(F32)
        scores.append(jnp.where(own, s - slope * dist, NEG_INF))

    def weighted_values(pr):
        pr = pr.astype(BF16)
        return sum(_dot(pr[:, i * cols:(i + 1) * cols], v_refs[i][...].astype(BF16)) for i in range(pps))

    update(jnp.concatenate(scores, axis=1), weighted_values)

    @pl.when(p == pl.num_programs(1) - 1)
    def _():
        s_new = jnp.sum(q_ref[...] * kn_ref[...], axis=-1, keepdims=True)
        vn = vn_ref[...]
        update(s_new, lambda pr: pr * jnp.concatenate([vn, vn], axis=0))
        o_hc = acc_sc[...] * (1.0 / l_sc[...])
        o = o_hc[:n_heads] - _lam(lam_ref, lam0) * o_hc[n_heads:]
        o_ref[...] = _rms(o, sg_ref[...]) * (1.0 - lam0)


def _attn_sample(q, k_new, v_new, cache_k, cache_v, page_table, slope_rows, lam_p, sub_gain, pps, lam0):
    n_seq, n_rows, _ = q.shape
    n_heads = n_rows // 2
    assert n_heads & (n_heads - 1) == 0
    n_pages = page_table.shape[1]
    n_pool, page = cache_k.shape[:2]
    ck = cache_k.reshape(n_pool * page * n_rows, HEAD_DIM)
    cv = cache_v.reshape(n_pool * page * n_heads, V_DIM)
    qspec = pl.BlockSpec((None, n_rows, HEAD_DIM), lambda b, p, pt: (b, 0, 0))
    vspec = pl.BlockSpec((None, n_heads, V_DIM), lambda b, p, pt: (b, 0, 0))

    def page_idx(b, p, pt, i):
        return pt[b * n_pages + p * pps + i]

    k_specs = [pl.BlockSpec((page * n_rows, HEAD_DIM), lambda b, p, pt, i=i: (page_idx(b, p, pt, i), 0))
               for i in range(pps)]
    v_specs = [pl.BlockSpec((page * n_heads, V_DIM), lambda b, p, pt, i=i: (page_idx(b, p, pt, i), 0))
               for i in range(pps)]
    return pl.pallas_call(
        functools.partial(_attn_sample_body, pps=pps, page=page, q_pos=n_pages * page, lam0=lam0),
        grid_spec=pltpu.PrefetchScalarGridSpec(
            num_scalar_prefetch=1, grid=(n_seq, n_pages // pps),
            in_specs=[qspec, qspec, vspec,
                      pl.BlockSpec((n_rows, LANE), lambda b, p, pt: (0, 0)),
                      pl.BlockSpec((4, HEAD_DIM), lambda b, p, pt: (0, 0)),
                      pl.BlockSpec((1, V_DIM), lambda b, p, pt: (0, 0))] + k_specs + v_specs,
            out_specs=vspec,
            scratch_shapes=[pltpu.VMEM((n_rows, 1), F32), pltpu.VMEM((n_rows, 1), F32),
                            pltpu.VMEM((n_rows, V_DIM), F32)]),
        out_shape=jax.ShapeDtypeStruct((n_seq, n_heads, V_DIM), F32),
        compiler_params=_params("arbitrary", "arbitrary"), name="attn_sample",
    )(page_table.reshape(-1), q, k_new, v_new, slope_rows, lam_p, sub_gain, *([ck] * pps), *([cv] * pps))


def _tiles(seq_len, d_model, d_ff, d_ff_e):
    tm = min(1024, seq_len)
    return dict(
        tm=tm,
        tr=min(512, tm),
        tn=min(1024, d_model),
        tn_conv=min(256, d_model),
        tf=256,
        moe_rows=tm + tm // 16,
        sub=(tm + tm // 16) // 2,
        gather_rows=min(256, tm),
        attn_blk=min(512, seq_len),
        pages_per_step=8,
    )


def kernel(x_prompt, x_sample, state_conv, cache_k, cache_v, page_table, mix_norm, ffn_norm, a_w_in, a_conv,
           a_w_out, kv_norm, w_kv, k_norm, b_w_q, b_q_norm, b_lam, b_subln, b_w_o, ffn_w_gate, ffn_w_up,
           ffn_w_down, moe_router, moe_w_gate, moe_w_up, moe_w_down):
    n_batch, seq_len, d = x_prompt.shape
    n_seq = x_sample.shape[0]
    assert x_sample.shape[1] == 1 and n_seq <= TAIL
    depth = mix_norm.shape[0]
    n_a = a_w_in.shape[0]
    n_heads = d // V_DIM
    n_exp = moe_router.shape[-1]
    k_width = n_heads * 2 * HEAD_DIM
    npr = n_batch * seq_len
    n_tok = npr + n_seq
    t = _tiles(seq_len, d, ffn_w_gate.shape[-1], moe_w_gate.shape[-1])
    tm, tn, tr = t["tm"], t["tn"], t["tr"]
    nb = npr // tm
    bps = seq_len // tm

    xm = x_prompt.reshape(npr, d)
    xt = jnp.pad(x_sample.reshape(n_seq, d), ((0, TAIL - n_seq), (0, 0)))

    slopes = jnp.exp2(-8.0 * jnp.arange(1, n_heads + 1, dtype=F32) / n_heads) * LOG2E
    slope_rows = jnp.broadcast_to(jnp.tile(slopes, 2)[:, None], (2 * n_heads, LANE))
    k_gain = jnp.tile(k_norm.reshape(-1), n_heads)

    def comp_major(a):
        return a[:n_seq].reshape(n_seq, n_heads, 2, HEAD_DIM).transpose(0, 2, 1, 3).reshape(n_seq, 2 * n_heads, HEAD_DIM)

    def pad_tail(a):
        return jnp.pad(a, ((0, TAIL - n_seq), (0, 0)))

    conv_prompt, conv_sample = [], []
    pending = None
    k_f32 = v_f32 = k_bf = v_bf = None
    for l in range(depth):
        if pending is None:
            hm, ht = _norm(xm, xt, mix_norm[l], tr)
        else:
            hm, ht, xm, xt = _norm(xm, xt, mix_norm[l], tr, add=pending)
            pending = None
        if l < n_a:
            prev0 = pad_tail(state_conv[l, :, 0, :])
            prev1 = pad_tail(state_conv[l, :, 1, :])
            gm, gt, ut = _conv_mixer(hm, ht, a_w_in, a_conv, l, prev0, prev1, tm, t["tn_conv"], seq_len)
            ut = ut.reshape(nb + 1, 8, d)
            conv_prompt.append(ut[bps - 1:nb:bps, 6:8, :])
            conv_sample.append(jnp.stack([state_conv[l, :, 1, :], ut[nb, :n_seq, :]], axis=1))
            xm, xt = _matmul(gm, gt, a_w_out, l, 0, d, tm, tn, [(F32, F32)], res=(xm, xt))
        else:
            j = l - n_a
            if k_f32 is None:
                nm, nt = _norm(xm, xt, kv_norm, tr)
                k_f32, kt_f32, k_bf, _ = _matmul(nm, nt, w_kv, 0, 0, k_width, tm, tn,
                                                 [(F32, F32), (BF16, BF16)], gain=k_gain)
                v_f32, vt_f32 = _matmul(nm, nt, w_kv, 0, k_width, d, tm, tn, [(F32, F32)])
            lam0 = 0.8 - 0.6 * math.exp(-0.3 * l)
            q_gain = jnp.tile(b_q_norm[j].reshape(-1), n_heads) * (HEAD_DIM ** -0.5 * LOG2E)
            qm, qt = _matmul(hm, ht, b_w_q, j, 0, k_width, tm, tn, [(BF16, F32)], gain=q_gain)
            sub_gain = b_subln[j].reshape(1, V_DIM)
            om = _attn_prompt(qm, k_bf, v_f32, slopes, b_lam[j], sub_gain, n_batch, seq_len,
                              t["attn_blk"], lam0)
            os_ = _attn_sample(comp_major(qt), comp_major(kt_f32), vt_f32[:n_seq].reshape(n_seq, n_heads, V_DIM),
                               cache_k, cache_v, page_table, slope_rows, b_lam[j], sub_gain,
                               t["pages_per_step"], lam0)
            ot = pad_tail(os_.reshape(n_seq, d)).astype(BF16)
            xm, xt = _matmul(om, ot, b_w_o, j, 0, d, tm, tn, [(F32, F32)], res=(xm, xt))
        i = l // 2
        if l % 2 == 0:
            h2m, h2t = _norm(xm, xt, ffn_norm[l], tr)
            pending = _ffn_dense(h2m, h2t, ffn_w_gate, ffn_w_up, ffn_w_down, i, tm, t["tf"])
        else:
            h2, rim, rit, rgm, rgt = _norm_router(xm, xt, ffn_norm[l], moe_router[i], tr)
            tok_of_slot, chunk_e, chunk_rows, slot1, slot2 = _route(rim, rit, n_tok, n_exp, t["moe_rows"])
            yb = _ffn_moe(h2, tok_of_slot, chunk_e, chunk_rows, moe_w_gate, moe_w_up, moe_w_down, i,
                          d, t["moe_rows"], t["tf"], t["sub"])
            valid = (jnp.arange(TAIL) < n_seq)[:, None]
            xm, xt = _combine(xm, xt, rgm, jnp.where(valid, rgt, 0.0), yb, slot1, slot2, t["gather_rows"])
    if pending is not None:
        xm, xt = xm + pending[0], xt + pending[1]

    y_prompt = xm.reshape(n_batch, seq_len, d)
    y_sample = xt[:n_seq].reshape(n_seq, 1, d)
    return (y_prompt, y_sample, jnp.stack(conv_prompt), jnp.stack(conv_sample),
            k_f32.reshape(n_batch, seq_len, n_heads, 2, HEAD_DIM),
            v_f32.reshape(n_batch, seq_len, n_heads, V_DIM),
            kt_f32[:n_seq].reshape(n_seq, 1, n_heads, 2, HEAD_DIM),
            vt_f32[:n_seq].reshape(n_seq, 1, n_heads, V_DIM))
```

```python
import functools
import math

import jax
import jax.numpy as jnp
from jax import lax
from jax.experimental import pallas as pl
from jax.experimental.pallas import tpu as pltpu

F32 = jnp.float32
BF16 = jnp.bfloat16
EPS = 1e-6
HEAD_DIM = 128
V_DIM = 2 * HEAD_DIM
TOP_K = 2
CONV_W = 3
LANE = 128
TAIL = 16
V7X_VMEM_BYTES = 64 * 1024 * 1024
VMEM_LIMIT = V7X_VMEM_BYTES - 8 * 1024 * 1024
NEG_INF = float("-inf")
LOG2E = 1.0 / math.log(2.0)


def _params(*sem):
    return pltpu.CompilerParams(dimension_semantics=sem, vmem_limit_bytes=VMEM_LIMIT)


def _dot(a, b):
    return jnp.dot(a, b, preferred_element_type=F32)


def _dot_nt(a, b):
    return lax.dot_general(a, b, (((1,), (1,)), ((), ())), preferred_element_type=F32)


def _rms(x, g):
    ms = jnp.mean(x * x, axis=-1, keepdims=True)
    return x * lax.rsqrt(ms + EPS) * g


def _group_rms(x, gsize):
    outs = []
    for g in range(x.shape[-1] // gsize):
        blk = x[:, g * gsize:(g + 1) * gsize]
        ms = jnp.mean(blk * blk, axis=-1, keepdims=True)
        outs.append(blk * lax.rsqrt(ms + EPS))
    return outs[0] if len(outs) == 1 else jnp.concatenate(outs, axis=-1)


def _norm_body(*refs, nb, has_add):
    if has_add:
        xm, xt, ym, yt, g, om, ot, sm, st = refs
    else:
        xm, xt, g, om, ot = refs
        ym = yt = sm = st = None
    m = pl.program_id(0)

    def run(x_ref, y_ref, o_ref, s_ref):
        x = x_ref[...]
        if has_add:
            x = x + y_ref[...]
            s_ref[...] = x
        o_ref[...] = _rms(x, g[...]).astype(o_ref.dtype)

    @pl.when(m < nb)
    def _():
        run(xm, ym, om, sm)

    @pl.when(m == nb)
    def _():
        run(xt, yt, ot, st)


def _norm(xm, xt, gain, tm, add=None, out_dtype=BF16):
    npr, d = xm.shape
    nb = npr // tm
    main = pl.BlockSpec((tm, d), lambda m: (jnp.minimum(m, nb - 1), 0))
    tail = pl.BlockSpec((TAIL, d), lambda m: (0, 0))
    gspec = pl.BlockSpec((1, d), lambda m: (0, 0))
    ins = [xm, xt]
    in_specs = [main, tail]
    out_shape = [jax.ShapeDtypeStruct((npr, d), out_dtype), jax.ShapeDtypeStruct((TAIL, d), out_dtype)]
    out_specs = [main, tail]
    if add is not None:
        ins += list(add)
        in_specs += [main, tail]
        out_shape += [jax.ShapeDtypeStruct((npr, d), F32), jax.ShapeDtypeStruct((TAIL, d), F32)]
        out_specs += [main, tail]
    ins.append(gain.reshape(1, d))
    in_specs.append(gspec)
    return pl.pallas_call(
        functools.partial(_norm_body, nb=nb, has_add=add is not None),
        grid=(nb + 1,), in_specs=in_specs, out_specs=out_specs, out_shape=out_shape,
        compiler_params=_params("arbitrary"), name="rmsnorm",
    )(*ins)


def _router_body(xm, xt, g, wr, h_all, im, it, gm, gt, *, nb, n_exp):
    m = pl.program_id(0)

    def run(x_ref, i_ref, g_ref):
        h = _rms(x_ref[...], g[...])
        rows, d = h.shape
        pk = d // (2 * LANE)
        bits = lax.bitcast_convert_type(h.astype(BF16).astype(F32), jnp.uint32)
        for j in range(pk):
            lo = lax.shift_right_logical(bits[:, j * LANE:(j + 1) * LANE], jnp.uint32(16))
            hi = bits[:, d // 2 + j * LANE:d // 2 + (j + 1) * LANE]
            h_all[pl.ds(j, rows, stride=pk), :] = jnp.bitwise_or(hi, lo)
        if rows * pk < h_all.shape[0]:
            h_all[rows * pk:, :] = jnp.zeros((h_all.shape[0] - rows * pk, LANE), jnp.uint32)
        logits = _dot(h.astype(BF16), wr[...].astype(BF16))
        lane = lax.broadcasted_iota(jnp.int32, logits.shape, 1)
        logits = jnp.where(lane < n_exp, logits, NEG_INF)
        v1 = jnp.max(logits, axis=-1, keepdims=True)
        i1 = jnp.min(jnp.where(logits == v1, lane, LANE), axis=-1, keepdims=True)
        rest = jnp.where(lane == i1, NEG_INF, logits)
        v2 = jnp.max(rest, axis=-1, keepdims=True)
        i2 = jnp.min(jnp.where(rest == v2, lane, LANE), axis=-1, keepdims=True)
        e = jnp.exp(v2 - v1)
        g1 = 1.0 / (1.0 + e)
        g2 = e / (1.0 + e)
        i_ref[...] = jnp.where(lane == 0, i1, jnp.where(lane == 1, i2, 0))
        g_ref[...] = jnp.where(lane == 0, g1, jnp.where(lane == 1, g2, 0.0))

    @pl.when(m < nb)
    def _():
        run(xm, im, gm)

    @pl.when(m == nb)
    def _():
        run(xt, it, gt)


def _norm_router(xm, xt, gain, w_router, tm):
    npr, d = xm.shape
    n_exp = w_router.shape[-1]
    nb = npr // tm
    wr = jnp.pad(w_router, ((0, 0), (0, LANE - n_exp)))
    main = pl.BlockSpec((tm, d), lambda m: (jnp.minimum(m, nb - 1), 0))
    tail = pl.BlockSpec((TAIL, d), lambda m: (0, 0))
    rmain = pl.BlockSpec((tm, LANE), lambda m: (jnp.minimum(m, nb - 1), 0))
    rtail = pl.BlockSpec((TAIL, LANE), lambda m: (0, 0))
    return pl.pallas_call(
        functools.partial(_router_body, nb=nb, n_exp=n_exp),
        grid=(nb + 1,),
        in_specs=[main, tail, pl.BlockSpec((1, d), lambda m: (0, 0)),
                  pl.BlockSpec((d, LANE), lambda m: (0, 0))],
        out_specs=[pl.BlockSpec((tm * (d // (2 * LANE)), LANE), lambda m: (m, 0)), rmain, rtail, rmain, rtail],
        out_shape=[jax.ShapeDtypeStruct(((npr + tm) * (d // (2 * LANE)), LANE), jnp.uint32),
                   jax.ShapeDtypeStruct((npr, LANE), jnp.int32), jax.ShapeDtypeStruct((TAIL, LANE), jnp.int32),
                   jax.ShapeDtypeStruct((npr, LANE), F32), jax.ShapeDtypeStruct((TAIL, LANE), F32)],
        compiler_params=_params("arbitrary"), name="rmsnorm_router",
    )(xm, xt, gain.reshape(1, d), wr)


def _mm_body(*refs, nb, has_res, has_gain, n_out):
    refs = list(refs)
    xm, xt, w = refs[:3]
    pos = 3
    rm = rt = gain = None
    if has_res:
        rm, rt = refs[pos:pos + 2]
        pos += 2
    if has_gain:
        gain = refs[pos]
        pos += 1
    outs = refs[pos:pos + 2 * n_out]
    wb = refs[pos + 2 * n_out]
    m = pl.program_id(1)

    @pl.when(m == 0)
    def _():
        wb[...] = w[...].astype(BF16)

    def run(x_ref, r_ref, o_refs):
        acc = _dot(x_ref[...], wb[...])
        if has_gain:
            acc = _group_rms(acc, HEAD_DIM) * gain[...]
        if has_res:
            acc = acc + r_ref[...]
        for o in o_refs:
            o[...] = acc.astype(o.dtype)

    @pl.when(m < nb)
    def _():
        run(xm, rm, outs[0::2])

    @pl.when(m == nb)
    def _():
        run(xt, rt, outs[1::2])


def _matmul(xm, xt, w, layer, col0, n_cols, tm, tn, out_dtypes, res=None, gain=None):
    npr, k = xm.shape
    nb = npr // tm
    cb = col0 // tn
    xmain = pl.BlockSpec((tm, k), lambda n, m: (jnp.minimum(m, nb - 1), 0))
    xtail = pl.BlockSpec((TAIL, k), lambda n, m: (0, 0))
    omain = pl.BlockSpec((tm, tn), lambda n, m: (jnp.minimum(m, nb - 1), n))
    otail = pl.BlockSpec((TAIL, tn), lambda n, m: (0, n))
    if w.ndim == 3:
        wspec = pl.BlockSpec((None, k, tn), lambda n, m: (layer, 0, cb + n))
    else:
        wspec = pl.BlockSpec((k, tn), lambda n, m: (0, cb + n))
    ins, in_specs = [xm, xt, w], [xmain, xtail, wspec]
    if res is not None:
        ins += list(res)
        in_specs += [omain, otail]
    if gain is not None:
        ins.append(gain.reshape(1, n_cols))
        in_specs.append(pl.BlockSpec((1, tn), lambda n, m: (0, n)))
    out_shape, out_specs = [], []
    for dm, dt in out_dtypes:
        out_shape += [jax.ShapeDtypeStruct((npr, n_cols), dm), jax.ShapeDtypeStruct((TAIL, n_cols), dt)]
        out_specs += [omain, otail]
    return pl.pallas_call(
        functools.partial(_mm_body, nb=nb, has_res=res is not None, has_gain=gain is not None,
                          n_out=len(out_dtypes)),
        grid=(n_cols // tn, nb + 1), in_specs=in_specs, out_specs=out_specs, out_shape=out_shape,
        scratch_shapes=[pltpu.VMEM((k, tn), BF16)],
        compiler_params=_params("arbitrary", "arbitrary"), name="matmul",
    )(*ins)


def _conv_body(xm, xt, w_b, w_c, w_v, cw, p0, p1, gm, gt, ut, wbs, wcs, wvs, carry, *, nb, bps):
    m = pl.program_id(1)

    @pl.when(m == 0)
    def _():
        wbs[...] = w_b[...].astype(BF16)
        wcs[...] = w_c[...].astype(BF16)
        wvs[...] = w_v[...].astype(BF16)

    w0, w1, w2 = cw[0:1, :], cw[1:2, :], cw[2:3, :]

    @pl.when(m < nb)
    def _():
        x = xm[...]
        u = _dot(x, wcs[...]) * _dot(x, wvs[...])
        b = _dot(x, wbs[...])

        @pl.when(m % bps == 0)
        def _():
            carry[...] = jnp.zeros_like(carry)

        prev1 = carry[7:8, :]
        prev2 = carry[6:7, :]
        row = lax.broadcasted_iota(jnp.int32, u.shape, 0)
        u1 = jnp.where(row == 0, prev1, pltpu.roll(u, 1, 0))
        u2 = jnp.where(row == 0, prev2, jnp.where(row == 1, prev1, pltpu.roll(u, 2, 0)))
        gm[...] = (b * (w0 * u2 + w1 * u1 + w2 * u)).astype(gm.dtype)
        tail_rows = u[u.shape[0] - 8:, :]
        carry[...] = tail_rows
        ut[...] = tail_rows

    @pl.when(m == nb)
    def _():
        x = xt[...]
        u = _dot(x, wcs[...]) * _dot(x, wvs[...])
        b = _dot(x, wbs[...])
        gt[...] = (b * (w0 * p0[...] + w1 * p1[...] + w2 * u)).astype(gt.dtype)
        ut[...] = u[0:8, :]


def _conv_mixer(hm, ht, w_in, conv_w, layer, prev0, prev1, tm, tn, seq_len):
    npr, d = hm.shape
    nb = npr // tm
    nn = d // tn
    xmain = pl.BlockSpec((tm, d), lambda n, m: (jnp.minimum(m, nb - 1), 0))
    xtail = pl.BlockSpec((TAIL, d), lambda n, m: (0, 0))
    omain = pl.BlockSpec((tm, tn), lambda n, m: (jnp.minimum(m, nb - 1), n))
    otail = pl.BlockSpec((TAIL, tn), lambda n, m: (0, n))

    def wspec(part):
        return pl.BlockSpec((None, d, tn), lambda n, m: (layer, 0, part * nn + n))

    return pl.pallas_call(
        functools.partial(_conv_body, nb=nb, bps=seq_len // tm),
        grid=(nn, nb + 1),
        in_specs=[xmain, xtail, wspec(0), wspec(1), wspec(2),
                  pl.BlockSpec((None, CONV_W, tn), lambda n, m: (layer, 0, n)), otail, otail],
        out_specs=[omain, otail, pl.BlockSpec((8, tn), lambda n, m: (m, n))],
        out_shape=[jax.ShapeDtypeStruct((npr, d), BF16), jax.ShapeDtypeStruct((TAIL, d), BF16),
                   jax.ShapeDtypeStruct(((nb + 1) * 8, d), F32)],
        scratch_shapes=[pltpu.VMEM((d, tn), BF16)] * 3 + [pltpu.VMEM((8, tn), F32)],
        compiler_params=_params("arbitrary", "arbitrary"), name="conv_mixer",
    )(hm, ht, w_in, w_in, w_in, conv_w, prev0, prev1)


def _ffn_rows(x, wgs, wus, wds):
    g = _dot(x, wgs[...])
    a = (g * jax.nn.sigmoid(g) * _dot(x, wus[...])).astype(BF16)
    return _dot(a, wds[...])


def _cast_weights(wg, wu, wd, wgs, wus, wds):
    wgs[...] = wg[...].astype(BF16)
    wus[...] = wu[...].astype(BF16)
    wds[...] = wd[...].astype(BF16)


def _ffn_dense_body(xm, xt, wg, wu, wd, om, ot, wgs, wus, wds, *, nb):
    c = pl.program_id(0)
    f = pl.program_id(1)

    @pl.when(c < nb)
    def _():
        @pl.when(f == 0)
        def _():
            om[...] = jnp.zeros_like(om)

        _cast_weights(wg, wu, wd, wgs, wus, wds)
        om[...] += _ffn_rows(xm[...], wgs, wus, wds)

    @pl.when(c == nb)
    def _():
        @pl.when(f == 0)
        def _():
            ot[...] = jnp.zeros_like(ot)

        _cast_weights(wg, wu, wd, wgs, wus, wds)
        ot[...] += _ffn_rows(xt[...], wgs, wus, wds)


def _ffn_dense(hm, ht, w_gate, w_up, w_down, layer, tm, tf):
    npr, d = hm.shape
    nb = npr // tm
    ff = w_gate.shape[-1]
    main = pl.BlockSpec((tm, d), lambda c, f: (jnp.minimum(c, nb - 1), 0))
    tail = pl.BlockSpec((TAIL, d), lambda c, f: (0, 0))
    return pl.pallas_call(
        functools.partial(_ffn_dense_body, nb=nb),
        grid=(nb + 1, ff // tf),
        in_specs=[main, tail,
                  pl.BlockSpec((None, d, tf), lambda c, f: (layer, 0, f)),
                  pl.BlockSpec((None, d, tf), lambda c, f: (layer, 0, f)),
                  pl.BlockSpec((None, tf, d), lambda c, f: (layer, f, 0))],
        out_specs=[main, tail],
        out_shape=[jax.ShapeDtypeStruct((npr, d), F32), jax.ShapeDtypeStruct((TAIL, d), F32)],
        scratch_shapes=[pltpu.VMEM((d, tf), BF16), pltpu.VMEM((d, tf), BF16), pltpu.VMEM((tf, d), BF16)],
        compiler_params=_params("arbitrary", "arbitrary"), name="ffn_dense",
    )(hm, ht, w_gate, w_up, w_down)


def _ffn_moe_body(ce, cn, tok, h, wg, wu, wd, o, gbuf, xsb, wgs, wus, wds, sem, *, sub, per_step):
    c = pl.program_id(0)
    f = pl.program_id(1)
    rows, d = o.shape
    pk = d // (2 * LANE)
    nv = cn[c]
    slot = lax.rem(c, 2)

    def issue(chunk, first, count, s):
        def body(i, carry):
            r = first + i
            t = tok[chunk * rows + r]
            pltpu.make_async_copy(h.at[pl.ds(t * pk, pk)], gbuf.at[s, pl.ds(r * pk, pk)], sem.at[s]).start()
            return carry

        lax.fori_loop(0, count, body, 0, unroll=8)

    @pl.when(jnp.logical_and(jnp.logical_and(c == 0, f == 0), nv > 0))
    def _():
        issue(0, 0, rows, 0)

    @pl.when(f == 0)
    def _():
        o[...] = jnp.zeros_like(o)

        @pl.when(nv > 0)
        def _():
            pltpu.make_async_copy(h.at[pl.ds(0, rows * pk)], gbuf.at[slot], sem.at[slot]).wait()
            for j in range(pk):
                w = gbuf[slot, pl.ds(j, rows, stride=pk), :]
                lo = lax.bitcast_convert_type(lax.shift_left(w, jnp.uint32(16)), F32)
                hi = lax.bitcast_convert_type(jnp.bitwise_and(w, jnp.uint32(0xFFFF0000)), F32)
                xsb[:, j * LANE:(j + 1) * LANE] = lo.astype(BF16)
                xsb[:, d // 2 + j * LANE:d // 2 + (j + 1) * LANE] = hi.astype(BF16)

    @pl.when(jnp.logical_and(f < rows // per_step, cn[c + 1] > 0))
    def _():
        issue(c + 1, f * per_step, per_step, 1 - slot)

    @pl.when(nv == rows)
    def _():
        _cast_weights(wg, wu, wd, wgs, wus, wds)
        o[...] += _ffn_rows(xsb[...], wgs, wus, wds)

    @pl.when(jnp.logical_and(nv > 0, nv < rows))
    def _():
        _cast_weights(wg, wu, wd, wgs, wus, wds)

        def step(i, carry):
            r = pl.ds(pl.multiple_of(i * sub, sub), sub)
            o[r, :] += _ffn_rows(xsb[r, :], wgs, wus, wds)
            return carry

        lax.fori_loop(0, (nv + sub - 1) // sub, step, 0)


def _ffn_moe(h, tok_of_slot, chunk_expert, chunk_rows, w_gate, w_up, w_down, layer, d, rows, tf, sub):
    p = tok_of_slot.shape[0]
    n_chunks = p // rows
    ff = w_gate.shape[-1]
    nf = ff // tf
    pk = d // (2 * LANE)
    per_step = next(q for q in range(-(-rows // nf), rows + 1) if rows % q == 0)

    def fsel(c, f, cn):
        return jnp.where(cn[c] > 0, f, nf - 1)

    return pl.pallas_call(
        functools.partial(_ffn_moe_body, sub=sub, per_step=per_step),
        grid_spec=pltpu.PrefetchScalarGridSpec(
            num_scalar_prefetch=3, grid=(n_chunks, nf),
            in_specs=[pl.BlockSpec(memory_space=pl.ANY),
                      pl.BlockSpec((None, None, d, tf),
                                   lambda c, f, ce, cn, tok: (layer, ce[c], 0, fsel(c, f, cn))),
                      pl.BlockSpec((None, None, d, tf),
                                   lambda c, f, ce, cn, tok: (layer, ce[c], 0, fsel(c, f, cn))),
                      pl.BlockSpec((None, None, tf, d),
                                   lambda c, f, ce, cn, tok: (layer, ce[c], fsel(c, f, cn), 0))],
            out_specs=pl.BlockSpec((rows, d), lambda c, f, ce, cn, tok: (c, 0)),
            scratch_shapes=[pltpu.VMEM((2, rows * pk, LANE), jnp.uint32), pltpu.VMEM((rows, d), BF16),
                            pltpu.VMEM((d, tf), BF16), pltpu.VMEM((d, tf), BF16), pltpu.VMEM((tf, d), BF16),
                            pltpu.SemaphoreType.DMA((2,))]),
        out_shape=jax.ShapeDtypeStruct((p, d), F32),
        compiler_params=_params("arbitrary", "arbitrary"), name="ffn_moe",
    )(chunk_expert, jnp.pad(chunk_rows, (0, 1)), tok_of_slot, h, w_gate, w_up, w_down)


def _row_copy(src, src_row, dst, dst_row, sem):
    return pltpu.make_async_copy(src.at[pl.ds(src_row, 1)], dst.at[pl.ds(dst_row, 1)], sem)


def _combine_body(s1, s2, xm, xt, gm, gt, yb, om, ot, abuf, bbuf, sem, *, nb, tb, npr):
    m = pl.program_id(0)
    slot = lax.rem(m, 2)

    def gather(base, rows, s):
        def issue(r, carry):
            _row_copy(yb, s1[base + r], abuf.at[s], r, sem.at[s]).start()
            _row_copy(yb, s2[base + r], bbuf.at[s], r, sem.at[s]).start()
            return carry

        lax.fori_loop(0, rows, issue, 0, unroll=8)

    def run(x_ref, g_ref, o_ref, rows):
        for buf in (abuf, bbuf):
            pltpu.make_async_copy(yb.at[pl.ds(0, rows)], buf.at[slot, pl.ds(0, rows)], sem.at[slot]).wait()
        g = g_ref[...]
        o_ref[...] = x_ref[...] + g[:, 0:1] * abuf[slot, 0:rows, :] + g[:, 1:2] * bbuf[slot, 0:rows, :]

    @pl.when(m == 0)
    def _():
        gather(0, tb, 0)

    @pl.when(m + 1 < nb)
    def _():
        gather((m + 1) * tb, tb, 1 - slot)

    @pl.when(m + 1 == nb)
    def _():
        gather(npr, TAIL, 1 - slot)

    @pl.when(m < nb)
    def _():
        run(xm, gm, om, tb)

    @pl.when(m == nb)
    def _():
        run(xt, gt, ot, TAIL)


def _combine(xm, xt, gm, gt, yb, slot1, slot2, tb):
    npr, d = xm.shape
    nb = npr // tb
    main = pl.BlockSpec((tb, d), lambda m, s1, s2: (jnp.minimum(m, nb - 1), 0))
    tail = pl.BlockSpec((TAIL, d), lambda m, s1, s2: (0, 0))
    gmain = pl.BlockSpec((tb, LANE), lambda m, s1, s2: (jnp.minimum(m, nb - 1), 0))
    gtail = pl.BlockSpec((TAIL, LANE), lambda m, s1, s2: (0, 0))
    return pl.pallas_call(
        functools.partial(_combine_body, nb=nb, tb=tb, npr=npr),
        grid_spec=pltpu.PrefetchScalarGridSpec(
            num_scalar_prefetch=2, grid=(nb + 1,),
            in_specs=[main, tail, gmain, gtail, pl.BlockSpec(memory_space=pl.ANY)],
            out_specs=[main, tail],
            scratch_shapes=[pltpu.VMEM((2, tb, d), F32), pltpu.VMEM((2, tb, d), F32),
                            pltpu.SemaphoreType.DMA((2,))]),
        out_shape=[jax.ShapeDtypeStruct((npr, d), F32), jax.ShapeDtypeStruct((TAIL, d), F32)],
        compiler_params=_params("arbitrary"), name="moe_combine",
    )(slot1, slot2, xm, xt, gm, gt, yb)


def _route(route_i, route_i_tail, n_tok, n_exp, rows):
    npr = route_i.shape[0]
    ids = jnp.concatenate([route_i[:, :TOP_K], route_i_tail[:n_tok - npr, :TOP_K]], axis=0)
    flat_e = ids.reshape(-1)
    onehot = (flat_e[:, None] == jnp.arange(n_exp, dtype=jnp.int32)[None, :]).astype(jnp.int32)
    before = jnp.cumsum(onehot, axis=0) - onehot
    rank = jnp.sum(before * onehot, axis=1)
    counts = jnp.sum(onehot, axis=0)
    seg = (counts + rows - 1) // rows * rows
    seg_end = jnp.cumsum(seg)
    seg_start = seg_end - seg
    slot = (seg_start[flat_e] + rank).astype(jnp.int32)
    n_chunks = (n_tok * TOP_K) // rows + n_exp
    tok = jnp.repeat(jnp.arange(n_tok, dtype=jnp.int32), TOP_K)
    tok_of_slot = jnp.zeros((n_chunks * rows,), jnp.int32).at[slot].set(tok)
    chunk_start = jnp.arange(n_chunks, dtype=jnp.int32) * rows
    chunk_e = jnp.minimum(jnp.searchsorted(seg_end, chunk_start, side="right"), n_exp - 1).astype(jnp.int32)
    chunk_rows = jnp.clip(counts[chunk_e] - (chunk_start - seg_start[chunk_e]), 0, rows).astype(jnp.int32)
    last_used = jnp.max(jnp.where(chunk_rows > 0, jnp.arange(n_chunks), 0))
    chunk_e = jnp.where(chunk_rows > 0, chunk_e, chunk_e[last_used]).astype(jnp.int32)
    slots = slot.reshape(n_tok, TOP_K)
    pad = npr + TAIL - n_tok
    slot1 = jnp.pad(slots[:, 0], (0, pad))
    slot2 = jnp.pad(slots[:, 1], (0, pad))
    return tok_of_slot, chunk_e, chunk_rows, slot1, slot2


def _lam(lam_ref, lam0):
    lp = lam_ref[...]
    a = jnp.sum(lp[0:1, :] * lp[1:2, :], axis=-1, keepdims=True)
    b = jnp.sum(lp[2:3, :] * lp[3:4, :], axis=-1, keepdims=True)
    return jnp.exp(a) - jnp.exp(b) + lam0


def _attn_body(q_ref, k_ref, v_ref, sl_ref, lam_ref, sg_ref, o_ref, vt_sc, kp_sc, s_a, s_b, p_a, p_b, al_a, al_b,
               m_sc, l_sc, acc_sc, *, blk, lam0):
    i = pl.program_id(2)
    rows = 64

    @pl.when(i == 0)
    def _():
        lane = lax.broadcasted_iota(jnp.int32, (blk, HEAD_DIM), 1)
        for c in range(vt_sc.shape[0]):
            vt_sc[c] = v_ref[c * blk:(c + 1) * blk, :].T.astype(BF16)
            pos = c * blk + lax.broadcasted_iota(jnp.int32, (blk, HEAD_DIM), 0)
            hi = lax.shift_right_logical(pos, 6)
            lo = jnp.bitwise_and(pos, 63)
            digits = jnp.where(lane < 4, jnp.where(jnp.bitwise_and(lane, 1) == 0, hi, lo), 0)
            kp_sc[c] = digits.astype(F32).astype(BF16)

    pos_cols = jnp.broadcast_to(sl_ref[...], (blk, HEAD_DIM)).astype(BF16)
    q = [jnp.concatenate([q_ref[:, c * HEAD_DIM:(c + 1) * HEAD_DIM], pos_cols], axis=1) for c in range(2)]
    m_sc[...] = jnp.full_like(m_sc, NEG_INF)
    l_sc[...] = jnp.zeros_like(l_sc)
    acc_sc[...] = jnp.zeros_like(acc_sc)
    p_b[...] = jnp.zeros_like(p_b)
    al_b[...] = jnp.ones_like(al_b)

    def scores(j, s_ref):
        k = k_ref[pl.ds(pl.multiple_of(j * blk, blk), blk), :]
        kp = kp_sc[j]
        for c in range(2):
            kc = jnp.concatenate([k[:, c * HEAD_DIM:(c + 1) * HEAD_DIM], kp], axis=1)
            s_ref[c] = _dot_nt(kc, q[c])

    def softmax(j, s_ref, p_ref, al_ref, masked):
        def piece(c, r):
            s = s_ref[c, r * rows:(r + 1) * rows, :]
            if masked:
                ahead = (lax.broadcasted_iota(jnp.int32, s.shape, 0) - lax.broadcasted_iota(jnp.int32, s.shape, 1)
                         + (r * rows + (j - i) * blk))
                s = jnp.where(ahead <= 0, s, NEG_INF)
            return s

        for c in range(2):
            top = piece(c, 0).reshape(rows // 8, 8, blk).max(axis=0)
            for r in range(1, blk // rows):
                top = jnp.maximum(top, piece(c, r).reshape(rows // 8, 8, blk).max(axis=0))
            m_old = m_sc[c]
            m_new = jnp.maximum(m_old, jnp.max(top, axis=0, keepdims=True))
            alpha = jnp.exp2(m_old - m_new)
            total = jnp.zeros((8, blk), F32)
            for r in range(blk // rows):
                p = jnp.exp2(piece(c, r) - m_new)
                total = total + p.reshape(rows // 8, 8, blk).sum(axis=0)
                p_ref[c, r * rows:(r + 1) * rows, :] = p.astype(BF16)
            l_sc[c] = alpha * l_sc[c] + jnp.sum(total, axis=0, keepdims=True)
            m_sc[c] = m_new
            al_ref[c] = alpha

    def values(j, p_ref, al_ref):
        vt = vt_sc[j]
        for c in range(2):
            acc_sc[c] = al_ref[c] * acc_sc[c] + _dot(vt, p_ref[c])

    n_pairs = (i + 2) // 2
    scores(0, s_a)

    def pair(t, carry):
        values(jnp.maximum(2 * t - 1, 0), p_b, al_b)
        softmax(2 * t, s_a, p_a, al_a, False)
        scores(2 * t + 1, s_b)
        values(2 * t, p_a, al_a)
        softmax(2 * t + 1, s_b, p_b, al_b, False)
        scores(2 * t + 2, s_a)
        return carry

    lax.fori_loop(0, n_pairs - 1, pair, 0)

    j0 = 2 * n_pairs - 2
    values(jnp.maximum(j0 - 1, 0), p_b, al_b)
    softmax(j0, s_a, p_a, al_a, True)

    @pl.when(j0 < i)
    def _():
        scores(i, s_b)
        values(j0, p_a, al_a)
        softmax(i, s_b, p_b, al_b, True)
        values(i, p_b, al_b)

    @pl.when(j0 == i)
    def _():
        values(j0, p_a, al_a)

    o = acc_sc[0] * (1.0 / l_sc[0]) - _lam(lam_ref, lam0) * (acc_sc[1] * (1.0 / l_sc[1]))
    ms = jnp.mean(o * o, axis=0, keepdims=True)
    o = (o * lax.rsqrt(ms + EPS)).T
    o_ref[...] = (o * sg_ref[...] * (1.0 - lam0)).astype(o_ref.dtype)


def _attn_prompt(q, k, v, slopes, lam_p, sub_gain, n_batch, seq_len, blk, lam0):
    npr, d = q.shape
    assert seq_len <= 64 * 64 and blk % 64 == 0
    s_hi = slopes.astype(BF16).astype(F32)
    s_lo = (slopes - s_hi).astype(BF16).astype(F32)
    pos_mult = jnp.pad(jnp.stack([64.0 * s_hi, s_hi, 64.0 * s_lo, s_lo], axis=1), ((0, 0), (0, LANE - 4)))
    slopes = pos_mult[:, None, :]
    n_heads = d // V_DIM
    nq = seq_len // blk
    kv_spec = pl.BlockSpec((seq_len, V_DIM), lambda b, h, i: (b, h))
    return pl.pallas_call(
        functools.partial(_attn_body, blk=blk, lam0=lam0),
        grid=(n_batch, n_heads, nq),
        in_specs=[pl.BlockSpec((blk, V_DIM), lambda b, h, i: (b * nq + i, h)), kv_spec, kv_spec,
                  pl.BlockSpec((None, 1, LANE), lambda b, h, i: (h, 0, 0)),
                  pl.BlockSpec((4, HEAD_DIM), lambda b, h, i: (0, 0)),
                  pl.BlockSpec((1, V_DIM), lambda b, h, i: (0, 0))],
        out_specs=pl.BlockSpec((blk, V_DIM), lambda b, h, i: (b * nq + i, h)),
        out_shape=jax.ShapeDtypeStruct((npr, d), BF16),
        scratch_shapes=[pltpu.VMEM((nq, V_DIM, blk), BF16), pltpu.VMEM((nq, blk, HEAD_DIM), BF16),
                        pltpu.VMEM((2, blk, blk), F32), pltpu.VMEM((2, blk, blk), F32),
                        pltpu.VMEM((2, blk, blk), BF16), pltpu.VMEM((2, blk, blk), BF16),
                        pltpu.VMEM((2, 1, blk), F32), pltpu.VMEM((2, 1, blk), F32),
                        pltpu.VMEM((2, 1, blk), F32), pltpu.VMEM((2, 1, blk), F32),
                        pltpu.VMEM((2, V_DIM, blk), F32)],
        compiler_params=_params("arbitrary", "arbitrary", "arbitrary"), name="attn_prompt",
    )(q, k, v, slopes, lam_p, sub_gain)


def _attn_sample_body(pt, q_ref, kn_ref, vn_ref, sl_ref, lam_ref, sg_ref, *refs, pps, page, q_pos, lam0):
    k_refs = refs[:pps]
    v_refs = refs[pps:2 * pps]
    o_ref, m_sc, l_sc, acc_sc = refs[2 * pps:]
    p = pl.program_id(1)
    n_rows = q_ref.shape[0]
    n_heads = n_rows // 2
    cols = page * n_heads
    row = lax.broadcasted_iota(jnp.int32, (n_rows, cols), 0)
    col = lax.broadcasted_iota(jnp.int32, (n_rows, cols), 1)
    own = jnp.bitwise_and(col, n_heads - 1) == jnp.bitwise_and(row, n_heads - 1)
    key = lax.shift_right_logical(lax.broadcasted_iota(jnp.int32, (1, cols), 1), n_heads.bit_length() - 1)
    slope = sl_ref[:, 0:1]

    @pl.when(p == 0)
    def _():
        m_sc[...] = jnp.full_like(m_sc, NEG_INF)
        l_sc[...] = jnp.zeros_like(l_sc)
        acc_sc[...] = jnp.zeros_like(acc_sc)

    def update(s, pv_fn):
        m_old = m_sc[...]
        m_new = jnp.maximum(m_old, jnp.max(s, axis=-1, keepdims=True))
        alpha = jnp.exp2(m_old - m_new)
        pr = jnp.exp2(s - m_new)
        l_sc[...] = alpha * l_sc[...] + jnp.sum(pr, axis=-1, keepdims=True)
        acc_sc[...] = alpha * acc_sc[...] + pv_fn(pr)
        m_sc[...] = m_new

    qb = q_ref[...].astype(BF16)
    scores = []
    for i in range(pps):
        s_c = [_dot_nt(qb, k_refs[i][pl.ds(c, cols, stride=2), :].astype(BF16)) for c in range(2)]
        s = jnp.concatenate([s_c[0][:n_heads], s_c[1][n_heads:]], axis=0)
        dist = (q_pos - ((p * pps + i) * page + key)).astype(F32)
        scores.append(jnp.where(own, s - slope * dist, NEG_INF))

    def weighted_values(pr):
        pr = pr.astype(BF16)
        return sum(_dot(pr[:, i * cols:(i + 1) * cols], v_refs[i][...].astype(BF16)) for i in range(pps))

    update(jnp.concatenate(scores, axis=1), weighted_values)

    @pl.when(p == pl.num_programs(1) - 1)
    def _():
        s_new = jnp.sum(q_ref[...] * kn_ref[...], axis=-1, keepdims=True)
        vn = vn_ref[...]
        update(s_new, lambda pr: pr * jnp.concatenate([vn, vn], axis=0))
        o_hc = acc_sc[...] * (1.0 / l_sc[...])
        o = o_hc[:n_heads] - _lam(lam_ref, lam0) * o_hc[n_heads:]
        o_ref[...] = _rms(o, sg_ref[...]) * (1.0 - lam0)


def _attn_sample(q, k_new, v_new, cache_k, cache_v, page_table, slope_rows, lam_p, sub_gain, pps, lam0):
    n_seq, n_rows, _ = q.shape
    n_heads = n_rows // 2
    assert n_heads & (n_heads - 1) == 0
    n_pages = page_table.shape[1]
    n_pool, page = cache_k.shape[:2]
    ck = cache_k.reshape(n_pool * page * n_rows, HEAD_DIM)
    cv = cache_v.reshape(n_pool * page * n_heads, V_DIM)
    qspec = pl.BlockSpec((None, n_rows, HEAD_DIM), lambda b, p, pt: (b, 0, 0))
    vspec = pl.BlockSpec((None, n_heads, V_DIM), lambda b, p, pt: (b, 0, 0))

    def page_idx(b, p, pt, i):
        return pt[b * n_pages + p * pps + i]

    k_specs = [pl.BlockSpec((page * n_rows, HEAD_DIM), lambda b, p, pt, i=i: (page_idx(b, p, pt, i), 0))
               for i in range(pps)]
    v_specs = [pl.BlockSpec((page * n_heads, V_DIM), lambda b, p, pt, i=i: (page_idx(b, p, pt, i), 0))
               for i in range(pps)]
    return pl.pallas_call(
        functools.partial(_attn_sample_body, pps=pps, page=page, q_pos=n_pages * page, lam0=lam0),
        grid_spec=pltpu.PrefetchScalarGridSpec(
            num_scalar_prefetch=1, grid=(n_seq, n_pages // pps),
            in_specs=[qspec, qspec, vspec,
                      pl.BlockSpec((n_rows, LANE), lambda b, p, pt: (0, 0)),
                      pl.BlockSpec((4, HEAD_DIM), lambda b, p, pt: (0, 0)),
                      pl.BlockSpec((1, V_DIM), lambda b, p, pt: (0, 0))] + k_specs + v_specs,
            out_specs=vspec,
            scratch_shapes=[pltpu.VMEM((n_rows, 1), F32), pltpu.VMEM((n_rows, 1), F32),
                            pltpu.VMEM((n_rows, V_DIM), F32)]),
        out_shape=jax.ShapeDtypeStruct((n_seq, n_heads, V_DIM), F32),
        compiler_params=_params("arbitrary", "arbitrary"), name="attn_sample",
    )(page_table.reshape(-1), q, k_new, v_new, slope_rows, lam_p, sub_gain, *([ck] * pps), *([cv] * pps))


def _tiles(seq_len, d_model, d_ff, d_ff_e):
    tm = min(1024, seq_len)
    return dict(
        tm=tm,
        tr=min(512, tm),
        tn=min(1024, d_model),
        tn_conv=min(256, d_model),
        tf=256,
        moe_rows=tm + tm // 16,
        sub=(tm + tm // 16) // 2,
        gather_rows=min(256, tm),
        attn_blk=min(512, seq_len),
        pages_per_step=8,
    )


def kernel(x_prompt, x_sample, state_conv, cache_k, cache_v, page_table, mix_norm, ffn_norm, a_w_in, a_conv,
           a_w_out, kv_norm, w_kv, k_norm, b_w_q, b_q_norm, b_lam, b_subln, b_w_o, ffn_w_gate, ffn_w_up,
           ffn_w_down, moe_router, moe_w_gate, moe_w_up, moe_w_down):
    n_batch, seq_len, d = x_prompt.shape
    n_seq = x_sample.shape[0]
    assert x_sample.shape[1] == 1 and n_seq <= TAIL
    depth = mix_norm.shape[0]
    n_a = a_w_in.shape[0]
    n_heads = d // V_DIM
    n_exp = moe_router.shape[-1]
    k_width = n_heads * 2 * HEAD_DIM
    npr = n_batch * seq_len
    n_tok = npr + n_seq
    t = _tiles(seq_len, d, ffn_w_gate.shape[-1], moe_w_gate.shape[-1])
    tm, tn, tr = t["tm"], t["tn"], t["tr"]
    nb = npr // tm
    bps = seq_len // tm

    xm = x_prompt.reshape(npr, d)
    xt = jnp.pad(x_sample.reshape(n_seq, d), ((0, TAIL - n_seq), (0, 0)))

    slopes = jnp.exp2(-8.0 * jnp.arange(1, n_heads + 1, dtype=F32) / n_heads) * LOG2E
    slope_rows = jnp.broadcast_to(jnp.tile(slopes, 2)[:, None], (2 * n_heads, LANE))
    k_gain = jnp.tile(k_norm.reshape(-1), n_heads)

    def comp_major(a):
        return a[:n_seq].reshape(n_seq, n_heads, 2, HEAD_DIM).transpose(0, 2, 1, 3).reshape(n_seq, 2 * n_heads, HEAD_DIM)

    def pad_tail(a):
        return jnp.pad(a, ((0, TAIL - n_seq), (0, 0)))

    conv_prompt, conv_sample = [], []
    pending = None
    k_f32 = v_f32 = k_bf = v_bf = None
    for l in range(depth):
        if pending is None:
            hm, ht = _norm(xm, xt, mix_norm[l], tr)
        else:
            hm, ht, xm, xt = _norm(xm, xt, mix_norm[l], tr, add=pending)
            pending = None
        if l < n_a:
            prev0 = pad_tail(state_conv[l, :, 0, :])
            prev1 = pad_tail(state_conv[l, :, 1, :])
            gm, gt, ut = _conv_mixer(hm, ht, a_w_in, a_conv, l, prev0, prev1, tm, t["tn_conv"], seq_len)
            ut = ut.reshape(nb + 1, 8, d)
            conv_prompt.append(ut[bps - 1:nb:bps, 6:8, :])
            conv_sample.append(jnp.stack([state_conv[l, :, 1, :], ut[nb, :n_seq, :]], axis=1))
            xm, xt = _matmul(gm, gt, a_w_out, l, 0, d, tm, tn, [(F32, F32)], res=(xm, xt))
        else:
            j = l - n_a
            if k_f32 is None:
                nm, nt = _norm(xm, xt, kv_norm, tr)
                k_f32, kt_f32, k_bf, _ = _matmul(nm, nt, w_kv, 0, 0, k_width, tm, tn,
                                                 [(F32, F32), (BF16, BF16)], gain=k_gain)
                v_f32, vt_f32 = _matmul(nm, nt, w_kv, 0, k_width, d, tm, tn, [(F32, F32)])
            lam0 = 0.8 - 0.6 * math.exp(-0.3 * l)
            q_gain = jnp.tile(b_q_norm[j].reshape(-1), n_heads) * (HEAD_DIM ** -0.5 * LOG2E)
            qm, qt = _matmul(hm, ht, b_w_q, j, 0, k_width, tm, tn, [(BF16, F32)], gain=q_gain)
            sub_gain = b_subln[j].reshape(1, V_DIM)
            om = _attn_prompt(qm, k_bf, v_f32, slopes, b_lam[j], sub_gain, n_batch, seq_len,
                              t["attn_blk"], lam0)
            os_ = _attn_sample(comp_major(qt), comp_major(kt_f32), vt_f32[:n_seq].reshape(n_seq, n_heads, V_DIM),
                               cache_k, cache_v, page_table, slope_rows, b_lam[j], sub_gain,
                               t["pages_per_step"], lam0)
            ot = pad_tail(os_.reshape(n_seq, d)).astype(BF16)
            xm, xt = _matmul(om, ot, b_w_o, j, 0, d, tm, tn, [(F32, F32)], res=(xm, xt))
        i = l // 2
        if l % 2 == 0:
            h2m, h2t = _norm(xm, xt, ffn_norm[l], tr)
            pending = _ffn_dense(h2m, h2t, ffn_w_gate, ffn_w_up, ffn_w_down, i, tm, t["tf"])
        else:
            h2, rim, rit, rgm, rgt = _norm_router(xm, xt, ffn_norm[l], moe_router[i], tr)
            tok_of_slot, chunk_e, chunk_rows, slot1, slot2 = _route(rim, rit, n_tok, n_exp, t["moe_rows"])
            yb = _ffn_moe(h2, tok_of_slot, chunk_e, chunk_rows, moe_w_gate, moe_w_up, moe_w_down, i,
                          d, t["moe_rows"], t["tf"], t["sub"])
            valid = (jnp.arange(TAIL) < n_seq)[:, None]
            xm, xt = _combine(xm, xt, rgm, jnp.where(valid, rgt, 0.0), yb, slot1, slot2, t["gather_rows"])
    if pending is not None:
        xm, xt = xm + pending[0], xt + pending[1]

    y_prompt = xm.reshape(n_batch, seq_len, d)
    y_sample = xt[:n_seq].reshape(n_seq, 1, d)
    return (y_prompt, y_sample, jnp.stack(conv_prompt), jnp.stack(conv_sample),
            k_f32.reshape(n_batch, seq_len, n_heads, 2, HEAD_DIM),
            v_f32.reshape(n_batch, seq_len, n_heads, V_DIM),
            kt_f32[:n_seq].reshape(n_seq, 1, n_heads, 2, HEAD_DIM),
            vt_f32[:n_seq].reshape(n_seq, 1, n_heads, V_DIM))
```

```python
import functools
import math

import jax
import jax.numpy as jnp
from jax import lax
from jax.experimental import pallas as pl
from jax.experimental.pallas import tpu as pltpu

F32 = jnp.float32
BF16 = jnp.bfloat16
EPS = 1e-6
HEAD_DIM = 128
V_DIM = 2 * HEAD_DIM
TOP_K = 2
CONV_W = 3
LANE = 128
TAIL = 16
V7X_VMEM_BYTES = 64 * 1024 * 1024
VMEM_LIMIT = V7X_VMEM_BYTES - 8 * 1024 * 1024
NEG_INF = float("-inf")
LOG2E = 1.0 / math.log(2.0)


def _params(*sem):
    return pltpu.CompilerParams(dimension_semantics=sem, vmem_limit_bytes=VMEM_LIMIT)


def _dot(a, b):
    return jnp.dot(a, b, preferred_element_type=F32)


def _dot_nt(a, b):
    return lax.dot_general(a, b, (((1,), (1,)), ((), ())), preferred_element_type=F32)


def _rms(x, g):
    ms = jnp.mean(x * x, axis=-1, keepdims=True)
    return x * lax.rsqrt(ms + EPS) * g


def _group_rms(x, gsize):
    outs = []
    for g in range(x.shape[-1] // gsize):
        blk = x[:, g * gsize:(g + 1) * gsize]
        ms = jnp.mean(blk * blk, axis=-1, keepdims=True)
        outs.append(blk * lax.rsqrt(ms + EPS))
    return outs[0] if len(outs) == 1 else jnp.concatenate(outs, axis=-1)


def _norm_body(*refs, nb, has_add):
    if has_add:
        xm, xt, ym, yt, g, om, ot, sm, st = refs
    else:
        xm, xt, g, om, ot = refs
        ym = yt = sm = st = None
    m = pl.program_id(0)

    def run(x_ref, y_ref, o_ref, s_ref):
        x = x_ref[...]
        if has_add:
            x = x + y_ref[...]
            s_ref[...] = x
        o_ref[...] = _rms(x, g[...]).astype(o_ref.dtype)

    @pl.when(m < nb)
    def _():
        run(xm, ym, om, sm)

    @pl.when(m == nb)
    def _():
        run(xt, yt, ot, st)


def _norm(xm, xt, gain, tm, add=None, out_dtype=BF16):
    npr, d = xm.shape
    nb = npr // tm
    main = pl.BlockSpec((tm, d), lambda m: (jnp.minimum(m, nb - 1), 0))
    tail = pl.BlockSpec((TAIL, d), lambda m: (0, 0))
    gspec = pl.BlockSpec((1, d), lambda m: (0, 0))
    ins = [xm, xt]
    in_specs = [main, tail]
    out_shape = [jax.ShapeDtypeStruct((npr, d), out_dtype), jax.ShapeDtypeStruct((TAIL, d), out_dtype)]
    out_specs = [main, tail]
    if add is not None:
        ins += list(add)
        in_specs += [main, tail]
        out_shape += [jax.ShapeDtypeStruct((npr, d), F32), jax.ShapeDtypeStruct((TAIL, d), F32)]
        out_specs += [main, tail]
    ins.append(gain.reshape(1, d))
    in_specs.append(gspec)
    return pl.pallas_call(
        functools.partial(_norm_body, nb=nb, has_add=add is not None),
        grid=(nb + 1,), in_specs=in_specs, out_specs=out_specs, out_shape=out_shape,
        compiler_params=_params("arbitrary"), name="rmsnorm",
    )(*ins)


def _router_body(xm, xt, g, wr, h_all, im, it, gm, gt, *, nb, n_exp):
    m = pl.program_id(0)

    def run(x_ref, i_ref, g_ref):
        h = _rms(x_ref[...], g[...])
        rows, d = h.shape
        pk = d // (2 * LANE)
        bits = lax.bitcast_convert_type(h.astype(BF16).astype(F32), jnp.uint32)
        for j in range(pk):
            lo = lax.shift_right_logical(bits[:, j * LANE:(j + 1) * LANE], jnp.uint32(16))
            hi = bits[:, d // 2 + j * LANE:d // 2 + (j + 1) * LANE]
            h_all[pl.ds(j, rows, stride=pk), :] = jnp.bitwise_or(hi, lo)
        if rows * pk < h_all.shape[0]:
            h_all[rows * pk:, :] = jnp.zeros((h_all.shape[0] - rows * pk, LANE), jnp.uint32)
        logits = _dot(h.astype(BF16), wr[...].astype(BF16))
        lane = lax.broadcasted_iota(jnp.int32, logits.shape, 1)
        logits = jnp.where(lane < n_exp, logits, NEG_INF)
        v1 = jnp.max(logits, axis=-1, keepdims=True)
        i1 = jnp.min(jnp.where(logits == v1, lane, LANE), axis=-1, keepdims=True)
        rest = jnp.where(lane == i1, NEG_INF, logits)
        v2 = jnp.max(rest, axis=-1, keepdims=True)
        i2 = jnp.min(jnp.where(rest == v2, lane, LANE), axis=-1, keepdims=True)
        e = jnp.exp(v2 - v1)
        g1 = 1.0 / (1.0 + e)
        g2 = e / (1.0 + e)
        i_ref[...] = jnp.where(lane == 0, i1, jnp.where(lane == 1, i2, 0))
        g_ref[...] = jnp.where(lane == 0, g1, jnp.where(lane == 1, g2, 0.0))

    @pl.when(m < nb)
    def _():
        run(xm, im, gm)

    @pl.when(m == nb)
    def _():
        run(xt, it, gt)


def _norm_router(xm, xt, gain, w_router, tm):
    npr, d = xm.shape
    n_exp = w_router.shape[-1]
    nb = npr // tm
    wr = jnp.pad(w_router, ((0, 0), (0, LANE - n_exp)))
    main = pl.BlockSpec((tm, d), lambda m: (jnp.minimum(m, nb - 1), 0))
    tail = pl.BlockSpec((TAIL, d), lambda m: (0, 0))
    rmain = pl.BlockSpec((tm, LANE), lambda m: (jnp.minimum(m, nb - 1), 0))
    rtail = pl.BlockSpec((TAIL, LANE), lambda m: (0, 0))
    return pl.pallas_call(
        functools.partial(_router_body, nb=nb, n_exp=n_exp),
        grid=(nb + 1,),
        in_specs=[main, tail, pl.BlockSpec((1, d), lambda m: (0, 0)),
                  pl.BlockSpec((d, LANE), lambda m: (0, 0))],
        out_specs=[pl.BlockSpec((tm * (d // (2 * LANE)), LANE), lambda m: (m, 0)), rmain, rtail, rmain, rtail],
        out_shape=[jax.ShapeDtypeStruct(((npr + tm) * (d // (2 * LANE)), LANE), jnp.uint32),
                   jax.ShapeDtypeStruct((npr, LANE), jnp.int32), jax.ShapeDtypeStruct((TAIL, LANE), jnp.int32),
                   jax.ShapeDtypeStruct((npr, LANE), F32), jax.ShapeDtypeStruct((TAIL, LANE), F32)],
        compiler_params=_params("arbitrary"), name="rmsnorm_router",
    )(xm, xt, gain.reshape(1, d), wr)


def _mm_body(*refs, nb, has_res, has_gain, n_out):
    refs = list(refs)
    xm, xt, w = refs[:3]
    pos = 3
    rm = rt = gain = None
    if has_res:
        rm, rt = refs[pos:pos + 2]
        pos += 2
    if has_gain:
        gain = refs[pos]
        pos += 1
    outs = refs[pos:pos + 2 * n_out]
    wb = refs[pos + 2 * n_out]
    m = pl.program_id(1)

    @pl.when(m == 0)
    def _():
        wb[...] = w[...].astype(BF16)

    def run(x_ref, r_ref, o_refs):
        acc = _dot(x_ref[...], wb[...])
        if has_gain:
            acc = _group_rms(acc, HEAD_DIM) * gain[...]
        if has_res:
            acc = acc + r_ref[...]
        for o in o_refs:
            o[...] = acc.astype(o.dtype)

    @pl.when(m < nb)
    def _():
        run(xm, rm, outs[0::2])

    @pl.when(m == nb)
    def _():
        run(xt, rt, outs[1::2])


def _matmul(xm, xt, w, layer, col0, n_cols, tm, tn, out_dtypes, res=None, gain=None):
    npr, k = xm.shape
    nb = npr // tm
    cb = col0 // tn
    xmain = pl.BlockSpec((tm, k), lambda n, m: (jnp.minimum(m, nb - 1), 0))
    xtail = pl.BlockSpec((TAIL, k), lambda n, m: (0, 0))
    omain = pl.BlockSpec((tm, tn), lambda n, m: (jnp.minimum(m, nb - 1), n))
    otail = pl.BlockSpec((TAIL, tn), lambda n, m: (0, n))
    if w.ndim == 3:
        wspec = pl.BlockSpec((None, k, tn), lambda n, m: (layer, 0, cb + n))
    else:
        wspec = pl.BlockSpec((k, tn), lambda n, m: (0, cb + n))
    ins, in_specs = [xm, xt, w], [xmain, xtail, wspec]
    if res is not None:
        ins += list(res)
        in_specs += [omain, otail]
    if gain is not None:
        ins.append(gain.reshape(1, n_cols))
        in_specs.append(pl.BlockSpec((1, tn), lambda n, m: (0, n)))
    out_shape, out_specs = [], []
    for dm, dt in out_dtypes:
        out_shape += [jax.ShapeDtypeStruct((npr, n_cols), dm), jax.ShapeDtypeStruct((TAIL, n_cols), dt)]
        out_specs += [omain, otail]
    return pl.pallas_call(
        functools.partial(_mm_body, nb=nb, has_res=res is not None, has_gain=gain is not None,
                          n_out=len(out_dtypes)),
        grid=(n_cols // tn, nb + 1), in_specs=in_specs, out_specs=out_specs, out_shape=out_shape,
        scratch_shapes=[pltpu.VMEM((k, tn), BF16)],
        compiler_params=_params("arbitrary", "arbitrary"), name="matmul",
    )(*ins)


def _conv_body(xm, xt, w_b, w_c, w_v, cw, p0, p1, gm, gt, ut, wbs, wcs, wvs, carry, *, nb, bps):
    m = pl.program_id(1)

    @pl.when(m == 0)
    def _():
        wbs[...] = w_b[...].astype(BF16)
        wcs[...] = w_c[...].astype(BF16)
        wvs[...] = w_v[...].astype(BF16)

    w0, w1, w2 = cw[0:1, :], cw[1:2, :], cw[2:3, :]

    @pl.when(m < nb)
    def _():
        x = xm[...]
        u = _dot(x, wcs[...]) * _dot(x, wvs[...])
        b = _dot(x, wbs[...])

        @pl.when(m % bps == 0)
        def _():
            carry[...] = jnp.zeros_like(carry)

        prev1 = carry[7:8, :]
        prev2 = carry[6:7, :]
        row = lax.broadcasted_iota(jnp.int32, u.shape, 0)
        u1 = jnp.where(row == 0, prev1, pltpu.roll(u, 1, 0))
        u2 = jnp.where(row == 0, prev2, jnp.where(row == 1, prev1, pltpu.roll(u, 2, 0)))
        gm[...] = (b * (w0 * u2 + w1 * u1 + w2 * u)).astype(gm.dtype)
        tail_rows = u[u.shape[0] - 8:, :]
        carry[...] = tail_rows
        ut[...] = tail_rows

    @pl.when(m == nb)
    def _():
        x = xt[...]
        u = _dot(x, wcs[...]) * _dot(x, wvs[...])
        b = _dot(x, wbs[...])
        gt[...] = (b * (w0 * p0[...] + w1 * p1[...] + w2 * u)).astype(gt.dtype)
        ut[...] = u[0:8, :]


def _conv_mixer(hm, ht, w_in, conv_w, layer, prev0, prev1, tm, tn, seq_len):
    npr, d = hm.shape
    nb = npr // tm
    nn = d // tn
    xmain = pl.BlockSpec((tm, d), lambda n, m: (jnp.minimum(m, nb - 1), 0))
    xtail = pl.BlockSpec((TAIL, d), lambda n, m: (0, 0))
    omain = pl.BlockSpec((tm, tn), lambda n, m: (jnp.minimum(m, nb - 1), n))
    otail = pl.BlockSpec((TAIL, tn), lambda n, m: (0, n))

    def wspec(part):
        return pl.BlockSpec((None, d, tn), lambda n, m: (layer, 0, part * nn + n))

    return pl.pallas_call(
        functools.partial(_conv_body, nb=nb, bps=seq_len // tm),
        grid=(nn, nb + 1),
        in_specs=[xmain, xtail, wspec(0), wspec(1), wspec(2),
                  pl.BlockSpec((None, CONV_W, tn), lambda n, m: (layer, 0, n)), otail, otail],
        out_specs=[omain, otail, pl.BlockSpec((8, tn), lambda n, m: (m, n))],
        out_shape=[jax.ShapeDtypeStruct((npr, d), BF16), jax.ShapeDtypeStruct((TAIL, d), BF16),
                   jax.ShapeDtypeStruct(((nb + 1) * 8, d), F32)],
        scratch_shapes=[pltpu.VMEM((d, tn), BF16)] * 3 + [pltpu.VMEM((8, tn), F32)],
        compiler_params=_params("arbitrary", "arbitrary"), name="conv_mixer",
    )(hm, ht, w_in, w_in, w_in, conv_w, prev0, prev1)


def _ffn_rows(x, wgs, wus, wds):
    g = _dot(x, wgs[...])
    a = (g * jax.nn.sigmoid(g) * _dot(x, wus[...])).astype(BF16)
    return _dot(a, wds[...])


def _cast_weights(wg, wu, wd, wgs, wus, wds):
    wgs[...] = wg[...].astype(BF16)
    wus[...] = wu[...].astype(BF16)
    wds[...] = wd[...].astype(BF16)


def _ffn_dense_body(xm, xt, wg, wu, wd, om, ot, wgs, wus, wds, *, nb):
    c = pl.program_id(0)
    f = pl.program_id(1)

    @pl.when(c < nb)
    def _():
        @pl.when(f == 0)
        def _():
            om[...] = jnp.zeros_like(om)

        _cast_weights(wg, wu, wd, wgs, wus, wds)
        om[...] += _ffn_rows(xm[...], wgs, wus, wds)

    @pl.when(c == nb)
    def _():
        @pl.when(f == 0)
        def _():
            ot[...] = jnp.zeros_like(ot)

        _cast_weights(wg, wu, wd, wgs, wus, wds)
        ot[...] += _ffn_rows(xt[...], wgs, wus, wds)


def _ffn_dense(hm, ht, w_gate, w_up, w_down, layer, tm, tf):
    npr, d = hm.shape
    nb = npr // tm
    ff = w_gate.shape[-1]
    main = pl.BlockSpec((tm, d), lambda c, f: (jnp.minimum(c, nb - 1), 0))
    tail = pl.BlockSpec((TAIL, d), lambda c, f: (0, 0))
    return pl.pallas_call(
        functools.partial(_ffn_dense_body, nb=nb),
        grid=(nb + 1, ff // tf),
        in_specs=[main, tail,
                  pl.BlockSpec((None, d, tf), lambda c, f: (layer, 0, f)),
                  pl.BlockSpec((None, d, tf), lambda c, f: (layer, 0, f)),
                  pl.BlockSpec((None, tf, d), lambda c, f: (layer, f, 0))],
        out_specs=[main, tail],
        out_shape=[jax.ShapeDtypeStruct((npr, d), F32), jax.ShapeDtypeStruct((TAIL, d), F32)],
        scratch_shapes=[pltpu.VMEM((d, tf), BF16), pltpu.VMEM((d, tf), BF16), pltpu.VMEM((tf, d), BF16)],
        compiler_params=_params("arbitrary", "arbitrary"), name="ffn_dense",
    )(hm, ht, w_gate, w_up, w_down)


def _ffn_moe_body(ce, cn, tok, h, wg, wu, wd, o, gbuf, xsb, wgs, wus, wds, sem, *, sub, per_step):
    c = pl.program_id(0)
    f = pl.program_id(1)
    rows, d = o.shape
    pk = d // (2 * LANE)
    nv = cn[c]
    slot = lax.rem(c, 2)

    def issue(chunk, first, count, s):
        def body(i, carry):
            for k in range(2):
                r = first + 2 * i + k
                t = tok[chunk * rows + r]
                pltpu.make_async_copy(h.at[pl.ds(t * pk, pk)], gbuf.at[s, pl.ds(r * pk, pk)],
                                      sem.at[s]).start(priority=k)
            return carry

        assert count % 2 == 0
        lax.fori_loop(0, count // 2, body, 0, unroll=4)

    @pl.when(jnp.logical_and(jnp.logical_and(c == 0, f == 0), nv > 0))
    def _():
        issue(0, 0, rows, 0)

    @pl.when(f == 0)
    def _():
        o[...] = jnp.zeros_like(o)

        @pl.when(nv > 0)
        def _():
            pltpu.make_async_copy(h.at[pl.ds(0, rows * pk)], gbuf.at[slot], sem.at[slot]).wait()
            for j in range(pk):
                w = gbuf[slot, pl.ds(j, rows, stride=pk), :]
                lo = lax.bitcast_convert_type(lax.shift_left(w, jnp.uint32(16)), F32)
                hi = lax.bitcast_convert_type(jnp.bitwise_and(w, jnp.uint32(0xFFFF0000)), F32)
                xsb[:, j * LANE:(j + 1) * LANE] = lo.astype(BF16)
                xsb[:, d // 2 + j * LANE:d // 2 + (j + 1) * LANE] = hi.astype(BF16)

    @pl.when(jnp.logical_and(f < rows // per_step, cn[c + 1] > 0))
    def _():
        issue(c + 1, f * per_step, per_step, 1 - slot)

    @pl.when(nv == rows)
    def _():
        _cast_weights(wg, wu, wd, wgs, wus, wds)
        o[...] += _ffn_rows(xsb[...], wgs, wus, wds)

    @pl.when(jnp.logical_and(nv > 0, nv < rows))
    def _():
        _cast_weights(wg, wu, wd, wgs, wus, wds)

        def step(i, carry):
            r = pl.ds(pl.multiple_of(i * sub, sub), sub)
            o[r, :] += _ffn_rows(xsb[r, :], wgs, wus, wds)
            return carry

        lax.fori_loop(0, (nv + sub - 1) // sub, step, 0)


def _ffn_moe(h, tok_of_slot, chunk_expert, chunk_rows, w_gate, w_up, w_down, layer, d, rows, tf, sub):
    p = tok_of_slot.shape[0]
    n_chunks = p // rows
    ff = w_gate.shape[-1]
    nf = ff // tf
    pk = d // (2 * LANE)
    per_step = next(q for q in range(-(-rows // nf), rows + 1) if rows % q == 0)

    def fsel(c, f, cn):
        return jnp.where(cn[c] > 0, f, nf - 1)

    return pl.pallas_call(
        functools.partial(_ffn_moe_body, sub=sub, per_step=per_step),
        grid_spec=pltpu.PrefetchScalarGridSpec(
            num_scalar_prefetch=3, grid=(n_chunks, nf),
            in_specs=[pl.BlockSpec(memory_space=pl.ANY),
                      pl.BlockSpec((None, None, d, tf),
                                   lambda c, f, ce, cn, tok: (layer, ce[c], 0, fsel(c, f, cn))),
                      pl.BlockSpec((None, None, d, tf),
                                   lambda c, f, ce, cn, tok: (layer, ce[c], 0, fsel(c, f, cn))),
                      pl.BlockSpec((None, None, tf, d),
                                   lambda c, f, ce, cn, tok: (layer, ce[c], fsel(c, f, cn), 0))],
            out_specs=pl.BlockSpec((rows, d), lambda c, f, ce, cn, tok: (c, 0)),
            scratch_shapes=[pltpu.VMEM((2, rows * pk, LANE), jnp.uint32), pltpu.VMEM((rows, d), BF16),
                            pltpu.VMEM((d, tf), BF16), pltpu.VMEM((d, tf), BF16), pltpu.VMEM((tf, d), BF16),
                            pltpu.SemaphoreType.DMA((2,))]),
        out_shape=jax.ShapeDtypeStruct((p, d), F32),
        compiler_params=_params("arbitrary", "arbitrary"), name="ffn_moe",
    )(chunk_expert, jnp.pad(chunk_rows, (0, 1)), tok_of_slot, h, w_gate, w_up, w_down)


def _row_copy(src, src_row, dst, dst_row, sem):
    return pltpu.make_async_copy(src.at[pl.ds(src_row, 1)], dst.at[pl.ds(dst_row, 1)], sem)


def _combine_body(s1, s2, xm, xt, gm, gt, yb, om, ot, abuf, bbuf, sem, *, nb, tb, npr):
    m = pl.program_id(0)
    slot = lax.rem(m, 2)

    def gather(base, rows, s):
        def issue(r, carry):
            _row_copy(yb, s1[base + r], abuf.at[s], r, sem.at[s]).start(priority=0)
            _row_copy(yb, s2[base + r], bbuf.at[s], r, sem.at[s]).start(priority=1)
            return carry

        lax.fori_loop(0, rows, issue, 0, unroll=8)

    def run(x_ref, g_ref, o_ref, rows):
        for buf in (abuf, bbuf):
            pltpu.make_async_copy(yb.at[pl.ds(0, rows)], buf.at[slot, pl.ds(0, rows)], sem.at[slot]).wait()
        g = g_ref[...]
        o_ref[...] = x_ref[...] + g[:, 0:1] * abuf[slot, 0:rows, :] + g[:, 1:2] * bbuf[slot, 0:rows, :]

    @pl.when(m == 0)
    def _():
        gather(0, tb, 0)

    @pl.when(m + 1 < nb)
    def _():
        gather((m + 1) * tb, tb, 1 - slot)

    @pl.when(m + 1 == nb)
    def _():
        gather(npr, TAIL, 1 - slot)

    @pl.when(m < nb)
    def _():
        run(xm, gm, om, tb)

    @pl.when(m == nb)
    def _():
        run(xt, gt, ot, TAIL)


def _combine(xm, xt, gm, gt, yb, slot1, slot2, tb):
    npr, d = xm.shape
    nb = npr // tb
    main = pl.BlockSpec((tb, d), lambda m, s1, s2: (jnp.minimum(m, nb - 1), 0))
    tail = pl.BlockSpec((TAIL, d), lambda m, s1, s2: (0, 0))
    gmain = pl.BlockSpec((tb, LANE), lambda m, s1, s2: (jnp.minimum(m, nb - 1), 0))
    gtail = pl.BlockSpec((TAIL, LANE), lambda m, s1, s2: (0, 0))
    return pl.pallas_call(
        functools.partial(_combine_body, nb=nb, tb=tb, npr=npr),
        grid_spec=pltpu.PrefetchScalarGridSpec(
            num_scalar_prefetch=2, grid=(nb + 1,),
            in_specs=[main, tail, gmain, gtail, pl.BlockSpec(memory_space=pl.ANY)],
            out_specs=[main, tail],
            scratch_shapes=[pltpu.VMEM((2, tb, d), F32), pltpu.VMEM((2, tb, d), F32),
                            pltpu.SemaphoreType.DMA((2,))]),
        out_shape=[jax.ShapeDtypeStruct((npr, d), F32), jax.ShapeDtypeStruct((TAIL, d), F32)],
        compiler_params=_params("arbitrary"), name="moe_combine",
    )(slot1, slot2, xm, xt, gm, gt, yb)


def _route(route_i, route_i_tail, n_tok, n_exp, rows):
    npr = route_i.shape[0]
    ids = jnp.concatenate([route_i[:, :TOP_K], route_i_tail[:n_tok - npr, :TOP_K]], axis=0)
    flat_e = ids.reshape(-1)
    onehot = (flat_e[:, None] == jnp.arange(n_exp, dtype=jnp.int32)[None, :]).astype(jnp.int32)
    before = jnp.cumsum(onehot, axis=0) - onehot
    rank = jnp.sum(before * onehot, axis=1)
    counts = jnp.sum(onehot, axis=0)
    seg = (counts + rows - 1) // rows * rows
    seg_end = jnp.cumsum(seg)
    seg_start = seg_end - seg
    slot = (seg_start[flat_e] + rank).astype(jnp.int32)
    n_chunks = (n_tok * TOP_K) // rows + n_exp
    tok = jnp.repeat(jnp.arange(n_tok, dtype=jnp.int32), TOP_K)
    tok_of_slot = jnp.zeros((n_chunks * rows,), jnp.int32).at[slot].set(tok)
    chunk_start = jnp.arange(n_chunks, dtype=jnp.int32) * rows
    chunk_e = jnp.minimum(jnp.searchsorted(seg_end, chunk_start, side="right"), n_exp - 1).astype(jnp.int32)
    chunk_rows = jnp.clip(counts[chunk_e] - (chunk_start - seg_start[chunk_e]), 0, rows).astype(jnp.int32)
    last_used = jnp.max(jnp.where(chunk_rows > 0, jnp.arange(n_chunks), 0))
    chunk_e = jnp.where(chunk_rows > 0, chunk_e, chunk_e[last_used]).astype(jnp.int32)
    slots = slot.reshape(n_tok, TOP_K)
    pad = npr + TAIL - n_tok
    slot1 = jnp.pad(slots[:, 0], (0, pad))
    slot2 = jnp.pad(slots[:, 1], (0, pad))
    return tok_of_slot, chunk_e, chunk_rows, slot1, slot2


def _lam(lam_ref, lam0):
    lp = lam_ref[...]
    a = jnp.sum(lp[0:1, :] * lp[1:2, :], axis=-1, keepdims=True)
    b = jnp.sum(lp[2:3, :] * lp[3:4, :], axis=-1, keepdims=True)
    return jnp.exp(a) - jnp.exp(b) + lam0


def _attn_body(q_ref, k_ref, v_ref, sl_ref, lam_ref, sg_ref, o_ref, vt_sc, kp_sc, s_a, s_b, p_a, p_b, al_a, al_b,
               m_sc, l_sc, acc_sc, *, blk, lam0):
    i = pl.program_id(2)
    rows = 64

    @pl.when(i == 0)
    def _():
        lane = lax.broadcasted_iota(jnp.int32, (blk, HEAD_DIM), 1)
        for c in range(vt_sc.shape[0]):
            vt_sc[c] = v_ref[c * blk:(c + 1) * blk, :].T.astype(BF16)
            pos = c * blk + lax.broadcasted_iota(jnp.int32, (blk, HEAD_DIM), 0)
            hi = lax.shift_right_logical(pos, 6)
            lo = jnp.bitwise_and(pos, 63)
            digits = jnp.where(lane < 4, jnp.where(jnp.bitwise_and(lane, 1) == 0, hi, lo), 0)
            kp_sc[c] = digits.astype(F32).astype(BF16)

    pos_cols = jnp.broadcast_to(sl_ref[...], (blk, HEAD_DIM)).astype(BF16)
    q = [jnp.concatenate([q_ref[:, c * HEAD_DIM:(c + 1) * HEAD_DIM], pos_cols], axis=1) for c in range(2)]
    m_sc[...] = jnp.full_like(m_sc, NEG_INF)
    l_sc[...] = jnp.zeros_like(l_sc)
    acc_sc[...] = jnp.zeros_like(acc_sc)
    p_b[...] = jnp.zeros_like(p_b)
    al_b[...] = jnp.ones_like(al_b)

    def scores(j, s_ref):
        k = k_ref[pl.ds(pl.multiple_of(j * blk, blk), blk), :]
        kp = kp_sc[j]
        for c in range(2):
            kc = jnp.concatenate([k[:, c * HEAD_DIM:(c + 1) * HEAD_DIM], kp], axis=1)
            s_ref[c] = _dot_nt(kc, q[c])

    def softmax(j, s_ref, p_ref, al_ref, masked):
        def piece(c, r):
            s = s_ref[c, r * rows:(r + 1) * rows, :]
            if masked:
                ahead = (lax.broadcasted_iota(jnp.int32, s.shape, 0) - lax.broadcasted_iota(jnp.int32, s.shape, 1)
                         + (r * rows + (j - i) * blk))
                s = jnp.where(ahead <= 0, s, NEG_INF)
            return s

        for c in range(2):
            top = piece(c, 0).reshape(rows // 8, 8, blk).max(axis=0)
            for r in range(1, blk // rows):
                top = jnp.maximum(top, piece(c, r).reshape(rows // 8, 8, blk).max(axis=0))
            m_old = m_sc[c]
            m_new = jnp.maximum(m_old, jnp.max(top, axis=0, keepdims=True))
            alpha = jnp.exp2(m_old - m_new)
            total = jnp.zeros((8, blk), F32)
            for r in range(blk // rows):
                p = jnp.exp2(piece(c, r) - m_new)
                total = total + p.reshape(rows // 8, 8, blk).sum(axis=0)
                p_ref[c, r * rows:(r + 1) * rows, :] = p.astype(BF16)
            l_sc[c] = alpha * l_sc[c] + jnp.sum(total, axis=0, keepdims=True)
            m_sc[c] = m_new
            al_ref[c] = alpha

    def values(j, p_ref, al_ref):
        vt = vt_sc[j]
        for c in range(2):
            acc_sc[c] = al_ref[c] * acc_sc[c] + _dot(vt, p_ref[c])

    n_pairs = (i + 2) // 2
    scores(0, s_a)

    def pair(t, carry):
        values(jnp.maximum(2 * t - 1, 0), p_b, al_b)
        softmax(2 * t, s_a, p_a, al_a, False)
        scores(2 * t + 1, s_b)
        values(2 * t, p_a, al_a)
        softmax(2 * t + 1, s_b, p_b, al_b, False)
        scores(2 * t + 2, s_a)
        return carry

    lax.fori_loop(0, n_pairs - 1, pair, 0)

    j0 = 2 * n_pairs - 2
    values(jnp.maximum(j0 - 1, 0), p_b, al_b)
    softmax(j0, s_a, p_a, al_a, True)

    @pl.when(j0 < i)
    def _():
        scores(i, s_b)
        values(j0, p_a, al_a)
        softmax(i, s_b, p_b, al_b, True)
        values(i, p_b, al_b)

    @pl.when(j0 == i)
    def _():
        values(j0, p_a, al_a)

    o = acc_sc[0] * (1.0 / l_sc[0]) - _lam(lam_ref, lam0) * (acc_sc[1] * (1.0 / l_sc[1]))
    ms = jnp.mean(o * o, axis=0, keepdims=True)
    o = (o * lax.rsqrt(ms + EPS)).T
    o_ref[...] = (o * sg_ref[...] * (1.0 - lam0)).astype(o_ref.dtype)


def _attn_prompt(q, k, v, slopes, lam_p, sub_gain, n_batch, seq_len, blk, lam0):
    npr, d = q.shape
    assert seq_len <= 64 * 64 and blk % 64 == 0
    s_hi = slopes.astype(BF16).astype(F32)
    s_lo = (slopes - s_hi).astype(BF16).astype(F32)
    pos_mult = jnp.pad(jnp.stack([64.0 * s_hi, s_hi, 64.0 * s_lo, s_lo], axis=1), ((0, 0), (0, LANE - 4)))
    slopes = pos_mult[:, None, :]
    n_heads = d // V_DIM
    nq = seq_len // blk
    kv_spec = pl.BlockSpec((seq_len, V_DIM), lambda b, h, i: (b, h))
    return pl.pallas_call(
        functools.partial(_attn_body, blk=blk, lam0=lam0),
        grid=(n_batch, n_heads, nq),
        in_specs=[pl.BlockSpec((blk, V_DIM), lambda b, h, i: (b * nq + i, h)), kv_spec, kv_spec,
                  pl.BlockSpec((None, 1, LANE), lambda b, h, i: (h, 0, 0)),
                  pl.BlockSpec((4, HEAD_DIM), lambda b, h, i: (0, 0)),
                  pl.BlockSpec((1, V_DIM), lambda b, h, i: (0, 0))],
        out_specs=pl.BlockSpec((blk, V_DIM), lambda b, h, i: (b * nq + i, h)),
        out_shape=jax.ShapeDtypeStruct((npr, d), BF16),
        scratch_shapes=[pltpu.VMEM((nq, V_DIM, blk), BF16), pltpu.VMEM((nq, blk, HEAD_DIM), BF16),
                        pltpu.VMEM((2, blk, blk), F32), pltpu.VMEM((2, blk, blk), F32),
                        pltpu.VMEM((2, blk, blk), BF16), pltpu.VMEM((2, blk, blk), BF16),
                        pltpu.VMEM((2, 1, blk), F32), pltpu.VMEM((2, 1, blk), F32),
                        pltpu.VMEM((2, 1, blk), F32), pltpu.VMEM((2, 1, blk), F32),
                        pltpu.VMEM((2, V_DIM, blk), F32)],
        compiler_params=_params("arbitrary", "arbitrary", "arbitrary"), name="attn_prompt",
    )(q, k, v, slopes, lam_p, sub_gain)


def _attn_sample_body(pt, q_ref, kn_ref, vn_ref, sl_ref, lam_ref, sg_ref, *refs, pps, page, q_pos, lam0):
    k_refs = refs[:pps]
    v_refs = refs[pps:2 * pps]
    o_ref, m_sc, l_sc, acc_sc = refs[2 * pps:]
    p = pl.program_id(1)
    n_rows = q_ref.shape[0]
    n_heads = n_rows // 2
    cols = page * n_heads
    row = lax.broadcasted_iota(jnp.int32, (n_rows, cols), 0)
    col = lax.broadcasted_iota(jnp.int32, (n_rows, cols), 1)
    own = jnp.bitwise_and(col, n_heads - 1) == jnp.bitwise_and(row, n_heads - 1)
    key = lax.shift_right_logical(lax.broadcasted_iota(jnp.int32, (1, cols), 1), n_heads.bit_length() - 1)
    slope = sl_ref[:, 0:1]

    @pl.when(p == 0)
    def _():
        m_sc[...] = jnp.full_like(m_sc, NEG_INF)
        l_sc[...] = jnp.zeros_like(l_sc)
        acc_sc[...] = jnp.zeros_like(acc_sc)

    def update(s, pv_fn):
        m_old = m_sc[...]
        m_new = jnp.maximum(m_old, jnp.max(s, axis=-1, keepdims=True))
        alpha = jnp.exp2(m_old - m_new)
        pr = jnp.exp2(s - m_new)
        l_sc[...] = alpha * l_sc[...] + jnp.sum(pr, axis=-1, keepdims=True)
        acc_sc[...] = alpha * acc_sc[...] + pv_fn(pr)
        m_sc[...] = m_new

    qb = q_ref[...].astype(BF16)
    scores = []
    for i in range(pps):
        s_c = [_dot_nt(qb, k_refs[i][pl.ds(c, cols, stride=2), :].astype(BF16)) for c in range(2)]
        s = jnp.concatenate([s_c[0][:n_heads], s_c[1][n_heads:]], axis=0)
        dist = (q_pos - ((p * pps + i) * page + key)).astype(F32)
        scores.append(jnp.where(own, s - slope * dist, NEG_INF))

    def weighted_values(pr):
        pr = pr.astype(BF16)
        return sum(_dot(pr[:, i * cols:(i + 1) * cols], v_refs[i][...].astype(BF16)) for i in range(pps))

    update(jnp.concatenate(scores, axis=1), weighted_values)

    @pl.when(p == pl.num_programs(1) - 1)
    def _():
        s_new = jnp.sum(q_ref[...] * kn_ref[...], axis=-1, keepdims=True)
        vn = vn_ref[...]
        update(s_new, lambda pr: pr * jnp.concatenate([vn, vn], axis=0))
        o_hc = acc_sc[...] * (1.0 / l_sc[...])
        o = o_hc[:n_heads] - _lam(lam_ref, lam0) * o_hc[n_heads:]
        o_ref[...] = _rms(o, sg_ref[...]) * (1.0 - lam0)


def _attn_sample(q, k_new, v_new, cache_k, cache_v, page_table, slope_rows, lam_p, sub_gain, pps, lam0):
    n_seq, n_rows, _ = q.shape
    n_heads = n_rows // 2
    assert n_heads & (n_heads - 1) == 0
    n_pages = page_table.shape[1]
    n_pool, page = cache_k.shape[:2]
    ck = cache_k.reshape(n_pool * page * n_rows, HEAD_DIM)
    cv = cache_v.reshape(n_pool * page * n_heads, V_DIM)
    qspec = pl.BlockSpec((None, n_rows, HEAD_DIM), lambda b, p, pt: (b, 0, 0))
    vspec = pl.BlockSpec((None, n_heads, V_DIM), lambda b, p, pt: (b, 0, 0))

    def page_idx(b, p, pt, i):
        return pt[b * n_pages + p * pps + i]

    k_specs = [pl.BlockSpec((page * n_rows, HEAD_DIM), lambda b, p, pt, i=i: (page_idx(b, p, pt, i), 0))
               for i in range(pps)]
    v_specs = [pl.BlockSpec((page * n_heads, V_DIM), lambda b, p, pt, i=i: (page_idx(b, p, pt, i), 0))
               for i in range(pps)]
    return pl.pallas_call(
        functools.partial(_attn_sample_body, pps=pps, page=page, q_pos=n_pages * page, lam0=lam0),
        grid_spec=pltpu.PrefetchScalarGridSpec(
            num_scalar_prefetch=1, grid=(n_seq, n_pages // pps),
            in_specs=[qspec, qspec, vspec,
                      pl.BlockSpec((n_rows, LANE), lambda b, p, pt: (0, 0)),
                      pl.BlockSpec((4, HEAD_DIM), lambda b, p, pt: (0, 0)),
                      pl.BlockSpec((1, V_DIM), lambda b, p, pt: (0, 0))] + k_specs + v_specs,
            out_specs=vspec,
            scratch_shapes=[pltpu.VMEM((n_rows, 1), F32), pltpu.VMEM((n_rows, 1), F32),
                            pltpu.VMEM((n_rows, V_DIM), F32)]),
        out_shape=jax.ShapeDtypeStruct((n_seq, n_heads, V_DIM), F32),
        compiler_params=_params("arbitrary", "arbitrary"), name="attn_sample",
    )(page_table.reshape(-1), q, k_new, v_new, slope_rows, lam_p, sub_gain, *([ck] * pps), *([cv] * pps))


def _tiles(seq_len, d_model, d_ff, d_ff_e):
    tm = min(1024, seq_len)
    return dict(
        tm=tm,
        tr=min(512, tm),
        tn=min(1024, d_model),
        tn_conv=min(256, d_model),
        tf=256,
        moe_rows=tm + tm // 16,
        sub=(tm + tm // 16) // 2,
        gather_rows=min(256, tm),
        attn_blk=min(512, seq_len),
        pages_per_step=8,
    )


def kernel(x_prompt, x_sample, state_conv, cache_k, cache_v, page_table, mix_norm, ffn_norm, a_w_in, a_conv,
           a_w_out, kv_norm, w_kv, k_norm, b_w_q, b_q_norm, b_lam, b_subln, b_w_o, ffn_w_gate, ffn_w_up,
           ffn_w_down, moe_router, moe_w_gate, moe_w_up, moe_w_down):
    n_batch, seq_len, d = x_prompt.shape
    n_seq = x_sample.shape[0]
    assert x_sample.shape[1] == 1 and n_seq <= TAIL
    depth = mix_norm.shape[0]
    n_a = a_w_in.shape[0]
    n_heads = d // V_DIM
    n_exp = moe_router.shape[-1]
    k_width = n_heads * 2 * HEAD_DIM
    npr = n_batch * seq_len
    n_tok = npr + n_seq
    t = _tiles(seq_len, d, ffn_w_gate.shape[-1], moe_w_gate.shape[-1])
    tm, tn, tr = t["tm"], t["tn"], t["tr"]
    nb = npr // tm
    bps = seq_len // tm

    xm = x_prompt.reshape(npr, d)
    xt = jnp.pad(x_sample.reshape(n_seq, d), ((0, TAIL - n_seq), (0, 0)))

    slopes = jnp.exp2(-8.0 * jnp.arange(1, n_heads + 1, dtype=F32) / n_heads) * LOG2E
    slope_rows = jnp.broadcast_to(jnp.tile(slopes, 2)[:, None], (2 * n_heads, LANE))
    k_gain = jnp.tile(k_norm.reshape(-1), n_heads)

    def comp_major(a):
        return a[:n_seq].reshape(n_seq, n_heads, 2, HEAD_DIM).transpose(0, 2, 1, 3).reshape(n_seq, 2 * n_heads, HEAD_DIM)

    def pad_tail(a):
        return jnp.pad(a, ((0, TAIL - n_seq), (0, 0)))

    conv_prompt, conv_sample = [], []
    pending = None
    k_f32 = v_f32 = k_bf = v_bf = None
    for l in range(depth):
        if pending is None:
            hm, ht = _norm(xm, xt, mix_norm[l], tr)
        else:
            hm, ht, xm, xt = _norm(xm, xt, mix_norm[l], tr, add=pending)
            pending = None
        if l < n_a:
            prev0 = pad_tail(state_conv[l, :, 0, :])
            prev1 = pad_tail(state_conv[l, :, 1, :])
            gm, gt, ut = _conv_mixer(hm, ht, a_w_in, a_conv, l, prev0, prev1, tm, t["tn_conv"], seq_len)
            ut = ut.reshape(nb + 1, 8, d)
            conv_prompt.append(ut[bps - 1:nb:bps, 6:8, :])
            conv_sample.append(jnp.stack([state_conv[l, :, 1, :], ut[nb, :n_seq, :]], axis=1))
            xm, xt = _matmul(gm, gt, a_w_out, l, 0, d, tm, tn, [(F32, F32)], res=(xm, xt))
        else:
            j = l - n_a
            if k_f32 is None:
                nm, nt = _norm(xm, xt, kv_norm, tr)
                k_f32, kt_f32, k_bf, _ = _matmul(nm, nt, w_kv, 0, 0, k_width, tm, tn,
                                                 [(F32, F32), (BF16, BF16)], gain=k_gain)
                v_f32, vt_f32 = _matmul(nm, nt, w_kv, 0, k_width, d, tm, tn, [(F32, F32)])
            lam0 = 0.8 - 0.6 * math.exp(-0.3 * l)
            q_gain = jnp.tile(b_q_norm[j].reshape(-1), n_heads) * (HEAD_DIM ** -0.5 * LOG2E)
            qm, qt = _matmul(hm, ht, b_w_q, j, 0, k_width, tm, tn, [(BF16, F32)], gain=q_gain)
            sub_gain = b_subln[j].reshape(1, V_DIM)
            om = _attn_prompt(qm, k_bf, v_f32, slopes, b_lam[j], sub_gain, n_batch, seq_len,
                              t["attn_blk"], lam0)
            os_ = _attn_sample(comp_major(qt), comp_major(kt_f32), vt_f32[:n_seq].reshape(n_seq, n_heads, V_DIM),
                               cache_k, cache_v, page_table, slope_rows, b_lam[j], sub_gain,
                               t["pages_per_step"], lam0)
            ot = pad_tail(os_.reshape(n_seq, d)).astype(BF16)
            xm, xt = _matmul(om, ot, b_w_o, j, 0, d, tm, tn, [(F32, F32)], res=(xm, xt))
        i = l // 2
        if l % 2 == 0:
            h2m, h2t = _norm(xm, xt, ffn_norm[l], tr)
            pending = _ffn_dense(h2m, h2t, ffn_w_gate, ffn_w_up, ffn_w_down, i, tm, t["tf"])
        else:
            h2, rim, rit, rgm, rgt = _norm_router(xm, xt, ffn_norm[l], moe_router[i], tr)
            tok_of_slot, chunk_e, chunk_rows, slot1, slot2 = _route(rim, rit, n_tok, n_exp, t["moe_rows"])
            yb = _ffn_moe(h2, tok_of_slot, chunk_e, chunk_rows, moe_w_gate, moe_w_up, moe_w_down, i,
                          d, t["moe_rows"], t["tf"], t["sub"])
            valid = (jnp.arange(TAIL) < n_seq)[:, None]
            xm, xt = _combine(xm, xt, rgm, jnp.where(valid, rgt, 0.0), yb, slot1, slot2, t["gather_rows"])
    if pending is not None:
        xm, xt = xm + pending[0], xt + pending[1]

    y_prompt = xm.reshape(n_batch, seq_len, d)
    y_sample = xt[:n_seq].reshape(n_seq, 1, d)
    return (y_prompt, y_sample, jnp.stack(conv_prompt), jnp.stack(conv_sample),
            k_f32.reshape(n_batch, seq_len, n_heads, 2, HEAD_DIM),
            v_f32.reshape(n_batch, seq_len, n_heads, V_DIM),
            kt_f32[:n_seq].reshape(n_seq, 1, n_heads, 2, HEAD_DIM),
            vt_f32[:n_seq].reshape(n_seq, 1, n_heads, V_DIM))
```
